```python
import math
import jax
import jax.numpy as jnp
from jax import lax
import numpy as np

D_MODEL = 1024
BATCH = 8
SEQ = 2048
DEPTH = 1
DEC_BATCH = 128
DEC_SEQ = 8
PAST_LEN = 16384
PAGE_SIZE = 128

RWKV_WIDTH = D_MODEL // 2
RWKV_HEAD = 64
RWKV_HEADS = RWKV_WIDTH // RWKV_HEAD
RWKV_DECAY_LORA = 64
RWKV_A_LORA = 64
RWKV_GATE_LORA = 128
RWKV_LN_EPS = 1e-5 * RWKV_HEAD
GLA_HEADS = 4
GLA_KEY_WIDTH = D_MODEL // 4
GLA_VALUE_WIDTH = D_MODEL // 2
GLA_DK = GLA_KEY_WIDTH // GLA_HEADS
GLA_DV = GLA_VALUE_WIDTH // GLA_HEADS
GLA_GATE_LORA = 16
GLA_GATE_TEMP = 16.0
GLA_CHUNK = 32
FFN_HIDDEN = ((8 * D_MODEL // 3 + 127) // 128) * 128
CONV_WIDTH = 3
NORM_EPS = 1e-6
SHIFT_COLS = 3 * RWKV_WIDTH + RWKV_DECAY_LORA + RWKV_A_LORA + RWKV_GATE_LORA
GLA_COLS = 2 * GLA_KEY_WIDTH + 2 * GLA_VALUE_WIDTH + GLA_GATE_LORA
GATE_COLS = 2 * D_MODEL
IN_COLS = SHIFT_COLS + GLA_COLS + GATE_COLS

kernel_name = "hybrid_rwkv7_gla_convffn_step"


def _split(t, sizes):
    return jnp.split(t, [int(s) for s in np.cumsum(sizes)[:-1]], axis=-1)


def _rmsnorm(x, g):
    xf = x.astype(jnp.float32)
    y = xf * lax.rsqrt(jnp.mean(xf * xf, axis=-1, keepdims=True) + NORM_EPS)
    return (y * g.astype(jnp.float32)).astype(x.dtype)


def _rwkv7_recurrence(r, decay, k, v, a, b, S0):
    def step(S, inp):
        r_t, w_t, k_t, v_t, a_t, b_t = inp
        sa = jnp.einsum("bhvk,bhk->bhv", S, a_t)
        S = S * w_t[:, :, None, :] + sa[..., None] * b_t[:, :, None, :] + v_t[..., None] * k_t[:, :, None, :]
        return S, jnp.einsum("bhvk,bhk->bhv", S, r_t)
    xs = tuple(jnp.moveaxis(t.astype(jnp.float32), 1, 0) for t in (r, decay, k, v, a, b))
    S, o = lax.scan(step, S0.astype(jnp.float32), xs)
    return jnp.moveaxis(o, 0, 1), S


def _gla_chunked(q, k, v, log_a, S0):
    B, T, H, _ = q.shape
    dv = v.shape[-1]
    C = math.gcd(T, GLA_CHUNK)
    N = T // C

    def chunks(t):
        return jnp.moveaxis(t.astype(jnp.float32).reshape(B, N, C, H, t.shape[-1]), 1, 0)

    causal = jnp.tril(jnp.ones((C, C), dtype=bool))

    def step(S, inp):
        qc, kc, vc, gc = inp
        cum = jnp.cumsum(gc, axis=1)
        last = cum[:, -1]
        q_dec = qc * jnp.exp(cum)
        k_inv = kc * jnp.exp(-cum)
        scores = jnp.where(causal, jnp.einsum("bchd,bshd->bhcs", q_dec, k_inv), 0.0)
        o = jnp.einsum("bhcs,bshv->bchv", scores, vc) + jnp.einsum("bchd,bhdv->bchv", q_dec, S)
        k_end = kc * jnp.exp(last[:, None] - cum)
        S = jnp.exp(last)[..., None] * S + jnp.einsum("bshd,bshv->bhdv", k_end, vc)
        return S, o

    S, o = lax.scan(step, S0.astype(jnp.float32), (chunks(q), chunks(k), chunks(v), chunks(log_a)))
    return jnp.moveaxis(o, 0, 1).reshape(B, T, H, dv), S


def _mixer(h, shift_prev, wkv0, gla0, p, l):
    B, T, _ = h.shape
    f32 = jnp.float32
    proj = h @ p["w_in"][l]
    p_rw, p_gla, p_gate = _split(proj, [SHIFT_COLS, GLA_COLS, GATE_COLS])

    prev = jnp.concatenate([shift_prev[:, None].astype(p_rw.dtype), p_rw[:, :-1]], axis=1)
    xs = p_rw + (prev - p_rw) * p["mu_shift"][l]
    new_shift = p_rw[:, -1]
    r, k, v, lw, la, lg = _split(xs, [RWKV_WIDTH] * 3 + [RWKV_DECAY_LORA, RWKV_A_LORA, RWKV_GATE_LORA])
    w = -jax.nn.softplus(-(p["rwkv_w0"][l] + jnp.tanh(lw) @ p["rwkv_w2"][l]).astype(f32)) - 0.5
    decay = jnp.exp(-jnp.exp(w))
    a = jax.nn.sigmoid((p["rwkv_a0"][l] + la @ p["rwkv_a2"][l]).astype(f32))
    g = jax.nn.sigmoid(lg) @ p["rwkv_g2"][l]

    def heads(t):
        return t.reshape(B, T, RWKV_HEADS, RWKV_HEAD)

    rh = heads(r.astype(f32))
    vh = heads(v.astype(f32))
    ah = heads(a)
    kk = heads((k * p["rwkv_k_k"][l]).astype(f32))
    kk = kk / jnp.maximum(jnp.sqrt(jnp.sum(kk * kk, axis=-1, keepdims=True)), 1e-12)
    kh = heads(k.astype(f32) * (1.0 + (a - 1.0) * p["rwkv_k_a"][l].astype(f32)))
    o, wkv = _rwkv7_recurrence(rh, heads(decay), kh, vh, -kk, kk * ah, wkv0)
    mu = jnp.mean(o, axis=-1, keepdims=True)
    var = jnp.mean(jnp.square(o - mu), axis=-1, keepdims=True)
    o = ((o - mu) * lax.rsqrt(var + RWKV_LN_EPS)).reshape(B, T, RWKV_WIDTH)
    o = o * p["rwkv_ln_w"][l] + p["rwkv_ln_b"][l]
    bonus = jnp.sum(rh * kh * p["rwkv_r_k"][l].astype(f32), axis=-1, keepdims=True) * vh
    o = o + bonus.reshape(B, T, RWKV_WIDTH)
    y_a = (o.astype(h.dtype) * g) @ p["w_out_a"][l]

    q, kg, vg, lga, og = _split(p_gla, [GLA_KEY_WIDTH, GLA_KEY_WIDTH, GLA_VALUE_WIDTH, GLA_GATE_LORA, GLA_VALUE_WIDTH])
    log_a = jax.nn.log_sigmoid((lga @ p["gla_wg2"][l] + p["gla_bg"][l]).astype(f32)) / GLA_GATE_TEMP
    ob, gla_state = _gla_chunked(
        (q * GLA_DK ** -0.5).reshape(B, T, GLA_HEADS, GLA_DK),
        kg.reshape(B, T, GLA_HEADS, GLA_DK),
        vg.reshape(B, T, GLA_HEADS, GLA_DV),
        log_a.reshape(B, T, GLA_HEADS, GLA_DK),
        gla0)
    ob = ob * lax.rsqrt(jnp.mean(ob * ob, axis=-1, keepdims=True) + NORM_EPS) * p["gla_norm_w"][l]
    ob = ob.reshape(B, T, GLA_VALUE_WIDTH) * jax.nn.silu(og.astype(f32))
    y_b = ob.astype(h.dtype) @ p["w_out_b"][l]

    gate_a, gate_b = _split(p_gate, [D_MODEL, D_MODEL])
    merged = jax.nn.sigmoid(gate_a) * y_a + jax.nn.sigmoid(gate_b) * y_b
    return merged @ p["w_o"][l], new_shift, wkv.astype(wkv0.dtype), gla_state.astype(gla0.dtype)


def _conv_ffn(h, conv_prev, p, l):
    T = h.shape[1]
    u = h @ p["ffn_w_up"][l]
    ext = jnp.concatenate([conv_prev.astype(u.dtype), u], axis=1)
    w = p["ffn_conv_w"][l]
    c = p["ffn_conv_b"][l] + w[0] * ext[:, 0:T]
    for j in range(1, CONV_WIDTH):
        c = c + w[j] * ext[:, j:j + T]
    val, gate = _split(c, [FFN_HIDDEN, FFN_HIDDEN])
    y = (jax.nn.gelu(gate) * val) @ p["ffn_w_down"][l]
    return y, ext[:, -(CONV_WIDTH - 1):]


def _trunk(x, shift0, wkv0, gla0, conv0, p):
    s_shift, s_wkv, s_gla, s_conv = [], [], [], []
    for l in range(DEPTH):
        y, n_shift, n_wkv, n_gla = _mixer(_rmsnorm(x, p["norm_mix"][l]), shift0[l], wkv0[l], gla0[l], p, l)
        x = x + y
        y, n_conv = _conv_ffn(_rmsnorm(x, p["norm_ffn"][l]), conv0[l], p, l)
        x = x + y
        s_shift.append(n_shift)
        s_wkv.append(n_wkv)
        s_gla.append(n_gla)
        s_conv.append(n_conv)
    return (_rmsnorm(x, p["norm_final"]), jnp.stack(s_shift), jnp.stack(s_wkv),
            jnp.stack(s_gla), jnp.stack(s_conv))


def setup_inputs(seed: int = 0) -> dict:
    key = jax.random.key(seed)
    ks = iter(jax.random.split(key, 40))

    def nrm(shape, scale):
        return scale * jax.random.normal(next(ks), shape, jnp.float32)

    def unif(shape, lo, hi):
        return jax.random.uniform(next(ks), shape, jnp.float32, lo, hi)

    L = DEPTH
    F2 = 2 * FFN_HIDDEN
    return {
        "x_prompt": nrm((BATCH, SEQ, D_MODEL), 1.0),
        "x_sample": nrm((DEC_BATCH, DEC_SEQ, D_MODEL), 1.0),
        "state_rwkv_shift": nrm((L, DEC_BATCH, SHIFT_COLS), 1.0),
        "state_rwkv_wkv": nrm((L, DEC_BATCH, RWKV_HEADS, RWKV_HEAD, RWKV_HEAD), 0.3),
        "state_gla": nrm((L, DEC_BATCH, GLA_HEADS, GLA_DK, GLA_DV), 0.3),
        "state_ffn_conv": nrm((L, DEC_BATCH, CONV_WIDTH - 1, F2), 1.0),
        "norm_mix": 1.0 + nrm((L, D_MODEL), 0.05),
        "w_in": nrm((L, D_MODEL, IN_COLS), D_MODEL ** -0.5),
        "mu_shift": unif((L, SHIFT_COLS), 0.0, 1.0),
        "rwkv_w0": unif((L, RWKV_WIDTH), -4.0, 0.0),
        "rwkv_w2": nrm((L, RWKV_DECAY_LORA, RWKV_WIDTH), 0.5 * RWKV_DECAY_LORA ** -0.5),
        "rwkv_a0": nrm((L, RWKV_WIDTH), 0.1),
        "rwkv_a2": nrm((L, RWKV_A_LORA, RWKV_WIDTH), RWKV_A_LORA ** -0.5),
        "rwkv_g2": nrm((L, RWKV_GATE_LORA, RWKV_WIDTH), RWKV_GATE_LORA ** -0.5),
        "rwkv_k_k": 0.85 + nrm((L, RWKV_WIDTH), 0.05),
        "rwkv_k_a": 1.0 + nrm((L, RWKV_WIDTH), 0.05),
        "rwkv_r_k": nrm((L, RWKV_HEADS, RWKV_HEAD), 0.1),
        "rwkv_ln_w": 1.0 + nrm((L, RWKV_WIDTH), 0.05),
        "rwkv_ln_b": nrm((L, RWKV_WIDTH), 0.02),
        "gla_wg2": nrm((L, GLA_GATE_LORA, GLA_KEY_WIDTH), GLA_GATE_LORA ** -0.5),
        "gla_bg": 1.0 + nrm((L, GLA_KEY_WIDTH), 0.5),
        "gla_norm_w": 1.0 + nrm((L, GLA_DV), 0.05),
        "w_out_a": nrm((L, RWKV_WIDTH, D_MODEL), RWKV_WIDTH ** -0.5),
        "w_out_b": nrm((L, GLA_VALUE_WIDTH, D_MODEL), GLA_VALUE_WIDTH ** -0.5),
        "w_o": nrm((L, D_MODEL, D_MODEL), D_MODEL ** -0.5),
        "norm_ffn": 1.0 + nrm((L, D_MODEL), 0.05),
        "ffn_w_up": nrm((L, D_MODEL, F2), D_MODEL ** -0.5),
        "ffn_conv_w": nrm((L, CONV_WIDTH, F2), CONV_WIDTH ** -0.5),
        "ffn_conv_b": nrm((L, F2), 0.02),
        "ffn_w_down": nrm((L, FFN_HIDDEN, D_MODEL), FFN_HIDDEN ** -0.5),
        "norm_final": 1.0 + nrm((D_MODEL,), 0.05),
    }


def reference(x_prompt, x_sample, state_rwkv_shift, state_rwkv_wkv, state_gla, state_ffn_conv,
              norm_mix, w_in, mu_shift, rwkv_w0, rwkv_w2, rwkv_a0, rwkv_a2, rwkv_g2,
              rwkv_k_k, rwkv_k_a, rwkv_r_k, rwkv_ln_w, rwkv_ln_b, gla_wg2, gla_bg, gla_norm_w,
              w_out_a, w_out_b, w_o, norm_ffn, ffn_w_up, ffn_conv_w, ffn_conv_b, ffn_w_down,
              norm_final):
    params = dict(norm_mix=norm_mix, w_in=w_in, mu_shift=mu_shift, rwkv_w0=rwkv_w0,
                  rwkv_w2=rwkv_w2, rwkv_a0=rwkv_a0, rwkv_a2=rwkv_a2, rwkv_g2=rwkv_g2,
                  rwkv_k_k=rwkv_k_k, rwkv_k_a=rwkv_k_a, rwkv_r_k=rwkv_r_k, rwkv_ln_w=rwkv_ln_w,
                  rwkv_ln_b=rwkv_ln_b, gla_wg2=gla_wg2, gla_bg=gla_bg, gla_norm_w=gla_norm_w,
                  w_out_a=w_out_a, w_out_b=w_out_b, w_o=w_o, norm_ffn=norm_ffn,
                  ffn_w_up=ffn_w_up, ffn_conv_w=ffn_conv_w, ffn_conv_b=ffn_conv_b,
                  ffn_w_down=ffn_w_down, norm_final=norm_final)
    bp = x_prompt.shape[0]
    dt = x_prompt.dtype
    zero_shift = jnp.zeros((DEPTH, bp, SHIFT_COLS), dt)
    zero_wkv = jnp.zeros((DEPTH, bp, RWKV_HEADS, RWKV_HEAD, RWKV_HEAD), dt)
    zero_gla = jnp.zeros((DEPTH, bp, GLA_HEADS, GLA_DK, GLA_DV), dt)
    zero_conv = jnp.zeros((DEPTH, bp, CONV_WIDTH - 1, 2 * FFN_HIDDEN), dt)
    y_p, shift_p, wkv_p, gla_p, conv_p = _trunk(x_prompt, zero_shift, zero_wkv, zero_gla, zero_conv, params)
    y_s, shift_s, wkv_s, gla_s, conv_s = _trunk(x_sample, state_rwkv_shift, state_rwkv_wkv, state_gla,
                                                state_ffn_conv, params)
    return (y_p, y_s, shift_p, wkv_p, gla_p, conv_p, shift_s, wkv_s, gla_s, conv_s)
```

```python
import functools
import math

import jax
import jax.numpy as jnp
from jax import lax
from jax.experimental import pallas as pl
from jax.experimental.pallas import tpu as pltpu

F32 = jnp.float32
BF16 = jnp.bfloat16
LANES = 128
TB = 128
TM = 256
NCHUNK = 512
VMEM_LIMIT = 56 * 1024 * 1024
NORM_EPS = 1e-6
HEAD = 64
GLA_GATE_TEMP = 16.0
PROMPT_CHUNK = 16
NN = ((1,), (0,))
NT = ((1,), (1,))
TN = ((0,), (0,))


def _dot(a, b, dims=NN):
    return lax.dot_general(a, b, (dims, ((), ())), preferred_element_type=F32)


def _split(x):
    hi = x.astype(BF16)
    lo = (x - hi.astype(F32)).astype(BF16)
    return hi, lo


def _dot3(a, b, dims=NN):
    ah, al = _split(a)
    bh, bl = _split(b)
    return _dot(ah, bh, dims) + (_dot(ah, bl, dims) + _dot(al, bh, dims))


def _split3(x):
    a1 = x.astype(BF16)
    r1 = x - a1.astype(F32)
    a2 = r1.astype(BF16)
    a3 = (r1 - a2.astype(F32)).astype(BF16)
    return a1, a2, a3


def _dot_exact_lhs(e, x):
    x1, x2, x3 = _split3(x)
    return _dot(e, x1) + (_dot(e, x2) + _dot(e, x3))


def _dot_exact_rhs(x, e):
    x1, x2, x3 = _split3(x)
    return _dot(x1, e) + (_dot(x2, e) + _dot(x3, e))


def _iota2(shape, dim):
    return lax.broadcasted_iota(jnp.int32, shape, dim)


def _same_block(n, size):
    sh = int(math.log2(size))
    return (_iota2((n, n), 0) >> sh) == (_iota2((n, n), 1) >> sh)


def _ones_where(mask):
    return jnp.where(mask, 1.0, 0.0).astype(BF16)


def _softplus(y):
    return jnp.maximum(y, 0.0) + jnp.log(1.0 + jnp.exp(-jnp.abs(y)))


def _rmsnorm(x, g):
    return x * lax.rsqrt(jnp.mean(x * x, axis=-1, keepdims=True) + NORM_EPS) * g


def _tile(x, p):
    return x[:, p * LANES:(p + 1) * LANES]


def _chunk_mats(chunk):
    same = _same_block(TB, chunk)
    r = _iota2((TB, TB), 0)
    c = _iota2((TB, TB), 1)
    incl = same & (c <= r)
    strict = same & (c < r)
    return incl, strict, _ones_where(incl), _ones_where(same)


def _half_masks():
    lane = _iota2((1, LANES), 1)
    lo = lane < HEAD
    return lo, jnp.logical_not(lo)


def _norm_proj_body(nw, x_ref, g_ref, *refs):
    w_refs, o_refs = refs[:nw], refs[nw:]
    hb = _rmsnorm(x_ref[...], g_ref[...]).astype(BF16)
    for w_ref, o_ref in zip(w_refs, o_refs):
        n = w_ref.shape[1]
        for n0 in range(0, n, NCHUNK):
            n1 = min(n0 + NCHUNK, n)
            o_ref[:, n0:n1] = _dot(hb, w_ref[:, n0:n1])


def _norm_proj(x, g, weights):
    m, d = x.shape
    nw = len(weights)
    return pl.pallas_call(
        functools.partial(_norm_proj_body, nw),
        grid=(m // TM,),
        in_specs=[pl.BlockSpec((TM, d), lambda i: (i, 0)), pl.BlockSpec((1, d), lambda i: (0, 0))]
        + [pl.BlockSpec(w.shape, lambda i: (0, 0)) for w in weights],
        out_specs=[pl.BlockSpec((TM, w.shape[1]), lambda i: (i, 0)) for w in weights],
        out_shape=[jax.ShapeDtypeStruct((m, w.shape[1]), F32) for w in weights],
        compiler_params=pltpu.CompilerParams(dimension_semantics=("parallel",), vmem_limit_bytes=VMEM_LIMIT),
        name="norm_proj",
    )(x, g, *weights)


def _inv_unit_lower(n, chunk, eye):
    p = eye + n
    x = n
    for _ in range(int(math.log2(chunk)) - 1):
        x = _dot3(x, x)
        p = p + _dot3(p, x)
    return p


def _rwkv_body(chunk, chained, x_ref, prev_ref, s0_ref, mu_ref, vec_ref, wlo_ref, g2_ref,
               z_ref, sout_ref, carry_ref, s_scr):
    nchunk = TB // chunk
    width = z_ref.shape[-1]
    npair = width // LANES
    x = x_ref[...]
    row = _iota2((TB, 1), 0)
    rolled = pltpu.roll(x, 1, axis=0)
    if chained:
        @pl.when(pl.program_id(1) == 0)
        def _():
            carry_ref[...] = prev_ref[...]
            s_scr[...] = s0_ref[...]
        prev = jnp.where(row == 0, carry_ref[...], rolled)
        carry_ref[...] = x[TB - 1:TB, :]
    else:
        prev = jnp.where((row & (chunk - 1)) == 0, prev_ref[...], rolled)
    xs = x + (prev - x) * mu_ref[...]

    r = xs[:, 0:width]
    k = xs[:, width:2 * width]
    v = xs[:, 2 * width:3 * width]
    lora = xs[:, 3 * width:3 * width + LANES]
    lg = xs[:, 3 * width + LANES:3 * width + 2 * LANES]
    lo_half, hi_half = _half_masks()
    lora = jnp.where(lo_half, jnp.tanh(lora), lora)
    wa = _dot(lora.astype(BF16), wlo_ref[...])
    w0, a0, k_k, k_a = vec_ref[0:1, :], vec_ref[1:2, :], vec_ref[2:3, :], vec_ref[3:4, :]
    r_k, ln_w, ln_b = vec_ref[4:5, :], vec_ref[5:6, :], vec_ref[6:7, :]
    wlog = -_softplus(-(w0 + wa[:, :width])) - 0.5
    logw = -jnp.exp(wlog)
    asig = jax.nn.sigmoid(a0 + wa[:, width:])
    g = _dot(jax.nn.sigmoid(lg).astype(BF16), g2_ref[...])

    seg = _ones_where(_same_block(LANES, HEAD))

    def headsum(t):
        return jnp.concatenate([_dot_exact_rhs(_tile(t, p), seg) for p in range(npair)], axis=1)

    kk = k * k_k
    kk = kk / jnp.maximum(jnp.sqrt(headsum(kk * kk)), 1e-12)
    kh = k * (1.0 + (asig - 1.0) * k_a)
    a_vec = -kk
    b_vec = kk * asig

    incl, strict, incl_m, same_m = _chunk_mats(chunk)
    cum = _dot_exact_lhs(incl_m, logw)
    cum_c = _dot_exact_lhs(same_m, logw)
    e_in = jnp.exp(cum)
    e_out = jnp.exp(-cum)
    e_end = jnp.exp(cum_c - cum)
    rt = r * e_in
    at = a_vec * jnp.exp(cum - logw)
    bt = b_vec * e_out
    kt = kh * e_out
    bp = b_vec * e_end
    kp = kh * e_end
    e_c = jnp.exp(cum_c)

    def msk(t, h):
        return jnp.where(lo_half if h == 0 else hi_half, t, 0.0)

    eye = jnp.where(_iota2((TB, TB), 0) == _iota2((TB, TB), 1), 1.0, 0.0)
    zero = jnp.zeros((TB, LANES), F32)
    ta, yy, qq, zz = [], [], [], []
    for p in range(npair):
        at_p, rt_p, bt_p, kt_p, v_p = (_tile(t, p) for t in (at, rt, bt, kt, v))
        sc = _dot3(jnp.concatenate([at_p, rt_p], axis=0),
                   jnp.concatenate([msk(bt_p, 0), msk(bt_p, 1), msk(kt_p, 0), msk(kt_p, 1)], axis=0), NT)
        a_ab = [jnp.where(strict, sc[:TB, h * TB:(h + 1) * TB], 0.0) for h in range(2)]
        a_ak = [jnp.where(strict, sc[:TB, (2 + h) * TB:(3 + h) * TB], 0.0) for h in range(2)]
        p_rb = [jnp.where(incl, sc[TB:, h * TB:(h + 1) * TB], 0.0) for h in range(2)]
        p_rk = [jnp.where(incl, sc[TB:, (2 + h) * TB:(3 + h) * TB], 0.0) for h in range(2)]
        t_inv = [_inv_unit_lower(a_ab[h], chunk, eye) for h in range(2)]
        av = _dot3(jnp.concatenate(a_ak, axis=1), jnp.concatenate([msk(v_p, 0), msk(v_p, 1)], axis=0))
        tay = _dot3(jnp.concatenate(t_inv, axis=1),
                    jnp.concatenate([jnp.concatenate([msk(at_p, h), msk(av, h)], axis=1) for h in range(2)], axis=0))
        ta_p, y_p = tay[:, :LANES], tay[:, LANES:]
        qz = _dot3(jnp.concatenate(p_rb + p_rk, axis=1),
                   jnp.concatenate([jnp.concatenate([msk(ta_p, h), msk(y_p, h)], axis=1) for h in range(2)]
                                   + [jnp.concatenate([zero, msk(v_p, h)], axis=1) for h in range(2)], axis=0))
        ta.append(ta_p)
        yy.append(y_p)
        qq.append(rt_p + qz[:, :LANES])
        zz.append(qz[:, LANES:])

    bd = _same_block(LANES, HEAD)
    o_tiles = []
    for p in range(npair):
        v_p, bp_p, kp_p, ec_p = (_tile(t, p) for t in (v, bp, kp, e_c))
        if chained:
            s = s_scr[p]
        o_rows = []
        for c in range(nchunk):
            sl = slice(c * chunk, (c + 1) * chunk)
            if not chained:
                s = s0_ref[c, p]
            uo = _dot3(jnp.concatenate([ta[p][sl], qq[p][sl]], axis=0), s, NT)
            u = uo[:chunk] + yy[p][sl]
            o_rows.append(uo[chunk:] + zz[p][sl])
            upd = _dot3(jnp.concatenate([u, v_p[sl]], axis=0),
                        jnp.concatenate([bp_p[sl], kp_p[sl]], axis=0), TN)
            s = s * ec_p[c * chunk:c * chunk + 1, :] + jnp.where(bd, upd, 0.0)
            if not chained:
                sout_ref[c, p] = s
        if chained:
            s_scr[p] = s
            sout_ref[p] = s
        o_tiles.append(jnp.concatenate(o_rows, axis=0))
    o = jnp.concatenate(o_tiles, axis=1)

    inv_n = 1.0 / HEAD
    mean = headsum(o) * inv_n
    d = o - mean
    var = headsum(d * d) * inv_n
    o = d * lax.rsqrt(var + 1e-5 * HEAD) * ln_w + ln_b
    o = o + headsum(r * kh * r_k) * v
    z_ref[...] = o * g


def _rwkv(x_rw, prev, s0, mu, vec, wlo, g2, seq_len):
    npair = s0.shape[1]
    width = npair * LANES
    cols = x_rw.shape[-1]
    chained = seq_len > TB
    if chained:
        chunk = PROMPT_CHUNK
        b, t, _ = x_rw.shape
        grid = (b, t // TB)
        in_specs = [pl.BlockSpec((None, TB, cols), lambda i, j: (i, j, 0)),
                    pl.BlockSpec((None, 1, cols), lambda i, j: (i, 0, 0)),
                    pl.BlockSpec((None, npair, LANES, LANES), lambda i, j: (i, 0, 0, 0))]
        out_specs = [pl.BlockSpec((None, TB, width), lambda i, j: (i, j, 0)),
                     pl.BlockSpec((None, npair, LANES, LANES), lambda i, j: (i, 0, 0, 0))]
        out_shape = [jax.ShapeDtypeStruct((b, t, width), F32), jax.ShapeDtypeStruct(s0.shape, F32)]
        sem = ("parallel", "arbitrary")
    else:
        chunk = seq_len
        nseq = TB // chunk
        m = x_rw.shape[1]
        grid = (m // TB, 1)
        in_specs = [pl.BlockSpec((None, TB, cols), lambda i, j: (0, i, 0)),
                    pl.BlockSpec((TB, cols), lambda i, j: (i, 0)),
                    pl.BlockSpec((nseq, npair, LANES, LANES), lambda i, j: (i, 0, 0, 0))]
        out_specs = [pl.BlockSpec((None, TB, width), lambda i, j: (0, i, 0)),
                     pl.BlockSpec((nseq, npair, LANES, LANES), lambda i, j: (i, 0, 0, 0))]
        out_shape = [jax.ShapeDtypeStruct((1, m, width), F32), jax.ShapeDtypeStruct(s0.shape, F32)]
        sem = ("parallel", "arbitrary")
    const = lambda a: pl.BlockSpec(a.shape, lambda i, j: (0,) * a.ndim)
    return pl.pallas_call(
        functools.partial(_rwkv_body, chunk, chained),
        grid=grid,
        in_specs=in_specs + [const(mu), const(vec), const(wlo), const(g2)],
        out_specs=out_specs,
        out_shape=out_shape,
        scratch_shapes=[pltpu.VMEM((1, cols), F32), pltpu.VMEM((npair, LANES, LANES), F32)],
        compiler_params=pltpu.CompilerParams(dimension_semantics=sem, vmem_limit_bytes=VMEM_LIMIT),
        name="rwkv7",
    )(x_rw, prev, s0, mu, vec, wlo, g2)


def _gla_body(chunk, chained, qkv_ref, lga_ref, og_ref, s0_ref, wg2_ref, bg_ref, nw_ref,
              z_ref, sout_ref, s_scr):
    nchunk = TB // chunk
    vw = z_ref.shape[-1]
    nhead = vw // LANES
    npair = nhead // 2
    kw = npair * LANES
    qkv = qkv_ref[...]
    q = qkv[:, 0:kw] * (HEAD ** -0.5)
    k = qkv[:, kw:2 * kw]
    v = qkv[:, 2 * kw:2 * kw + vw]
    gl = _dot(lga_ref[...].astype(BF16), wg2_ref[...]) + bg_ref[...]
    log_a = -_softplus(-gl) * (1.0 / GLA_GATE_TEMP)

    incl, _, incl_m, same_m = _chunk_mats(chunk)
    cum = _dot_exact_lhs(incl_m, log_a)
    cum_c = _dot_exact_lhs(same_m, log_a)
    qt = q * jnp.exp(cum)
    kt = k * jnp.exp(-cum)
    kp = k * jnp.exp(cum_c - cum)
    e_c = jnp.exp(cum_c)
    lo_half, hi_half = _half_masks()

    def msk(t, h):
        return jnp.where(lo_half if h == 0 else hi_half, t, 0.0)

    if chained:
        @pl.when(pl.program_id(1) == 0)
        def _():
            s_scr[...] = s0_ref[...]

    o_heads = [None] * nhead
    for p in range(npair):
        qt_p, kt_p, kp_p, ec_p = (_tile(t, p) for t in (qt, kt, kp, e_c))
        qm = [msk(qt_p, 0), msk(qt_p, 1)]
        sc = _dot3(jnp.concatenate(qm, axis=0), kt_p, NT)
        intra = [_dot3(jnp.where(incl, sc[h * TB:(h + 1) * TB], 0.0), _tile(v, 2 * p + h)) for h in range(2)]
        if chained:
            s = s_scr[p]
        rows = [[], []]
        for c in range(nchunk):
            sl = slice(c * chunk, (c + 1) * chunk)
            if not chained:
                s = s0_ref[c, p]
            inter = _dot3(jnp.concatenate([qm[0][sl], qm[1][sl]], axis=0), s, NT)
            rows[0].append(inter[:chunk])
            rows[1].append(inter[chunk:])
            upd = [_dot3(_tile(v, 2 * p + h)[sl], kp_p[sl], TN) for h in range(2)]
            s = s * ec_p[c * chunk:c * chunk + 1, :] + jnp.where(lo_half, upd[0], upd[1])
            if not chained:
                sout_ref[c, p] = s
        if chained:
            s_scr[p] = s
            sout_ref[p] = s
        for h in range(2):
            o_heads[2 * p + h] = intra[h] + jnp.concatenate(rows[h], axis=0)

    og = og_ref[...]
    for h in range(nhead):
        o = o_heads[h]
        o = o * lax.rsqrt(jnp.mean(o * o, axis=-1, keepdims=True) + NORM_EPS) * nw_ref[...]
        z_ref[:, h * LANES:(h + 1) * LANES] = o * jax.nn.silu(_tile(og, h))


def _gla(qkv, lga, og, s0, wg2, bg, nw, seq_len):
    npair = s0.shape[1]
    vw = og.shape[-1]
    chained = seq_len > TB
    row3 = lambda a: pl.BlockSpec((None, TB, a.shape[-1]), (lambda i, j: (i, j, 0)) if chained else (lambda i, j: (0, i, 0)))
    if chained:
        chunk = PROMPT_CHUNK
        b, t, _ = qkv.shape
        grid = (b, t // TB)
        st = pl.BlockSpec((None, npair, LANES, LANES), lambda i, j: (i, 0, 0, 0))
        z_shape = (b, t, vw)
    else:
        chunk = seq_len
        m = qkv.shape[1]
        grid = (m // TB, 1)
        st = pl.BlockSpec((TB // chunk, npair, LANES, LANES), lambda i, j: (i, 0, 0, 0))
        z_shape = (1, m, vw)
    const = lambda a: pl.BlockSpec(a.shape, lambda i, j: (0,) * a.ndim)
    return pl.pallas_call(
        functools.partial(_gla_body, chunk, chained),
        grid=grid,
        in_specs=[row3(qkv), row3(lga), row3(og), st, const(wg2), const(bg), const(nw)],
        out_specs=[pl.BlockSpec((None, TB, vw), (lambda i, j: (i, j, 0)) if chained else (lambda i, j: (0, i, 0))), st],
        out_shape=[jax.ShapeDtypeStruct(z_shape, F32), jax.ShapeDtypeStruct(s0.shape, F32)],
        scratch_shapes=[pltpu.VMEM((npair, LANES, LANES), F32)],
        compiler_params=pltpu.CompilerParams(dimension_semantics=("parallel", "arbitrary"),
                                             vmem_limit_bytes=VMEM_LIMIT),
        name="gla",
    )(qkv, lga, og, s0, wg2, bg, nw)


def _merge_body(x_ref, za_ref, zb_ref, gate_ref, woa_ref, wob_ref, wo_ref, x1_ref):
    d = x_ref.shape[-1]
    ya = _dot(za_ref[...].astype(BF16), woa_ref[...])
    yb = _dot(zb_ref[...].astype(BF16), wob_ref[...])
    gate = gate_ref[...]
    merged = jax.nn.sigmoid(gate[:, :d]) * ya + jax.nn.sigmoid(gate[:, d:]) * yb
    x1_ref[...] = x_ref[...] + _dot(merged.astype(BF16), wo_ref[...])


def _merge(x, za, zb, gate, woa, wob, wo):
    m, d = x.shape
    rows = lambda a: pl.BlockSpec((TM, a.shape[1]), lambda i: (i, 0))
    const = lambda a: pl.BlockSpec(a.shape, lambda i: (0, 0))
    return pl.pallas_call(
        _merge_body,
        grid=(m // TM,),
        in_specs=[rows(x), rows(za), rows(zb), rows(gate), const(woa), const(wob), const(wo)],
        out_specs=rows(x),
        out_shape=jax.ShapeDtypeStruct((m, d), F32),
        compiler_params=pltpu.CompilerParams(dimension_semantics=("parallel",), vmem_limit_bytes=VMEM_LIMIT),
        name="merge_out",
    )(x, za, zb, gate, woa, wob, wo)


def _ffn_tail(u, p1, p2, x1, cw_ref, cb_ref, wd_ref, nf_ref, y_ref):
    hidden = wd_ref.shape[0]
    c = cb_ref[...] + cw_ref[0:1, :] * p2 + cw_ref[1:2, :] * p1 + cw_ref[2:3, :] * u
    act = jax.nn.gelu(c[:, hidden:]) * c[:, :hidden]
    x2 = x1 + _dot(act.astype(BF16), wd_ref[...])
    y_ref[...] = _rmsnorm(x2, nf_ref[...])


def _ffn_chained_body(blocks_per_seq, u_ref, halo_ref, st_ref, x1_ref, cw_ref, cb_ref, wd_ref, nf_ref, y_ref):
    u = u_ref[...]
    row = _iota2((TM, 1), 0)
    first = (pl.program_id(0) % blocks_per_seq) == 0
    h1 = jnp.where(first, st_ref[1:2, :], halo_ref[7:8, :])
    h2 = jnp.where(first, st_ref[0:1, :], halo_ref[6:7, :])
    p1 = jnp.where(row == 0, h1, pltpu.roll(u, 1, axis=0))
    p2 = jnp.where(row == 0, h2, jnp.where(row == 1, h1, pltpu.roll(u, 2, axis=0)))
    _ffn_tail(u, p1, p2, x1_ref[...], cw_ref, cb_ref, wd_ref, nf_ref, y_ref)


def _ffn_short_body(seq_len, u_ref, pa_ref, x1_ref, cw_ref, cb_ref, wd_ref, nf_ref, y_ref):
    u = u_ref[...]
    pa = pa_ref[...]
    t = _iota2((TM, 1), 0) & (seq_len - 1)
    p1 = jnp.where(t == 0, pltpu.roll(pa, TM - 1, axis=0), pltpu.roll(u, 1, axis=0))
    p2 = jnp.where(t < 2, pa, pltpu.roll(u, 2, axis=0))
    _ffn_tail(u, p1, p2, x1_ref[...], cw_ref, cb_ref, wd_ref, nf_ref, y_ref)


def _ffn(u, x1, conv_state, cw, cb, wd, nf, seq_len):
    m, f2 = u.shape
    d = x1.shape[1]
    rows = lambda a: pl.BlockSpec((TM, a.shape[1]), lambda i: (i, 0))
    const = lambda a: pl.BlockSpec(a.shape, lambda i: (0,) * a.ndim)
    common = dict(
        grid=(m // TM,),
        out_specs=rows(x1),
        out_shape=jax.ShapeDtypeStruct((m, d), F32),
        compiler_params=pltpu.CompilerParams(dimension_semantics=("parallel",), vmem_limit_bytes=VMEM_LIMIT),
    )
    if seq_len > TM:
        bps = seq_len // TM
        halo = pl.BlockSpec((8, f2), lambda i: (jnp.maximum(i * (TM // 8) - 1, 0), 0))
        st = pl.BlockSpec((None, 2, f2), lambda i: (i // bps, 0, 0))
        return pl.pallas_call(
            functools.partial(_ffn_chained_body, bps),
            in_specs=[rows(u), halo, st, rows(x1), const(cw), const(cb), const(wd), const(nf)],
            name="ffn_tail", **common)(u, u, conv_state, x1, cw, cb, wd, nf)
    pa = jnp.pad(conv_state, ((0, 0), (0, seq_len - 2), (0, 0))).reshape(m, f2)
    return pl.pallas_call(
        functools.partial(_ffn_short_body, seq_len),
        in_specs=[rows(u), rows(pa), rows(x1), const(cw), const(cb), const(wd), const(nf)],
        name="ffn_tail_short", **common)(u, pa, x1, cw, cb, wd, nf)


def _wkv_to_pairs(s):
    b, h, n, _ = s.shape
    s = s.reshape(b, h // 2, 2, n, n)
    z = jnp.zeros_like(s[:, :, 0])
    top = jnp.concatenate([s[:, :, 0], z], axis=-1)
    bot = jnp.concatenate([z, s[:, :, 1]], axis=-1)
    return jnp.concatenate([top, bot], axis=-2)


def _wkv_from_pairs(s):
    b, p, n2, _ = s.shape
    n = n2 // 2
    return jnp.stack([s[:, :, :n, :n], s[:, :, n:, n:]], axis=2).reshape(b, 2 * p, n, n)


def _gla_to_pairs(s):
    b, h, dk, dv = s.shape
    return s.reshape(b, h // 2, 2, dk, dv).transpose(0, 1, 4, 2, 3).reshape(b, h // 2, dv, 2 * dk)


def _gla_from_pairs(s):
    b, p, dv, dk2 = s.shape
    return s.reshape(b, p, dv, 2, dk2 // 2).transpose(0, 1, 3, 4, 2).reshape(b, 2 * p, dk2 // 2, dv)


def _trunk(x, shift0, wkv0, gla0, conv0, w):
    b, t, d = x.shape
    m = b * t
    xf = x.reshape(m, d)
    p_rw, p_qkv, p_og, p_gate, p_lga = _norm_proj(xf, w["norm_mix"], w["w_in_parts"])
    shift_cols = p_rw.shape[1]
    new_shift = p_rw.reshape(b, t, shift_cols)[:, -1]

    wkv_pairs = _wkv_to_pairs(wkv0)
    gla_pairs = _gla_to_pairs(gla0)
    if t > TB:
        shape3 = lambda a: a.reshape(b, t, a.shape[1])
        prev = shift0[:, None, :]
    else:
        shape3 = lambda a: a[None]
        prev = jnp.pad(shift0[:, None, :], ((0, 0), (0, t - 1), (0, 0))).reshape(m, shift_cols)
    z_a, wkv_new = _rwkv(shape3(p_rw), prev, wkv_pairs, w["mu_shift"], w["rwkv_vec"], w["rwkv_wlo"],
                         w["rwkv_g2"], t)
    z_b, gla_new = _gla(shape3(p_qkv), shape3(p_lga), shape3(p_og), gla_pairs, w["gla_wg2"], w["gla_bg"],
                        w["gla_norm_w"], t)
    z_a = z_a.reshape(m, z_a.shape[-1])
    z_b = z_b.reshape(m, z_b.shape[-1])

    x1 = _merge(xf, z_a, z_b, p_gate, w["w_out_a"], w["w_out_b"], w["w_o"])
    (u,) = _norm_proj(x1, w["norm_ffn"], [w["ffn_w_up"]])
    y = _ffn(u, x1, conv0, w["ffn_conv_w"], w["ffn_conv_b"], w["ffn_w_down"], w["norm_final"], t)
    new_conv = u.reshape(b, t, u.shape[1])[:, -2:]
    return (y.reshape(b, t, d), new_shift[None], _wkv_from_pairs(wkv_new)[None],
            _gla_from_pairs(gla_new)[None], new_conv[None])


def kernel(x_prompt, x_sample, state_rwkv_shift, state_rwkv_wkv, state_gla, state_ffn_conv, norm_mix, w_in, mu_shift, rwkv_w0, rwkv_w2, rwkv_a0, rwkv_a2, rwkv_g2, rwkv_k_k, rwkv_k_a, rwkv_r_k, rwkv_ln_w, rwkv_ln_b, gla_wg2, gla_bg, gla_norm_w, w_out_a, w_out_b, w_o, norm_ffn, ffn_w_up, ffn_conv_w, ffn_conv_b, ffn_w_down, norm_final):
    assert norm_mix.shape[0] == 1, "single-layer step"
    d = x_prompt.shape[-1]
    width = rwkv_w0.shape[-1]
    shift_cols = mu_shift.shape[-1]
    kw = gla_wg2.shape[-1]
    vw = w_out_b.shape[1]
    lora_g = gla_wg2.shape[1]
    lw, la = rwkv_w2.shape[1], rwkv_a2.shape[1]
    assert lw == HEAD and la == HEAD and rwkv_g2.shape[1] == LANES

    win = w_in[0].astype(BF16)
    c0 = shift_cols
    c1 = c0 + 2 * kw + vw
    c2 = c1 + lora_g
    c3 = c2 + vw
    w_lga = jnp.pad(win[:, c1:c2], ((0, 0), (0, LANES - lora_g)))
    w_in_parts = [win[:, :c0], win[:, c0:c1], win[:, c2:c3], win[:, c3:], w_lga]

    zw = jnp.zeros((lw, width), BF16)
    wlo = jnp.concatenate([jnp.concatenate([rwkv_w2[0].astype(BF16), zw], axis=1),
                           jnp.concatenate([zw, rwkv_a2[0].astype(BF16)], axis=1)], axis=0)
    vec = jnp.stack([rwkv_w0[0], rwkv_a0[0], rwkv_k_k[0], rwkv_k_a[0], rwkv_r_k[0].reshape(width),
                     rwkv_ln_w[0], rwkv_ln_b[0], jnp.zeros((width,), F32)])
    w = dict(
        norm_mix=norm_mix, w_in_parts=w_in_parts, mu_shift=mu_shift, rwkv_vec=vec, rwkv_wlo=wlo,
        rwkv_g2=rwkv_g2[0].astype(BF16),
        gla_wg2=jnp.pad(gla_wg2[0].astype(BF16), ((0, LANES - lora_g), (0, 0))), gla_bg=gla_bg,
        gla_norm_w=gla_norm_w,
        w_out_a=w_out_a[0].astype(BF16), w_out_b=w_out_b[0].astype(BF16), w_o=w_o[0].astype(BF16),
        norm_ffn=norm_ffn, ffn_w_up=ffn_w_up[0].astype(BF16), ffn_conv_w=ffn_conv_w[0],
        ffn_conv_b=ffn_conv_b, ffn_w_down=ffn_w_down[0].astype(BF16), norm_final=norm_final[None],
    )

    bp = x_prompt.shape[0]
    dt = x_prompt.dtype
    zeros = lambda s: jnp.zeros((bp,) + s.shape[2:], dt)
    y_p, shift_p, wkv_p, gla_p, conv_p = _trunk(
        x_prompt, zeros(state_rwkv_shift), zeros(state_rwkv_wkv), zeros(state_gla), zeros(state_ffn_conv), w)
    y_s, shift_s, wkv_s, gla_s, conv_s = _trunk(
        x_sample, state_rwkv_shift[0], state_rwkv_wkv[0], state_gla[0], state_ffn_conv[0], w)
    return (y_p, y_s, shift_p, wkv_p, gla_p, conv_p, shift_s, wkv_s, gla_s, conv_s)
```

```python
import functools
import math

import jax
import jax.numpy as jnp
from jax import lax
from jax.experimental import pallas as pl
from jax.experimental.pallas import tpu as pltpu

F32 = jnp.float32
BF16 = jnp.bfloat16
LANES = 128
TB = 128
TM = 256
NCHUNK = 512
VMEM_LIMIT = 56 * 1024 * 1024
NORM_EPS = 1e-6
HEAD = 64
GLA_GATE_TEMP = 16.0
PROMPT_CHUNK = 16
NN = ((1,), (0,))
NT = ((1,), (1,))
TN = ((0,), (0,))


def _dot(a, b, dims=NN):
    return lax.dot_general(a, b, (dims, ((), ())), preferred_element_type=F32)


def _split(x):
    hi = x.astype(BF16)
    lo = (x - hi.astype(F32)).astype(BF16)
    return hi, lo


def _dot3(a, b, dims=NN):
    ah, al = _split(a)
    bh, bl = _split(b)
    return _dot(ah, bh, dims) + (_dot(ah, bl, dims) + _dot(al, bh, dims))


def _split3(x):
    a1 = x.astype(BF16)
    r1 = x - a1.astype(F32)
    a2 = r1.astype(BF16)
    a3 = (r1 - a2.astype(F32)).astype(BF16)
    return a1, a2, a3


def _dot_exact_lhs(e, x):
    x1, x2, x3 = _split3(x)
    return _dot(e, x1) + (_dot(e, x2) + _dot(e, x3))


def _dot_exact_rhs(x, e):
    x1, x2, x3 = _split3(x)
    return _dot(x1, e) + (_dot(x2, e) + _dot(x3, e))


def _iota2(shape, dim):
    return lax.broadcasted_iota(jnp.int32, shape, dim)


def _same_block(n, size):
    sh = int(math.log2(size))
    return (_iota2((n, n), 0) >> sh) == (_iota2((n, n), 1) >> sh)


def _ones_where(mask):
    return jnp.where(mask, 1.0, 0.0).astype(BF16)


def _softplus(y):
    return jnp.maximum(y, 0.0) + jnp.log(1.0 + jnp.exp(-jnp.abs(y)))


def _rmsnorm(x, g):
    return x * lax.rsqrt(jnp.mean(x * x, axis=-1, keepdims=True) + NORM_EPS) * g


def _tile(x, p):
    return x[:, p * LANES:(p + 1) * LANES]


def _chunk_mats(chunk):
    same = _same_block(TB, chunk)
    r = _iota2((TB, TB), 0)
    c = _iota2((TB, TB), 1)
    incl = same & (c <= r)
    strict = same & (c < r)
    return incl, strict, _ones_where(incl), _ones_where(same)


def _half_masks():
    lane = _iota2((1, LANES), 1)
    lo = lane < HEAD
    return lo, jnp.logical_not(lo)


def _norm_proj_body(nw, x_ref, g_ref, *refs):
    w_refs, o_refs = refs[:nw], refs[nw:]
    hb = _rmsnorm(x_ref[...], g_ref[...]).astype(BF16)
    for w_ref, o_ref in zip(w_refs, o_refs):
        n = w_ref.shape[1]
        for n0 in range(0, n, NCHUNK):
            n1 = min(n0 + NCHUNK, n)
            o_ref[:, n0:n1] = _dot(hb, w_ref[:, n0:n1])


def _norm_proj(x, g, weights):
    m, d = x.shape
    nw = len(weights)
    return pl.pallas_call(
        functools.partial(_norm_proj_body, nw),
        grid=(m // TM,),
        in_specs=[pl.BlockSpec((TM, d), lambda i: (i, 0)), pl.BlockSpec((1, d), lambda i: (0, 0))]
        + [pl.BlockSpec(w.shape, lambda i: (0, 0)) for w in weights],
        out_specs=[pl.BlockSpec((TM, w.shape[1]), lambda i: (i, 0)) for w in weights],
        out_shape=[jax.ShapeDtypeStruct((m, w.shape[1]), F32) for w in weights],
        compiler_params=pltpu.CompilerParams(dimension_semantics=("parallel",), vmem_limit_bytes=VMEM_LIMIT),
        name="norm_proj",
    )(x, g, *weights)


def _rwkv_body(chunk, chained, x_ref, prev_ref, s0_ref, mu_ref, vec_ref, wlo_ref, g2_ref,
               z_ref, sout_ref, carry_ref, s_scr):
    nchunk = TB // chunk
    width = z_ref.shape[-1]
    npair = width // LANES
    x = x_ref[...]
    row = _iota2((TB, 1), 0)
    rolled = pltpu.roll(x, 1, axis=0)
    if chained:
        @pl.when(pl.program_id(1) == 0)
        def _():
            carry_ref[...] = prev_ref[...]
            s_scr[...] = s0_ref[...]
        prev = jnp.where(row == 0, carry_ref[...], rolled)
        carry_ref[...] = x[TB - 1:TB, :]
    else:
        prev = jnp.where((row & (chunk - 1)) == 0, prev_ref[...], rolled)
    xs = x + (prev - x) * mu_ref[...]

    r = xs[:, 0:width]
    k = xs[:, width:2 * width]
    v = xs[:, 2 * width:3 * width]
    lora = xs[:, 3 * width:3 * width + LANES]
    lg = xs[:, 3 * width + LANES:3 * width + 2 * LANES]
    lo_half, hi_half = _half_masks()
    lora = jnp.where(lo_half, jnp.tanh(lora), lora)
    wa = _dot(lora.astype(BF16), wlo_ref[...])
    w0, a0, k_k, k_a = vec_ref[0:1, :], vec_ref[1:2, :], vec_ref[2:3, :], vec_ref[3:4, :]
    r_k, ln_w, ln_b = vec_ref[4:5, :], vec_ref[5:6, :], vec_ref[6:7, :]
    wlog = -_softplus(-(w0 + wa[:, :width])) - 0.5
    logw = -jnp.exp(wlog)
    asig = jax.nn.sigmoid(a0 + wa[:, width:])
    g = _dot(jax.nn.sigmoid(lg).astype(BF16), g2_ref[...])

    seg = _ones_where(_same_block(LANES, HEAD))

    def headsum(t):
        return jnp.concatenate([_dot_exact_rhs(_tile(t, p), seg) for p in range(npair)], axis=1)

    kk = k * k_k
    kk = kk / jnp.maximum(jnp.sqrt(headsum(kk * kk)), 1e-12)
    kh = k * (1.0 + (asig - 1.0) * k_a)
    a_vec = -kk
    b_vec = kk * asig

    incl, strict, incl_m, same_m = _chunk_mats(chunk)
    cum = _dot_exact_lhs(incl_m, logw)
    cum_c = _dot_exact_lhs(same_m, logw)
    e_in = jnp.exp(cum)
    e_out = jnp.exp(-cum)
    e_end = jnp.exp(cum_c - cum)
    rt = r * e_in
    at = a_vec * jnp.exp(cum - logw)
    bt = b_vec * e_out
    kt = kh * e_out
    bp = b_vec * e_end
    kp = kh * e_end
    e_c = jnp.exp(cum_c)

    def msk(t, h):
        return jnp.where(lo_half if h == 0 else hi_half, t, 0.0)

    eye = jnp.where(_iota2((TB, TB), 0) == _iota2((TB, TB), 1), 1.0, 0.0)
    zero = jnp.zeros((TB, LANES), F32)
    pairs = range(npair)
    heads = [(p, h) for p in pairs for h in range(2)]
    at_t, rt_t, bt_t, kt_t, v_t, bp_t, kp_t, ec_t = ([_tile(t, p) for p in pairs]
                                                     for t in (at, rt, bt, kt, v, bp, kp, e_c))
    sc = [_dot3(jnp.concatenate([at_t[p], rt_t[p]], axis=0),
                jnp.concatenate([msk(bt_t[p], 0), msk(bt_t[p], 1), msk(kt_t[p], 0), msk(kt_t[p], 1)], axis=0), NT)
          for p in pairs]
    a_ab = [jnp.where(strict, sc[p][:TB, h * TB:(h + 1) * TB], 0.0) for p, h in heads]
    a_ak = [jnp.where(strict, sc[p][:TB, (2 + h) * TB:(3 + h) * TB], 0.0) for p, h in heads]
    p_rb = [jnp.where(incl, sc[p][TB:, h * TB:(h + 1) * TB], 0.0) for p, h in heads]
    p_rk = [jnp.where(incl, sc[p][TB:, (2 + h) * TB:(3 + h) * TB], 0.0) for p, h in heads]
    t_inv = [eye + n for n in a_ab]
    powers = a_ab
    for _ in range(int(math.log2(chunk)) - 1):
        powers = [_dot3(n, n) for n in powers]
        t_inv = [t + _dot3(t, n) for t, n in zip(t_inv, powers)]
    av = [_dot3(jnp.concatenate(a_ak[2 * p:2 * p + 2], axis=1),
                jnp.concatenate([msk(v_t[p], 0), msk(v_t[p], 1)], axis=0)) for p in pairs]
    tay = [_dot3(jnp.concatenate(t_inv[2 * p:2 * p + 2], axis=1),
                 jnp.concatenate([jnp.concatenate([msk(at_t[p], h), msk(av[p], h)], axis=1) for h in range(2)], axis=0))
           for p in pairs]
    ta = [t[:, :LANES] for t in tay]
    yy = [t[:, LANES:] for t in tay]
    qz = [_dot3(jnp.concatenate(p_rb[2 * p:2 * p + 2] + p_rk[2 * p:2 * p + 2], axis=1),
                jnp.concatenate([jnp.concatenate([msk(ta[p], h), msk(yy[p], h)], axis=1) for h in range(2)]
                                + [jnp.concatenate([zero, msk(v_t[p], h)], axis=1) for h in range(2)], axis=0))
          for p in pairs]
    qq = [rt_t[p] + qz[p][:, :LANES] for p in pairs]
    zz = [qz[p][:, LANES:] for p in pairs]

    bd = _same_block(LANES, HEAD)
    diag = _iota2((LANES, LANES), 0) == _iota2((LANES, LANES), 1)
    rows = lambda t, c: t[c * chunk:(c + 1) * chunk]
    m_mat = [[jnp.where(diag, ec_t[p][c * chunk:c * chunk + 1, :], 0.0)
              + jnp.where(bd, _dot3(rows(ta[p], c), rows(bp_t[p], c), TN), 0.0) for p in pairs]
             for c in range(nchunk)]
    n_mat = [[jnp.where(bd, _dot3(jnp.concatenate([rows(yy[p], c), rows(v_t[p], c)], axis=0),
                                  jnp.concatenate([rows(bp_t[p], c), rows(kp_t[p], c)], axis=0), TN), 0.0)
              for p in pairs] for c in range(nchunk)]
    state = [s_scr[p] for p in pairs] if chained else None
    o_rows = [[] for _ in pairs]
    for c in range(nchunk):
        for p in pairs:
            s = state[p] if chained else s0_ref[c, p]
            o_rows[p].append(_dot3(rows(qq[p], c), s, NT) + rows(zz[p], c))
            s = _dot3(s, m_mat[c][p]) + n_mat[c][p]
            if chained:
                state[p] = s
            else:
                sout_ref[c, p] = s
    if chained:
        for p in pairs:
            s_scr[p] = state[p]
            sout_ref[p] = state[p]
    o = jnp.concatenate([jnp.concatenate(o_rows[p], axis=0) for p in pairs], axis=1)

    inv_n = 1.0 / HEAD
    mean = headsum(o) * inv_n
    d = o - mean
    var = headsum(d * d) * inv_n
    o = d * lax.rsqrt(var + 1e-5 * HEAD) * ln_w + ln_b
    o = o + headsum(r * kh * r_k) * v
    z_ref[...] = o * g


def _rwkv(x_rw, prev, s0, mu, vec, wlo, g2, seq_len):
    npair = s0.shape[1]
    width = npair * LANES
    cols = x_rw.shape[-1]
    chained = seq_len > TB
    if chained:
        chunk = PROMPT_CHUNK
        b, t, _ = x_rw.shape
        grid = (b, t // TB)
        in_specs = [pl.BlockSpec((None, TB, cols), lambda i, j: (i, j, 0)),
                    pl.BlockSpec((None, 1, cols), lambda i, j: (i, 0, 0)),
                    pl.BlockSpec((None, npair, LANES, LANES), lambda i, j: (i, 0, 0, 0))]
        out_specs = [pl.BlockSpec((None, TB, width), lambda i, j: (i, j, 0)),
                     pl.BlockSpec((None, npair, LANES, LANES), lambda i, j: (i, 0, 0, 0))]
        out_shape = [jax.ShapeDtypeStruct((b, t, width), F32), jax.ShapeDtypeStruct(s0.shape, F32)]
        sem = ("parallel", "arbitrary")
    else:
        chunk = seq_len
        nseq = TB // chunk
        m = x_rw.shape[1]
        grid = (m // TB, 1)
        in_specs = [pl.BlockSpec((None, TB, cols), lambda i, j: (0, i, 0)),
                    pl.BlockSpec((TB, cols), lambda i, j: (i, 0)),
                    pl.BlockSpec((nseq, npair, LANES, LANES), lambda i, j: (i, 0, 0, 0))]
        out_specs = [pl.BlockSpec((None, TB, width), lambda i, j: (0, i, 0)),
                     pl.BlockSpec((nseq, npair, LANES, LANES), lambda i, j: (i, 0, 0, 0))]
        out_shape = [jax.ShapeDtypeStruct((1, m, width), F32), jax.ShapeDtypeStruct(s0.shape, F32)]
        sem = ("parallel", "arbitrary")
    const = lambda a: pl.BlockSpec(a.shape, lambda i, j: (0,) * a.ndim)
    return pl.pallas_call(
        functools.partial(_rwkv_body, chunk, chained),
        grid=grid,
        in_specs=in_specs + [const(mu), const(vec), const(wlo), const(g2)],
        out_specs=out_specs,
        out_shape=out_shape,
        scratch_shapes=[pltpu.VMEM((1, cols), F32), pltpu.VMEM((npair, LANES, LANES), F32)],
        compiler_params=pltpu.CompilerParams(dimension_semantics=sem, vmem_limit_bytes=VMEM_LIMIT),
        name="rwkv7",
    )(x_rw, prev, s0, mu, vec, wlo, g2)


def _gla_body(chunk, chained, qkv_ref, lga_ref, og_ref, s0_ref, wg2_ref, bg_ref, nw_ref,
              z_ref, sout_ref, s_scr):
    nchunk = TB // chunk
    vw = z_ref.shape[-1]
    nhead = vw // LANES
    npair = nhead // 2
    kw = npair * LANES
    qkv = qkv_ref[...]
    q = qkv[:, 0:kw] * (HEAD ** -0.5)
    k = qkv[:, kw:2 * kw]
    v = qkv[:, 2 * kw:2 * kw + vw]
    gl = _dot(lga_ref[...].astype(BF16), wg2_ref[...]) + bg_ref[...]
    log_a = -_softplus(-gl) * (1.0 / GLA_GATE_TEMP)

    incl, _, incl_m, same_m = _chunk_mats(chunk)
    cum = _dot_exact_lhs(incl_m, log_a)
    cum_c = _dot_exact_lhs(same_m, log_a)
    qt = q * jnp.exp(cum)
    kt = k * jnp.exp(-cum)
    kp = k * jnp.exp(cum_c - cum)
    e_c = jnp.exp(cum_c)
    lo_half, hi_half = _half_masks()

    def msk(t, h):
        return jnp.where(lo_half if h == 0 else hi_half, t, 0.0)

    if chained:
        @pl.when(pl.program_id(1) == 0)
        def _():
            s_scr[...] = s0_ref[...]

    pairs = range(npair)
    heads = [(p, h) for p in pairs for h in range(2)]
    rows = lambda t, c: t[c * chunk:(c + 1) * chunk]
    kt_t, kp_t, ec_t = ([_tile(t, p) for p in pairs] for t in (kt, kp, e_c))
    v_t = [_tile(v, h) for h in range(nhead)]
    qm = [msk(_tile(qt, p), h) for p, h in heads]
    sc = [_dot3(jnp.concatenate(qm[2 * p:2 * p + 2], axis=0), kt_t[p], NT) for p in pairs]
    intra = [_dot3(jnp.where(incl, sc[p][h * TB:(h + 1) * TB], 0.0), v_t[2 * p + h]) for p, h in heads]
    upd = [[jnp.where(lo_half, _dot3(rows(v_t[2 * p], c), rows(kp_t[p], c), TN),
                      _dot3(rows(v_t[2 * p + 1], c), rows(kp_t[p], c), TN)) for p in pairs]
           for c in range(nchunk)]
    state = [s_scr[p] for p in pairs] if chained else None
    inter = [[] for _ in range(nhead)]
    for c in range(nchunk):
        for p in pairs:
            s = state[p] if chained else s0_ref[c, p]
            io = _dot3(jnp.concatenate([rows(qm[2 * p], c), rows(qm[2 * p + 1], c)], axis=0), s, NT)
            inter[2 * p].append(io[:chunk])
            inter[2 * p + 1].append(io[chunk:])
            s = s * ec_t[p][c * chunk:c * chunk + 1, :] + upd[c][p]
            if chained:
                state[p] = s
            else:
                sout_ref[c, p] = s
    if chained:
        for p in pairs:
            s_scr[p] = state[p]
            sout_ref[p] = state[p]
    o_heads = [intra[h] + jnp.concatenate(inter[h], axis=0) for h in range(nhead)]

    og = og_ref[...]
    for h in range(nhead):
        o = o_heads[h]
        o = o * lax.rsqrt(jnp.mean(o * o, axis=-1, keepdims=True) + NORM_EPS) * nw_ref[...]
        z_ref[:, h * LANES:(h + 1) * LANES] = o * jax.nn.silu(_tile(og, h))


def _gla(qkv, lga, og, s0, wg2, bg, nw, seq_len):
    npair = s0.shape[1]
    vw = og.shape[-1]
    chained = seq_len > TB
    row3 = lambda a: pl.BlockSpec((None, TB, a.shape[-1]), (lambda i, j: (i, j, 0)) if chained else (lambda i, j: (0, i, 0)))
    if chained:
        chunk = PROMPT_CHUNK
        b, t, _ = qkv.shape
        grid = (b, t // TB)
        st = pl.BlockSpec((None, npair, LANES, LANES), lambda i, j: (i, 0, 0, 0))
        z_shape = (b, t, vw)
    else:
        chunk = seq_len
        m = qkv.shape[1]
        grid = (m // TB, 1)
        st = pl.BlockSpec((TB // chunk, npair, LANES, LANES), lambda i, j: (i, 0, 0, 0))
        z_shape = (1, m, vw)
    const = lambda a: pl.BlockSpec(a.shape, lambda i, j: (0,) * a.ndim)
    return pl.pallas_call(
        functools.partial(_gla_body, chunk, chained),
        grid=grid,
        in_specs=[row3(qkv), row3(lga), row3(og), st, const(wg2), const(bg), const(nw)],
        out_specs=[pl.BlockSpec((None, TB, vw), (lambda i, j: (i, j, 0)) if chained else (lambda i, j: (0, i, 0))), st],
        out_shape=[jax.ShapeDtypeStruct(z_shape, F32), jax.ShapeDtypeStruct(s0.shape, F32)],
        scratch_shapes=[pltpu.VMEM((npair, LANES, LANES), F32)],
        compiler_params=pltpu.CompilerParams(dimension_semantics=("parallel", "arbitrary"),
                                             vmem_limit_bytes=VMEM_LIMIT),
        name="gla",
    )(qkv, lga, og, s0, wg2, bg, nw)


def _merge_body(x_ref, za_ref, zb_ref, gate_ref, woa_ref, wob_ref, wo_ref, x1_ref):
    d = x_ref.shape[-1]
    ya = _dot(za_ref[...].astype(BF16), woa_ref[...])
    yb = _dot(zb_ref[...].astype(BF16), wob_ref[...])
    gate = gate_ref[...]
    merged = jax.nn.sigmoid(gate[:, :d]) * ya + jax.nn.sigmoid(gate[:, d:]) * yb
    x1_ref[...] = x_ref[...] + _dot(merged.astype(BF16), wo_ref[...])


def _merge(x, za, zb, gate, woa, wob, wo):
    m, d = x.shape
    rows = lambda a: pl.BlockSpec((TM, a.shape[1]), lambda i: (i, 0))
    const = lambda a: pl.BlockSpec(a.shape, lambda i: (0, 0))
    return pl.pallas_call(
        _merge_body,
        grid=(m // TM,),
        in_specs=[rows(x), rows(za), rows(zb), rows(gate), const(woa), const(wob), const(wo)],
        out_specs=rows(x),
        out_shape=jax.ShapeDtypeStruct((m, d), F32),
        compiler_params=pltpu.CompilerParams(dimension_semantics=("parallel",), vmem_limit_bytes=VMEM_LIMIT),
        name="merge_out",
    )(x, za, zb, gate, woa, wob, wo)


def _ffn_tail(u, p1, p2, x1, cw_ref, cb_ref, wd_ref, nf_ref, y_ref):
    hidden = wd_ref.shape[0]
    c = cb_ref[...] + cw_ref[0:1, :] * p2 + cw_ref[1:2, :] * p1 + cw_ref[2:3, :] * u
    act = jax.nn.gelu(c[:, hidden:]) * c[:, :hidden]
    x2 = x1 + _dot(act.astype(BF16), wd_ref[...])
    y_ref[...] = _rmsnorm(x2, nf_ref[...])


def _ffn_chained_body(blocks_per_seq, u_ref, halo_ref, st_ref, x1_ref, cw_ref, cb_ref, wd_ref, nf_ref, y_ref):
    u = u_ref[...]
    row = _iota2((TM, 1), 0)
    first = (pl.program_id(0) % blocks_per_seq) == 0
    h1 = jnp.where(first, st_ref[1:2, :], halo_ref[7:8, :])
    h2 = jnp.where(first, st_ref[0:1, :], halo_ref[6:7, :])
    p1 = jnp.where(row == 0, h1, pltpu.roll(u, 1, axis=0))
    p2 = jnp.where(row == 0, h2, jnp.where(row == 1, h1, pltpu.roll(u, 2, axis=0)))
    _ffn_tail(u, p1, p2, x1_ref[...], cw_ref, cb_ref, wd_ref, nf_ref, y_ref)


def _ffn_short_body(seq_len, u_ref, pa_ref, x1_ref, cw_ref, cb_ref, wd_ref, nf_ref, y_ref):
    u = u_ref[...]
    pa = pa_ref[...]
    t = _iota2((TM, 1), 0) & (seq_len - 1)
    p1 = jnp.where(t == 0, pltpu.roll(pa, TM - 1, axis=0), pltpu.roll(u, 1, axis=0))
    p2 = jnp.where(t < 2, pa, pltpu.roll(u, 2, axis=0))
    _ffn_tail(u, p1, p2, x1_ref[...], cw_ref, cb_ref, wd_ref, nf_ref, y_ref)


def _ffn(u, x1, conv_state, cw, cb, wd, nf, seq_len):
    m, f2 = u.shape
    d = x1.shape[1]
    rows = lambda a: pl.BlockSpec((TM, a.shape[1]), lambda i: (i, 0))
    const = lambda a: pl.BlockSpec(a.shape, lambda i: (0,) * a.ndim)
    common = dict(
        grid=(m // TM,),
        out_specs=rows(x1),
        out_shape=jax.ShapeDtypeStruct((m, d), F32),
        compiler_params=pltpu.CompilerParams(dimension_semantics=("parallel",), vmem_limit_bytes=VMEM_LIMIT),
    )
    if seq_len > TM:
        bps = seq_len // TM
        halo = pl.BlockSpec((8, f2), lambda i: (jnp.maximum(i * (TM // 8) - 1, 0), 0))
        st = pl.BlockSpec((None, 2, f2), lambda i: (i // bps, 0, 0))
        return pl.pallas_call(
            functools.partial(_ffn_chained_body, bps),
            in_specs=[rows(u), halo, st, rows(x1), const(cw), const(cb), const(wd), const(nf)],
            name="ffn_tail", **common)(u, u, conv_state, x1, cw, cb, wd, nf)
    pa = jnp.pad(conv_state, ((0, 0), (0, seq_len - 2), (0, 0))).reshape(m, f2)
    return pl.pallas_call(
        functools.partial(_ffn_short_body, seq_len),
        in_specs=[rows(u), rows(pa), rows(x1), const(cw), const(cb), const(wd), const(nf)],
        name="ffn_tail_short", **common)(u, pa, x1, cw, cb, wd, nf)


def _wkv_to_pairs(s):
    b, h, n, _ = s.shape
    s = s.reshape(b, h // 2, 2, n, n)
    z = jnp.zeros_like(s[:, :, 0])
    top = jnp.concatenate([s[:, :, 0], z], axis=-1)
    bot = jnp.concatenate([z, s[:, :, 1]], axis=-1)
    return jnp.concatenate([top, bot], axis=-2)


def _wkv_from_pairs(s):
    b, p, n2, _ = s.shape
    n = n2 // 2
    return jnp.stack([s[:, :, :n, :n], s[:, :, n:, n:]], axis=2).reshape(b, 2 * p, n, n)


def _gla_to_pairs(s):
    b, h, dk, dv = s.shape
    return s.reshape(b, h // 2, 2, dk, dv).transpose(0, 1, 4, 2, 3).reshape(b, h // 2, dv, 2 * dk)


def _gla_from_pairs(s):
    b, p, dv, dk2 = s.shape
    return s.reshape(b, p, dv, 2, dk2 // 2).transpose(0, 1, 3, 4, 2).reshape(b, 2 * p, dk2 // 2, dv)


def _trunk(x, shift0, wkv0, gla0, conv0, w):
    b, t, d = x.shape
    m = b * t
    xf = x.reshape(m, d)
    p_rw, p_qkv, p_og, p_gate, p_lga = _norm_proj(xf, w["norm_mix"], w["w_in_parts"])
    shift_cols = p_rw.shape[1]
    new_shift = p_rw.reshape(b, t, shift_cols)[:, -1]

    wkv_pairs = _wkv_to_pairs(wkv0)
    gla_pairs = _gla_to_pairs(gla0)
    if t > TB:
        shape3 = lambda a: a.reshape(b, t, a.shape[1])
        prev = shift0[:, None, :]
    else:
        shape3 = lambda a: a[None]
        prev = jnp.pad(shift0[:, None, :], ((0, 0), (0, t - 1), (0, 0))).reshape(m, shift_cols)
    z_a, wkv_new = _rwkv(shape3(p_rw), prev, wkv_pairs, w["mu_shift"], w["rwkv_vec"], w["rwkv_wlo"],
                         w["rwkv_g2"], t)
    z_b, gla_new = _gla(shape3(p_qkv), shape3(p_lga), shape3(p_og), gla_pairs, w["gla_wg2"], w["gla_bg"],
                        w["gla_norm_w"], t)
    z_a = z_a.reshape(m, z_a.shape[-1])
    z_b = z_b.reshape(m, z_b.shape[-1])

    x1 = _merge(xf, z_a, z_b, p_gate, w["w_out_a"], w["w_out_b"], w["w_o"])
    (u,) = _norm_proj(x1, w["norm_ffn"], [w["ffn_w_up"]])
    y = _ffn(u, x1, conv0, w["ffn_conv_w"], w["ffn_conv_b"], w["ffn_w_down"], w["norm_final"], t)
    new_conv = u.reshape(b, t, u.shape[1])[:, -2:]
    return (y.reshape(b, t, d), new_shift[None], _wkv_from_pairs(wkv_new)[None],
            _gla_from_pairs(gla_new)[None], new_conv[None])


def kernel(x_prompt, x_sample, state_rwkv_shift, state_rwkv_wkv, state_gla, state_ffn_conv, norm_mix, w_in, mu_shift, rwkv_w0, rwkv_w2, rwkv_a0, rwkv_a2, rwkv_g2, rwkv_k_k, rwkv_k_a, rwkv_r_k, rwkv_ln_w, rwkv_ln_b, gla_wg2, gla_bg, gla_norm_w, w_out_a, w_out_b, w_o, norm_ffn, ffn_w_up, ffn_conv_w, ffn_conv_b, ffn_w_down, norm_final):
    assert norm_mix.shape[0] == 1, "single-layer step"
    d = x_prompt.shape[-1]
    width = rwkv_w0.shape[-1]
    shift_cols = mu_shift.shape[-1]
    kw = gla_wg2.shape[-1]
    vw = w_out_b.shape[1]
    lora_g = gla_wg2.shape[1]
    lw, la = rwkv_w2.shape[1], rwkv_a2.shape[1]
    assert lw == HEAD and la == HEAD and rwkv_g2.shape[1] == LANES

    win = w_in[0].astype(BF16)
    c0 = shift_cols
    c1 = c0 + 2 * kw + vw
    c2 = c1 + lora_g
    c3 = c2 + vw
    w_lga = jnp.pad(win[:, c1:c2], ((0, 0), (0, LANES - lora_g)))
    w_in_parts = [win[:, :c0], win[:, c0:c1], win[:, c2:c3], win[:, c3:], w_lga]

    zw = jnp.zeros((lw, width), BF16)
    wlo = jnp.concatenate([jnp.concatenate([rwkv_w2[0].astype(BF16), zw], axis=1),
                           jnp.concatenate([zw, rwkv_a2[0].astype(BF16)], axis=1)], axis=0)
    vec = jnp.stack([rwkv_w0[0], rwkv_a0[0], rwkv_k_k[0], rwkv_k_a[0], rwkv_r_k[0].reshape(width),
                     rwkv_ln_w[0], rwkv_ln_b[0], jnp.zeros((width,), F32)])
    w = dict(
        norm_mix=norm_mix, w_in_parts=w_in_parts, mu_shift=mu_shift, rwkv_vec=vec, rwkv_wlo=wlo,
        rwkv_g2=rwkv_g2[0].astype(BF16),
        gla_wg2=jnp.pad(gla_wg2[0].astype(BF16), ((0, LANES - lora_g), (0, 0))), gla_bg=gla_bg,
        gla_norm_w=gla_norm_w,
        w_out_a=w_out_a[0].astype(BF16), w_out_b=w_out_b[0].astype(BF16), w_o=w_o[0].astype(BF16),
        norm_ffn=norm_ffn, ffn_w_up=ffn_w_up[0].astype(BF16), ffn_conv_w=ffn_conv_w[0],
        ffn_conv_b=ffn_conv_b, ffn_w_down=ffn_w_down[0].astype(BF16), norm_final=norm_final[None],
    )

    bp = x_prompt.shape[0]
    dt = x_prompt.dtype
    zeros = lambda s: jnp.zeros((bp,) + s.shape[2:], dt)
    y_p, shift_p, wkv_p, gla_p, conv_p = _trunk(
        x_prompt, zeros(state_rwkv_shift), zeros(state_rwkv_wkv), zeros(state_gla), zeros(state_ffn_conv), w)
    y_s, shift_s, wkv_s, gla_s, conv_s = _trunk(
        x_sample, state_rwkv_shift[0], state_rwkv_wkv[0], state_gla[0], state_ffn_conv[0], w)
    return (y_p, y_s, shift_p, wkv_p, gla_p, conv_p, shift_s, wkv_s, gla_s, conv_s)
```

```python
import functools
import math

import jax
import jax.numpy as jnp
from jax import lax
from jax.experimental import pallas as pl
from jax.experimental.pallas import tpu as pltpu

F32 = jnp.float32
BF16 = jnp.bfloat16
LANES = 128
BF16_ROWS = 16
TB = 128
TM = 256
NCHUNK = 512
FFN_CHUNK = 256
VMEM_LIMIT = 56 * 1024 * 1024
NORM_EPS = 1e-6
HEAD = 64
GLA_GATE_TEMP = 16.0
PROMPT_CHUNK = 16
NN = ((1,), (0,))
NT = ((1,), (1,))
TN = ((0,), (0,))


def _dot(a, b, dims=NN):
    return lax.dot_general(a, b, (dims, ((), ())), preferred_element_type=F32)


def _split(x):
    hi = x.astype(BF16)
    lo = (x - hi.astype(F32)).astype(BF16)
    return hi, lo


def _dot3(a, b, dims=NN):
    ah, al = a if isinstance(a, tuple) else _split(a)
    bh, bl = b if isinstance(b, tuple) else _split(b)
    ka, kb = dims[0][0], dims[1][0]
    if ah.shape[ka] % BF16_ROWS:
        return _dot(ah, bh, dims) + (_dot(ah, bl, dims) + _dot(al, bh, dims))
    return _dot(jnp.concatenate([ah, al, ah], axis=ka), jnp.concatenate([bh, bh, bl], axis=kb), dims)


def _dot1(a, b, dims=NN):
    ah = a[0] if isinstance(a, tuple) else a.astype(BF16)
    bh = b[0] if isinstance(b, tuple) else b.astype(BF16)
    return _dot(ah, bh, dims)


def _split3(x):
    a1 = x.astype(BF16)
    r1 = x - a1.astype(F32)
    a2 = r1.astype(BF16)
    a3 = (r1 - a2.astype(F32)).astype(BF16)
    return a1, a2, a3


def _dot_exact_lhs(e, x):
    return _dot(jnp.concatenate([e, e, e], axis=1), jnp.concatenate(_split3(x), axis=0))


def _dot_exact_rhs(x, e):
    return _dot(jnp.concatenate(_split(x), axis=1), jnp.concatenate([e, e], axis=0))


def _iota2(shape, dim):
    return lax.broadcasted_iota(jnp.int32, shape, dim)


def _same_block(n, size):
    sh = int(math.log2(size))
    return (_iota2((n, n), 0) >> sh) == (_iota2((n, n), 1) >> sh)


def _ones_where(mask):
    return jnp.where(mask, 1.0, 0.0).astype(BF16)


def _softplus(y):
    return jnp.maximum(y, 0.0) + jnp.log(1.0 + jnp.exp(-jnp.abs(y)))


def _rmsnorm(x, g):
    return x * lax.rsqrt(jnp.mean(x * x, axis=-1, keepdims=True) + NORM_EPS) * g


def _tile(x, p):
    return x[:, p * LANES:(p + 1) * LANES]


def _chunk_mats(chunk):
    same = _same_block(TB, chunk)
    r = _iota2((TB, TB), 0)
    c = _iota2((TB, TB), 1)
    incl = same & (c <= r)
    strict = same & (c < r)
    return incl, strict, _ones_where(incl), _ones_where(same)


def _half_masks():
    lane = _iota2((1, LANES), 1)
    lo = lane < HEAD
    return lo, jnp.logical_not(lo)


def _norm_proj_body(nw, x_ref, g_ref, *refs):
    w_refs, o_refs = refs[:nw], refs[nw:]
    hb = _rmsnorm(x_ref[...], g_ref[...]).astype(BF16)
    for w_ref, o_ref in zip(w_refs, o_refs):
        n = w_ref.shape[1]
        for n0 in range(0, n, NCHUNK):
            n1 = min(n0 + NCHUNK, n)
            o_ref[:, n0:n1] = _dot(hb, w_ref[:, n0:n1])


def _norm_proj(x, g, weights):
    m, d = x.shape
    nw = len(weights)
    return pl.pallas_call(
        functools.partial(_norm_proj_body, nw),
        grid=(m // TM,),
        in_specs=[pl.BlockSpec((TM, d), lambda i: (i, 0)), pl.BlockSpec((1, d), lambda i: (0, 0))]
        + [pl.BlockSpec(w.shape, lambda i: (0, 0)) for w in weights],
        out_specs=[pl.BlockSpec((TM, w.shape[1]), lambda i: (i, 0)) for w in weights],
        out_shape=[jax.ShapeDtypeStruct((m, w.shape[1]), F32) for w in weights],
        compiler_params=pltpu.CompilerParams(dimension_semantics=("parallel",), vmem_limit_bytes=VMEM_LIMIT),
        name="norm_proj",
    )(x, g, *weights)


def _rwkv_body(chunk, chained, x_ref, prev_ref, s0_ref, mu_ref, vec_ref, wlo_ref, g2_ref,
               z_ref, sout_ref, carry_ref, s_scr):
    nchunk = TB // chunk
    width = z_ref.shape[-1]
    npair = width // LANES
    x = x_ref[...]
    row = _iota2((TB, 1), 0)
    rolled = pltpu.roll(x, 1, axis=0)
    if chained:
        @pl.when(pl.program_id(1) == 0)
        def _():
            carry_ref[...] = prev_ref[...]
            s_scr[...] = s0_ref[...]
        prev = jnp.where(row == 0, carry_ref[...], rolled)
        carry_ref[...] = x[TB - 1:TB, :]
    else:
        prev = jnp.where((row & (chunk - 1)) == 0, prev_ref[...], rolled)
    xs = x + (prev - x) * mu_ref[...]

    r = xs[:, 0:width]
    k = xs[:, width:2 * width]
    v = xs[:, 2 * width:3 * width]
    lora = xs[:, 3 * width:3 * width + LANES]
    lg = xs[:, 3 * width + LANES:3 * width + 2 * LANES]
    lo_half, hi_half = _half_masks()
    lora = jnp.where(lo_half, jnp.tanh(lora), lora)
    wa = _dot(lora.astype(BF16), wlo_ref[...])
    w0, a0, k_k, k_a = vec_ref[0:1, :], vec_ref[1:2, :], vec_ref[2:3, :], vec_ref[3:4, :]
    r_k, ln_w, ln_b = vec_ref[4:5, :], vec_ref[5:6, :], vec_ref[6:7, :]
    wlog = -_softplus(-(w0 + wa[:, :width])) - 0.5
    logw = -jnp.exp(wlog)
    asig = jax.nn.sigmoid(a0 + wa[:, width:])
    g = _dot(jax.nn.sigmoid(lg).astype(BF16), g2_ref[...])

    seg = _ones_where(_same_block(LANES, HEAD))

    def headsum(t):
        return jnp.concatenate([_dot_exact_rhs(_tile(t, p), seg) for p in range(npair)], axis=1)

    kk = k * k_k
    kk = kk / jnp.maximum(jnp.sqrt(headsum(kk * kk)), 1e-12)
    kh = k * (1.0 + (asig - 1.0) * k_a)
    a_vec = -kk
    b_vec = kk * asig

    incl, strict, incl_m, same_m = _chunk_mats(chunk)
    cums = _dot_exact_lhs(jnp.concatenate([incl_m, same_m], axis=0), logw)
    cum, cum_c = cums[:TB], cums[TB:]
    e_in = jnp.exp(cum)
    e_out = jnp.exp(-cum)
    e_end = jnp.exp(cum_c - cum)
    rt = r * e_in
    at = a_vec * jnp.exp(cum - logw)
    bt = b_vec * e_out
    kt = kh * e_out
    bp = b_vec * e_end
    kp = kh * e_end
    e_c = jnp.exp(cum_c)

    def msk(t, h):
        return jnp.where(lo_half if h == 0 else hi_half, t, 0.0)

    eye = jnp.where(_iota2((TB, TB), 0) == _iota2((TB, TB), 1), 1.0, 0.0)
    pairs = range(npair)
    heads = [(p, h) for p in pairs for h in range(2)]
    at_t, rt_t, bt_t, kt_t, v_t, bp_t, kp_t, ec_t = ([_tile(t, p) for p in pairs]
                                                     for t in (at, rt, bt, kt, v, bp, kp, e_c))
    bk = [_split(jnp.concatenate([msk(bt_t[p], 0), msk(bt_t[p], 1), msk(kt_t[p], 0), msk(kt_t[p], 1)], axis=0))
          for p in pairs]
    sc_a = [_dot3(at_t[p], bk[p], NT) for p in pairs]
    sc_r = [_dot1(rt_t[p], bk[p], NT) for p in pairs]
    a_ab = [jnp.where(strict, sc_a[p][:, h * TB:(h + 1) * TB], 0.0) for p, h in heads]
    a_ak = [jnp.where(strict, sc_a[p][:, (2 + h) * TB:(3 + h) * TB], 0.0) for p, h in heads]
    p_rb = [jnp.where(incl, sc_r[p][:, h * TB:(h + 1) * TB], 0.0) for p, h in heads]
    p_rk = [jnp.where(incl, sc_r[p][:, (2 + h) * TB:(3 + h) * TB], 0.0) for p, h in heads]
    t_inv = [eye + n for n in a_ab]
    powers = a_ab
    for _ in range(int(math.log2(chunk)) - 1):
        powers = [_dot3(n, n) for n in powers]
        t_inv = [t + _dot3(t, n) for t, n in zip(t_inv, powers)]
    av = [_dot3(jnp.concatenate(a_ak[2 * p:2 * p + 2], axis=1),
                jnp.concatenate([msk(v_t[p], 0), msk(v_t[p], 1)], axis=0)) for p in pairs]
    tay = [_dot3(jnp.concatenate(t_inv[2 * p:2 * p + 2], axis=1),
                 jnp.concatenate([jnp.concatenate([msk(at_t[p], h), msk(av[p], h)], axis=1) for h in range(2)], axis=0))
           for p in pairs]
    ta = [t[:, :LANES] for t in tay]
    yy = [t[:, LANES:] for t in tay]
    qz = [_dot1(jnp.concatenate(p_rb[2 * p:2 * p + 2], axis=1),
                jnp.concatenate([jnp.concatenate([msk(ta[p], h), msk(yy[p], h)], axis=1) for h in range(2)], axis=0))
          for p in pairs]
    zv = [_dot1(jnp.concatenate(p_rk[2 * p:2 * p + 2], axis=1),
                jnp.concatenate([msk(v_t[p], 0), msk(v_t[p], 1)], axis=0)) for p in pairs]
    qq = [rt_t[p] + qz[p][:, :LANES] for p in pairs]
    zz = [qz[p][:, LANES:] + zv[p] for p in pairs]

    bd = _same_block(LANES, HEAD)
    diag = _iota2((LANES, LANES), 0) == _iota2((LANES, LANES), 1)
    rows = lambda t, c: t[c * chunk:(c + 1) * chunk]
    m_mat = [[jnp.where(diag, ec_t[p][c * chunk:c * chunk + 1, :], 0.0)
              + jnp.where(bd, _dot3(rows(ta[p], c), rows(bp_t[p], c), TN), 0.0) for p in pairs]
             for c in range(nchunk)]
    n_mat = [[jnp.where(bd, _dot3(jnp.concatenate([rows(yy[p], c), rows(v_t[p], c)], axis=0),
                                  jnp.concatenate([rows(bp_t[p], c), rows(kp_t[p], c)], axis=0), TN), 0.0)
              for p in pairs] for c in range(nchunk)]
    state = [s_scr[p] for p in pairs] if chained else None
    o_rows = [[] for _ in pairs]
    for c in range(nchunk):
        for p in pairs:
            s = _split(state[p] if chained else s0_ref[c, p])
            o_rows[p].append(_dot1(rows(qq[p], c), s, NT) + rows(zz[p], c))
            s = _dot3(s, m_mat[c][p]) + n_mat[c][p]
            if chained:
                state[p] = s
            else:
                sout_ref[c, p] = s
    if chained:
        for p in pairs:
            s_scr[p] = state[p]
            sout_ref[p] = state[p]
    o = jnp.concatenate([jnp.concatenate(o_rows[p], axis=0) for p in pairs], axis=1)

    inv_n = 1.0 / HEAD
    mean = headsum(o) * inv_n
    d = o - mean
    var = headsum(d * d) * inv_n
    o = d * lax.rsqrt(var + 1e-5 * HEAD) * ln_w + ln_b
    o = o + headsum(r * kh * r_k) * v
    z_ref[...] = o * g


def _rwkv(x_rw, prev, s0, mu, vec, wlo, g2, seq_len):
    npair = s0.shape[1]
    width = npair * LANES
    cols = x_rw.shape[-1]
    chained = seq_len > TB
    if chained:
        chunk = PROMPT_CHUNK
        b, t, _ = x_rw.shape
        grid = (b, t // TB)
        in_specs = [pl.BlockSpec((None, TB, cols), lambda i, j: (i, j, 0)),
                    pl.BlockSpec((None, 1, cols), lambda i, j: (i, 0, 0)),
                    pl.BlockSpec((None, npair, LANES, LANES), lambda i, j: (i, 0, 0, 0))]
        out_specs = [pl.BlockSpec((None, TB, width), lambda i, j: (i, j, 0)),
                     pl.BlockSpec((None, npair, LANES, LANES), lambda i, j: (i, 0, 0, 0))]
        out_shape = [jax.ShapeDtypeStruct((b, t, width), F32), jax.ShapeDtypeStruct(s0.shape, F32)]
        sem = ("parallel", "arbitrary")
    else:
        chunk = seq_len
        nseq = TB // chunk
        m = x_rw.shape[1]
        grid = (m // TB, 1)
        in_specs = [pl.BlockSpec((None, TB, cols), lambda i, j: (0, i, 0)),
                    pl.BlockSpec((TB, cols), lambda i, j: (i, 0)),
                    pl.BlockSpec((nseq, npair, LANES, LANES), lambda i, j: (i, 0, 0, 0))]
        out_specs = [pl.BlockSpec((None, TB, width), lambda i, j: (0, i, 0)),
                     pl.BlockSpec((nseq, npair, LANES, LANES), lambda i, j: (i, 0, 0, 0))]
        out_shape = [jax.ShapeDtypeStruct((1, m, width), F32), jax.ShapeDtypeStruct(s0.shape, F32)]
        sem = ("parallel", "arbitrary")
    const = lambda a: pl.BlockSpec(a.shape, lambda i, j: (0,) * a.ndim)
    return pl.pallas_call(
        functools.partial(_rwkv_body, chunk, chained),
        grid=grid,
        in_specs=in_specs + [const(mu), const(vec), const(wlo), const(g2)],
        out_specs=out_specs,
        out_shape=out_shape,
        scratch_shapes=[pltpu.VMEM((1, cols), F32), pltpu.VMEM((npair, LANES, LANES), F32)],
        compiler_params=pltpu.CompilerParams(dimension_semantics=sem, vmem_limit_bytes=VMEM_LIMIT),
        name="rwkv7",
    )(x_rw, prev, s0, mu, vec, wlo, g2)


def _gla_body(chunk, chained, qkv_ref, lga_ref, og_ref, s0_ref, wg2_ref, bg_ref, nw_ref,
              z_ref, sout_ref, s_scr):
    nchunk = TB // chunk
    vw = z_ref.shape[-1]
    nhead = vw // LANES
    npair = nhead // 2
    kw = npair * LANES
    qkv = qkv_ref[...]
    q = qkv[:, 0:kw] * (HEAD ** -0.5)
    k = qkv[:, kw:2 * kw]
    v = qkv[:, 2 * kw:2 * kw + vw]
    gl = _dot(lga_ref[...].astype(BF16), wg2_ref[...]) + bg_ref[...]
    log_a = -_softplus(-gl) * (1.0 / GLA_GATE_TEMP)

    incl, _, incl_m, same_m = _chunk_mats(chunk)
    cums = _dot_exact_lhs(jnp.concatenate([incl_m, same_m], axis=0), log_a)
    cum, cum_c = cums[:TB], cums[TB:]
    qt = q * jnp.exp(cum)
    kt = k * jnp.exp(-cum)
    kp = k * jnp.exp(cum_c - cum)
    e_c = jnp.exp(cum_c)
    lo_half, hi_half = _half_masks()

    def msk(t, h):
        return jnp.where(lo_half if h == 0 else hi_half, t, 0.0)

    if chained:
        @pl.when(pl.program_id(1) == 0)
        def _():
            s_scr[...] = s0_ref[...]

    pairs = range(npair)
    heads = [(p, h) for p in pairs for h in range(2)]
    rows = lambda t, c: t[c * chunk:(c + 1) * chunk]
    kt_t, kp_t, ec_t = ([_tile(t, p) for p in pairs] for t in (kt, kp, e_c))
    v_t = [_tile(v, h) for h in range(nhead)]
    qm = [msk(_tile(qt, p), h) for p, h in heads]
    sc = [_dot1(jnp.concatenate(qm[2 * p:2 * p + 2], axis=0), kt_t[p], NT) for p in pairs]
    intra = [_dot1(jnp.where(incl, sc[p][h * TB:(h + 1) * TB], 0.0), v_t[2 * p + h]) for p, h in heads]
    upd = [[jnp.where(lo_half, _dot3(rows(v_t[2 * p], c), rows(kp_t[p], c), TN),
                      _dot3(rows(v_t[2 * p + 1], c), rows(kp_t[p], c), TN)) for p in pairs]
           for c in range(nchunk)]
    state = [s_scr[p] for p in pairs] if chained else None
    inter = [[] for _ in range(nhead)]
    for c in range(nchunk):
        for p in pairs:
            s = state[p] if chained else s0_ref[c, p]
            io = _dot1(jnp.concatenate([rows(qm[2 * p], c), rows(qm[2 * p + 1], c)], axis=0), s, NT)
            inter[2 * p].append(io[:chunk])
            inter[2 * p + 1].append(io[chunk:])
            s = s * ec_t[p][c * chunk:c * chunk + 1, :] + upd[c][p]
            if chained:
                state[p] = s
            else:
                sout_ref[c, p] = s
    if chained:
        for p in pairs:
            s_scr[p] = state[p]
            sout_ref[p] = state[p]
    o_heads = [intra[h] + jnp.concatenate(inter[h], axis=0) for h in range(nhead)]

    og = og_ref[...]
    for h in range(nhead):
        o = o_heads[h]
        o = o * lax.rsqrt(jnp.mean(o * o, axis=-1, keepdims=True) + NORM_EPS) * nw_ref[...]
        z_ref[:, h * LANES:(h + 1) * LANES] = o * jax.nn.silu(_tile(og, h))


def _gla(qkv, lga, og, s0, wg2, bg, nw, seq_len):
    npair = s0.shape[1]
    vw = og.shape[-1]
    chained = seq_len > TB
    row3 = lambda a: pl.BlockSpec((None, TB, a.shape[-1]), (lambda i, j: (i, j, 0)) if chained else (lambda i, j: (0, i, 0)))
    if chained:
        chunk = PROMPT_CHUNK
        b, t, _ = qkv.shape
        grid = (b, t // TB)
        st = pl.BlockSpec((None, npair, LANES, LANES), lambda i, j: (i, 0, 0, 0))
        z_shape = (b, t, vw)
    else:
        chunk = seq_len
        m = qkv.shape[1]
        grid = (m // TB, 1)
        st = pl.BlockSpec((TB // chunk, npair, LANES, LANES), lambda i, j: (i, 0, 0, 0))
        z_shape = (1, m, vw)
    const = lambda a: pl.BlockSpec(a.shape, lambda i, j: (0,) * a.ndim)
    return pl.pallas_call(
        functools.partial(_gla_body, chunk, chained),
        grid=grid,
        in_specs=[row3(qkv), row3(lga), row3(og), st, const(wg2), const(bg), const(nw)],
        out_specs=[pl.BlockSpec((None, TB, vw), (lambda i, j: (i, j, 0)) if chained else (lambda i, j: (0, i, 0))), st],
        out_shape=[jax.ShapeDtypeStruct(z_shape, F32), jax.ShapeDtypeStruct(s0.shape, F32)],
        scratch_shapes=[pltpu.VMEM((npair, LANES, LANES), F32)],
        compiler_params=pltpu.CompilerParams(dimension_semantics=("parallel", "arbitrary"),
                                             vmem_limit_bytes=VMEM_LIMIT),
        name="gla",
    )(qkv, lga, og, s0, wg2, bg, nw)


def _post_body(chained, seq_len, x_ref, za_ref, zb_ref, gate_ref, aux_ref, woa_ref, wob_ref, wo_ref, g_ref,
               wup_ref, cw_ref, cb_ref, wd_ref, nf_ref, y_ref, tail_ref, carry_ref):
    tm, d = x_ref.shape
    hidden = wd_ref.shape[0]
    ya = _dot(za_ref[...].astype(BF16), woa_ref[...])
    yb = _dot(zb_ref[...].astype(BF16), wob_ref[...])
    gate = gate_ref[...]
    merged = jax.nn.sigmoid(gate[:, :d]) * ya + jax.nn.sigmoid(gate[:, d:]) * yb
    x1 = x_ref[...] + _dot(merged.astype(BF16), wo_ref[...])
    hb = _rmsnorm(x1, g_ref[...]).astype(BF16)

    row = _iota2((tm, 1), 0)
    if chained:
        @pl.when(pl.program_id(1) == 0)
        def _():
            carry_ref[0:6, :] = jnp.zeros((6, carry_ref.shape[1]), F32)
            carry_ref[6:8, :] = aux_ref[...]
    else:
        t = row & (seq_len - 1)

    def conv(u, cs):
        if chained:
            h1, h2 = carry_ref[7:8, cs], carry_ref[6:7, cs]
            p1 = jnp.where(row == 0, h1, pltpu.roll(u, 1, axis=0))
            p2 = jnp.where(row == 0, h2, jnp.where(row == 1, h1, pltpu.roll(u, 2, axis=0)))
            carry_ref[:, cs] = u[tm - 8:, :]
            tail_ref[:, cs] = u[tm - 8:, :]
        else:
            pa = aux_ref[:, cs]
            p1 = jnp.where(t == 0, pltpu.roll(pa, tm - 1, axis=0), pltpu.roll(u, 1, axis=0))
            p2 = jnp.where(t < 2, pa, pltpu.roll(u, 2, axis=0))
            tail_ref[:, cs] = u
        return cb_ref[:, cs] + cw_ref[0:1, cs] * p2 + cw_ref[1:2, cs] * p1 + cw_ref[2:3, cs] * u

    def up(j0):
        cols = (slice(j0, j0 + FFN_CHUNK), slice(hidden + j0, hidden + j0 + FFN_CHUNK))
        return [(_dot(hb, wup_ref[:, cs]), cs) for cs in cols]

    starts = list(range(0, hidden, FFN_CHUNK))
    acc = jnp.zeros((tm, d), F32)
    nxt = up(starts[0])
    for i, j0 in enumerate(starts):
        cur = nxt
        if i + 1 < len(starts):
            nxt = up(starts[i + 1])
        val, gat = (conv(u, cs) for u, cs in cur)
        act = (jax.nn.gelu(gat) * val).astype(BF16)
        acc = acc + _dot(act, wd_ref[j0:j0 + FFN_CHUNK, :])
    y_ref[...] = _rmsnorm(x1 + acc, nf_ref[...])


def _post(x, za, zb, gate, conv_state, w, seq_len):
    b, t, d = x.shape
    m = b * t
    f2 = conv_state.shape[-1]
    chained = seq_len >= TM
    assert seq_len % TM == 0 if chained else (TM // 2) % seq_len == 0
    weights = [w["w_out_a"], w["w_out_b"], w["w_o"], w["norm_ffn"], w["ffn_w_up"], w["ffn_conv_w"],
               w["ffn_conv_b"], w["ffn_w_down"], w["norm_final"]]
    const = lambda a: pl.BlockSpec(a.shape, lambda i, j: (0,) * a.ndim, pipeline_mode=pl.Buffered(1))
    if chained:
        tm = TM
        grid = (b, t // tm)
        rows = lambda a: pl.BlockSpec((None, tm, a.shape[-1]), lambda i, j: (i, j, 0))
        acts = [x, za.reshape(b, t, -1), zb.reshape(b, t, -1), gate.reshape(b, t, -1)]
        aux, aux_spec = conv_state, pl.BlockSpec((None, 2, f2), lambda i, j: (i, 0, 0))
        tail_spec = pl.BlockSpec((None, 8, f2), lambda i, j: (i, 0, 0))
        tail_shape = (b, 8, f2)
    else:
        tm = TM // 2
        grid = (m // tm, 1)
        rows = lambda a: pl.BlockSpec((tm, a.shape[-1]), lambda i, j: (i, 0))
        acts = [x.reshape(m, d), za, zb, gate]
        aux = jnp.pad(conv_state, ((0, 0), (0, seq_len - 2), (0, 0))).reshape(m, f2)
        aux_spec = tail_spec = pl.BlockSpec((tm, f2), lambda i, j: (i, 0))
        tail_shape = (m, f2)
    y, tail = pl.pallas_call(
        functools.partial(_post_body, chained, seq_len),
        grid=grid,
        in_specs=[rows(a) for a in acts] + [aux_spec] + [const(a) for a in weights],
        out_specs=[rows(acts[0]), tail_spec],
        out_shape=[jax.ShapeDtypeStruct(acts[0].shape, F32), jax.ShapeDtypeStruct(tail_shape, F32)],
        scratch_shapes=[pltpu.VMEM((8, f2), F32)],
        compiler_params=pltpu.CompilerParams(dimension_semantics=("parallel", "arbitrary"),
                                             vmem_limit_bytes=VMEM_LIMIT),
        name="post_mixer",
    )(*acts, aux, *weights)
    new_conv = tail[:, -2:] if chained else tail.reshape(b, t, f2)[:, -2:]
    return y.reshape(b, t, d), new_conv


def _wkv_to_pairs(s):
    b, h, n, _ = s.shape
    s = s.reshape(b, h // 2, 2, n, n)
    z = jnp.zeros_like(s[:, :, 0])
    top = jnp.concatenate([s[:, :, 0], z], axis=-1)
    bot = jnp.concatenate([z, s[:, :, 1]], axis=-1)
    return jnp.concatenate([top, bot], axis=-2)


def _wkv_from_pairs(s):
    b, p, n2, _ = s.shape
    n = n2 // 2
    return jnp.stack([s[:, :, :n, :n], s[:, :, n:, n:]], axis=2).reshape(b, 2 * p, n, n)


def _gla_to_pairs(s):
    b, h, dk, dv = s.shape
    return s.reshape(b, h // 2, 2, dk, dv).transpose(0, 1, 4, 2, 3).reshape(b, h // 2, dv, 2 * dk)


def _gla_from_pairs(s):
    b, p, dv, dk2 = s.shape
    return s.reshape(b, p, dv, 2, dk2 // 2).transpose(0, 1, 3, 4, 2).reshape(b, 2 * p, dk2 // 2, dv)


def _trunk(x, shift0, wkv0, gla0, conv0, w):
    b, t, d = x.shape
    m = b * t
    xf = x.reshape(m, d)
    p_rw, p_qkv, p_og, p_gate, p_lga = _norm_proj(xf, w["norm_mix"], w["w_in_parts"])
    shift_cols = p_rw.shape[1]
    new_shift = p_rw.reshape(b, t, shift_cols)[:, -1]

    wkv_pairs = _wkv_to_pairs(wkv0)
    gla_pairs = _gla_to_pairs(gla0)
    if t > TB:
        shape3 = lambda a: a.reshape(b, t, a.shape[1])
        prev = shift0[:, None, :]
    else:
        shape3 = lambda a: a[None]
        prev = jnp.pad(shift0[:, None, :], ((0, 0), (0, t - 1), (0, 0))).reshape(m, shift_cols)
    z_a, wkv_new = _rwkv(shape3(p_rw), prev, wkv_pairs, w["mu_shift"], w["rwkv_vec"], w["rwkv_wlo"],
                         w["rwkv_g2"], t)
    z_b, gla_new = _gla(shape3(p_qkv), shape3(p_lga), shape3(p_og), gla_pairs, w["gla_wg2"], w["gla_bg"],
                        w["gla_norm_w"], t)
    z_a = z_a.reshape(m, z_a.shape[-1])
    z_b = z_b.reshape(m, z_b.shape[-1])

    y, new_conv = _post(x, z_a, z_b, p_gate, conv0, w, t)
    return (y, new_shift[None], _wkv_from_pairs(wkv_new)[None], _gla_from_pairs(gla_new)[None], new_conv[None])


def kernel(x_prompt, x_sample, state_rwkv_shift, state_rwkv_wkv, state_gla, state_ffn_conv, norm_mix, w_in, mu_shift, rwkv_w0, rwkv_w2, rwkv_a0, rwkv_a2, rwkv_g2, rwkv_k_k, rwkv_k_a, rwkv_r_k, rwkv_ln_w, rwkv_ln_b, gla_wg2, gla_bg, gla_norm_w, w_out_a, w_out_b, w_o, norm_ffn, ffn_w_up, ffn_conv_w, ffn_conv_b, ffn_w_down, norm_final):
    assert norm_mix.shape[0] == 1, "single-layer step"
    d = x_prompt.shape[-1]
    width = rwkv_w0.shape[-1]
    shift_cols = mu_shift.shape[-1]
    kw = gla_wg2.shape[-1]
    vw = w_out_b.shape[1]
    lora_g = gla_wg2.shape[1]
    lw, la = rwkv_w2.shape[1], rwkv_a2.shape[1]
    assert lw == HEAD and la == HEAD and rwkv_g2.shape[1] == LANES

    win = w_in[0].astype(BF16)
    c0 = shift_cols
    c1 = c0 + 2 * kw + vw
    c2 = c1 + lora_g
    c3 = c2 + vw
    w_lga = jnp.pad(win[:, c1:c2], ((0, 0), (0, LANES - lora_g)))
    w_in_parts = [win[:, :c0], win[:, c0:c1], win[:, c2:c3], win[:, c3:], w_lga]

    zw = jnp.zeros((lw, width), BF16)
    wlo = jnp.concatenate([jnp.concatenate([rwkv_w2[0].astype(BF16), zw], axis=1),
                           jnp.concatenate([zw, rwkv_a2[0].astype(BF16)], axis=1)], axis=0)
    vec = jnp.stack([rwkv_w0[0], rwkv_a0[0], rwkv_k_k[0], rwkv_k_a[0], rwkv_r_k[0].reshape(width),
                     rwkv_ln_w[0], rwkv_ln_b[0], jnp.zeros((width,), F32)])
    w = dict(
        norm_mix=norm_mix, w_in_parts=w_in_parts, mu_shift=mu_shift, rwkv_vec=vec, rwkv_wlo=wlo,
        rwkv_g2=rwkv_g2[0].astype(BF16),
        gla_wg2=jnp.pad(gla_wg2[0].astype(BF16), ((0, LANES - lora_g), (0, 0))), gla_bg=gla_bg,
        gla_norm_w=gla_norm_w,
        w_out_a=w_out_a[0].astype(BF16), w_out_b=w_out_b[0].astype(BF16), w_o=w_o[0].astype(BF16),
        norm_ffn=norm_ffn, ffn_w_up=ffn_w_up[0].astype(BF16), ffn_conv_w=ffn_conv_w[0],
        ffn_conv_b=ffn_conv_b, ffn_w_down=ffn_w_down[0].astype(BF16), norm_final=norm_final[None],
    )

    bp = x_prompt.shape[0]
    dt = x_prompt.dtype
    zeros = lambda s: jnp.zeros((bp,) + s.shape[2:], dt)
    y_p, shift_p, wkv_p, gla_p, conv_p = _trunk(
        x_prompt, zeros(state_rwkv_shift), zeros(state_rwkv_wkv), zeros(state_gla), zeros(state_ffn_conv), w)
    y_s, shift_s, wkv_s, gla_s, conv_s = _trunk(
        x_sample, state_rwkv_shift[0], state_rwkv_wkv[0], state_gla[0], state_ffn_conv[0], w)
    return (y_p, y_s, shift_p, wkv_p, gla_p, conv_p, shift_s, wkv_s, gla_s, conv_s)
```

```python
import functools
import math

import jax
import jax.numpy as jnp
from jax import lax
from jax.experimental import pallas as pl
from jax.experimental.pallas import tpu as pltpu

F32 = jnp.float32
BF16 = jnp.bfloat16
LANES = 128
BF16_ROWS = 16
MXU_DEPTH = 256
TB = 128
TM = 256
NCHUNK = 512
FFN_CHUNK = 256
VMEM_LIMIT = 56 * 1024 * 1024
NORM_EPS = 1e-6
HEAD = 64
GLA_GATE_TEMP = 16.0
PROMPT_CHUNK = 32
INV_BLOCK = 16
NN = ((1,), (0,))
NT = ((1,), (1,))
TN = ((0,), (0,))


def _dot(a, b, dims=NN):
    return lax.dot_general(a, b, (dims, ((), ())), preferred_element_type=F32)


def _split(x):
    hi = x.astype(BF16)
    lo = (x - hi.astype(F32)).astype(BF16)
    return hi, lo


def _dot3(a, b, dims=NN):
    ah, al = a if isinstance(a, tuple) else _split(a)
    bh, bl = b if isinstance(b, tuple) else _split(b)
    ka, kb = dims[0][0], dims[1][0]
    k = ah.shape[ka]
    if k % BF16_ROWS or 2 * k > MXU_DEPTH:
        return _dot(ah, bh, dims) + (_dot(ah, bl, dims) + _dot(al, bh, dims))
    if 3 * k <= MXU_DEPTH:
        return _dot(jnp.concatenate([ah, al, ah], axis=ka), jnp.concatenate([bh, bh, bl], axis=kb), dims)
    return _dot(jnp.concatenate([ah, al], axis=ka), jnp.concatenate([bh, bh], axis=kb), dims) + _dot(ah, bl, dims)


def _dot1(a, b, dims=NN):
    ah = a[0] if isinstance(a, tuple) else a.astype(BF16)
    bh = b[0] if isinstance(b, tuple) else b.astype(BF16)
    return _dot(ah, bh, dims)


def _split3(x):
    a1 = x.astype(BF16)
    r1 = x - a1.astype(F32)
    a2 = r1.astype(BF16)
    a3 = (r1 - a2.astype(F32)).astype(BF16)
    return a1, a2, a3


def _dot_exact_lhs(e, x):
    x1, x2, x3 = _split3(x)
    return _dot(jnp.concatenate([e, e], axis=1), jnp.concatenate([x1, x2], axis=0)) + _dot(e, x3)


def _dot_exact_rhs(x, e):
    return _dot(jnp.concatenate(_split(x), axis=1), jnp.concatenate([e, e], axis=0))


def _iota2(shape, dim):
    return lax.broadcasted_iota(jnp.int32, shape, dim)


def _same_block(n, size):
    sh = int(math.log2(size))
    return (_iota2((n, n), 0) >> sh) == (_iota2((n, n), 1) >> sh)


def _ones_where(mask):
    return jnp.where(mask, 1.0, 0.0).astype(BF16)


def _softplus(y):
    return jnp.maximum(y, 0.0) + jnp.log(1.0 + jnp.exp(-jnp.abs(y)))


def _rmsnorm(x, g):
    return x * lax.rsqrt(jnp.mean(x * x, axis=-1, keepdims=True) + NORM_EPS) * g


def _tile(x, p):
    return x[:, p * LANES:(p + 1) * LANES]


def _chunk_mats(chunk):
    same = _same_block(TB, chunk)
    r = _iota2((TB, TB), 0)
    c = _iota2((TB, TB), 1)
    incl = same & (c <= r)
    strict = same & (c < r)
    return incl, strict, _ones_where(incl), _ones_where(same)


def _half_masks():
    lane = _iota2((1, LANES), 1)
    lo = lane < HEAD
    return lo, jnp.logical_not(lo)


def _norm_proj_body(nw, x_ref, g_ref, *refs):
    w_refs, o_refs = refs[:nw], refs[nw:]
    hb = _rmsnorm(x_ref[...], g_ref[...]).astype(BF16)
    for w_ref, o_ref in zip(w_refs, o_refs):
        n = w_ref.shape[1]
        for n0 in range(0, n, NCHUNK):
            n1 = min(n0 + NCHUNK, n)
            o_ref[:, n0:n1] = _dot(hb, w_ref[:, n0:n1])


def _norm_proj(x, g, weights):
    m, d = x.shape
    nw = len(weights)
    return pl.pallas_call(
        functools.partial(_norm_proj_body, nw),
        grid=(m // TM,),
        in_specs=[pl.BlockSpec((TM, d), lambda i: (i, 0)), pl.BlockSpec((1, d), lambda i: (0, 0))]
        + [pl.BlockSpec(w.shape, lambda i: (0, 0)) for w in weights],
        out_specs=[pl.BlockSpec((TM, w.shape[1]), lambda i: (i, 0)) for w in weights],
        out_shape=[jax.ShapeDtypeStruct((m, w.shape[1]), F32) for w in weights],
        compiler_params=pltpu.CompilerParams(dimension_semantics=("parallel",), vmem_limit_bytes=VMEM_LIMIT),
        name="norm_proj",
    )(x, g, *weights)


def _rwkv_body(chunk, chained, x_ref, prev_ref, s0_ref, mu_ref, vec_ref, wlo_ref, g2_ref,
               z_ref, sout_ref, carry_ref, s_scr):
    nchunk = TB // chunk
    width = z_ref.shape[-1]
    npair = width // LANES
    x = x_ref[...]
    row = _iota2((TB, 1), 0)
    rolled = pltpu.roll(x, 1, axis=0)
    if chained:
        @pl.when(pl.program_id(1) == 0)
        def _():
            carry_ref[...] = prev_ref[...]
            s_scr[...] = s0_ref[...]
        prev = jnp.where(row == 0, carry_ref[...], rolled)
        carry_ref[...] = x[TB - 1:TB, :]
    else:
        prev = jnp.where((row & (chunk - 1)) == 0, prev_ref[...], rolled)
    xs = x + (prev - x) * mu_ref[...]

    r = xs[:, 0:width]
    k = xs[:, width:2 * width]
    v = xs[:, 2 * width:3 * width]
    lora = xs[:, 3 * width:3 * width + LANES]
    lg = xs[:, 3 * width + LANES:3 * width + 2 * LANES]
    lo_half, hi_half = _half_masks()
    lora = jnp.where(lo_half, jnp.tanh(lora), lora)
    wa = _dot(lora.astype(BF16), wlo_ref[...])
    w0, a0, k_k, k_a = vec_ref[0:1, :], vec_ref[1:2, :], vec_ref[2:3, :], vec_ref[3:4, :]
    r_k, ln_w, ln_b = vec_ref[4:5, :], vec_ref[5:6, :], vec_ref[6:7, :]
    wlog = -_softplus(-(w0 + wa[:, :width])) - 0.5
    logw = -jnp.exp(wlog)
    asig = jax.nn.sigmoid(a0 + wa[:, width:])
    g = _dot(jax.nn.sigmoid(lg).astype(BF16), g2_ref[...])

    seg = _ones_where(_same_block(LANES, HEAD))

    def headsum(t):
        return jnp.concatenate([_dot_exact_rhs(_tile(t, p), seg) for p in range(npair)], axis=1)

    kk = k * k_k
    kk = kk / jnp.maximum(jnp.sqrt(headsum(kk * kk)), 1e-12)
    kh = k * (1.0 + (asig - 1.0) * k_a)
    a_vec = -kk
    b_vec = kk * asig

    incl, strict, incl_m, same_m = _chunk_mats(chunk)
    cums = _dot_exact_lhs(jnp.concatenate([incl_m, same_m], axis=0), logw)
    cum, cum_c = cums[:TB], cums[TB:]
    e_in = jnp.exp(cum)
    e_out = jnp.exp(-cum)
    e_end = jnp.exp(cum_c - cum)
    rt = r * e_in
    at = a_vec * jnp.exp(cum - logw)
    bt = b_vec * e_out
    kt = kh * e_out
    bp = b_vec * e_end
    kp = kh * e_end
    e_c = jnp.exp(cum_c)

    def msk(t, h):
        return jnp.where(lo_half if h == 0 else hi_half, t, 0.0)

    eye = jnp.where(_iota2((TB, TB), 0) == _iota2((TB, TB), 1), 1.0, 0.0)
    pairs = range(npair)
    heads = [(p, h) for p in pairs for h in range(2)]
    at_t, rt_t, bt_t, kt_t, v_t, bp_t, kp_t, ec_t = ([_tile(t, p) for p in pairs]
                                                     for t in (at, rt, bt, kt, v, bp, kp, e_c))
    bk = [_split(jnp.concatenate([msk(bt_t[p], 0), msk(bt_t[p], 1), msk(kt_t[p], 0), msk(kt_t[p], 1)], axis=0))
          for p in pairs]
    sc_a = [_dot3(at_t[p], bk[p], NT) for p in pairs]
    sc_r = [_dot1(rt_t[p], bk[p], NT) for p in pairs]
    a_ab = [jnp.where(strict, sc_a[p][:, h * TB:(h + 1) * TB], 0.0) for p, h in heads]
    a_ak = [jnp.where(strict, sc_a[p][:, (2 + h) * TB:(3 + h) * TB], 0.0) for p, h in heads]
    p_rb = [jnp.where(incl, sc_r[p][:, h * TB:(h + 1) * TB], 0.0) for p, h in heads]
    p_rk = [jnp.where(incl, sc_r[p][:, (2 + h) * TB:(3 + h) * TB], 0.0) for p, h in heads]
    base = min(chunk, INV_BLOCK)
    assert chunk in (base, 2 * base)
    in_base = _same_block(TB, base)
    n_in = [jnp.where(in_base, n, 0.0) for n in a_ab]
    t_inv = [eye + n for n in n_in]
    levels = int(math.log2(base))
    powers = [_dot3(n, n) for n in n_in] if levels > 1 else n_in
    for level in range(1, levels):
        if level + 1 < levels:
            both = [_dot3(x, jnp.concatenate([x, t], axis=1)) for x, t in zip(powers, t_inv)]
            powers = [b[:, :TB] for b in both]
            t_inv = [t + b[:, TB:] for t, b in zip(t_inv, both)]
        else:
            t_inv = [t + _dot3(x, t) for x, t in zip(powers, t_inv)]
    if chunk > base:
        n_off = [n - ni for n, ni in zip(a_ab, n_in)]
        t_inv = [t + _dot3(t, _dot3(no, t)) for t, no in zip(t_inv, n_off)]
    av = [_dot3(jnp.concatenate(a_ak[2 * p:2 * p + 2], axis=1),
                jnp.concatenate([msk(v_t[p], 0), msk(v_t[p], 1)], axis=0)) for p in pairs]
    tay = [_dot3(jnp.concatenate(t_inv[2 * p:2 * p + 2], axis=1),
                 jnp.concatenate([jnp.concatenate([msk(at_t[p], h), msk(av[p], h)], axis=1) for h in range(2)], axis=0))
           for p in pairs]
    ta = [t[:, :LANES] for t in tay]
    yy = [t[:, LANES:] for t in tay]
    qz = [_dot1(jnp.concatenate(p_rb[2 * p:2 * p + 2], axis=1),
                jnp.concatenate([jnp.concatenate([msk(ta[p], h), msk(yy[p], h)], axis=1) for h in range(2)], axis=0))
          for p in pairs]
    zv = [_dot1(jnp.concatenate(p_rk[2 * p:2 * p + 2], axis=1),
                jnp.concatenate([msk(v_t[p], 0), msk(v_t[p], 1)], axis=0)) for p in pairs]
    qq = [rt_t[p] + qz[p][:, :LANES] for p in pairs]
    zz = [qz[p][:, LANES:] + zv[p] for p in pairs]

    bd = _same_block(LANES, HEAD)
    diag = _iota2((LANES, LANES), 0) == _iota2((LANES, LANES), 1)
    rows = lambda t, c: t[c * chunk:(c + 1) * chunk]
    m_mat = [[jnp.where(diag, ec_t[p][c * chunk:c * chunk + 1, :], 0.0)
              + jnp.where(bd, _dot3(rows(ta[p], c), rows(bp_t[p], c), TN), 0.0) for p in pairs]
             for c in range(nchunk)]
    n_mat = [[jnp.where(bd, _dot3(jnp.concatenate([rows(yy[p], c), rows(v_t[p], c)], axis=0),
                                  jnp.concatenate([rows(bp_t[p], c), rows(kp_t[p], c)], axis=0), TN), 0.0)
              for p in pairs] for c in range(nchunk)]
    state = [s_scr[p] for p in pairs] if chained else None
    o_rows = [[] for _ in pairs]
    for c in range(nchunk):
        for p in pairs:
            s = _split(state[p] if chained else s0_ref[c, p])
            o_rows[p].append(_dot1(rows(qq[p], c), s, NT) + rows(zz[p], c))
            s = _dot3(s, m_mat[c][p]) + n_mat[c][p]
            if chained:
                state[p] = s
            else:
                sout_ref[c, p] = s
    if chained:
        for p in pairs:
            s_scr[p] = state[p]
            sout_ref[p] = state[p]
    o = jnp.concatenate([jnp.concatenate(o_rows[p], axis=0) for p in pairs], axis=1)

    inv_n = 1.0 / HEAD
    mean = headsum(o) * inv_n
    d = o - mean
    var = headsum(d * d) * inv_n
    o = d * lax.rsqrt(var + 1e-5 * HEAD) * ln_w + ln_b
    o = o + headsum(r * kh * r_k) * v
    z_ref[...] = o * g


def _rwkv(x_rw, prev, s0, mu, vec, wlo, g2, seq_len):
    npair = s0.shape[1]
    width = npair * LANES
    cols = x_rw.shape[-1]
    chained = seq_len > TB
    if chained:
        chunk = PROMPT_CHUNK
        b, t, _ = x_rw.shape
        grid = (b, t // TB)
        in_specs = [pl.BlockSpec((None, TB, cols), lambda i, j: (i, j, 0)),
                    pl.BlockSpec((None, 1, cols), lambda i, j: (i, 0, 0)),
                    pl.BlockSpec((None, npair, LANES, LANES), lambda i, j: (i, 0, 0, 0))]
        out_specs = [pl.BlockSpec((None, TB, width), lambda i, j: (i, j, 0)),
                     pl.BlockSpec((None, npair, LANES, LANES), lambda i, j: (i, 0, 0, 0))]
        out_shape = [jax.ShapeDtypeStruct((b, t, width), F32), jax.ShapeDtypeStruct(s0.shape, F32)]
        sem = ("parallel", "arbitrary")
    else:
        chunk = seq_len
        nseq = TB // chunk
        m = x_rw.shape[1]
        grid = (m // TB, 1)
        in_specs = [pl.BlockSpec((None, TB, cols), lambda i, j: (0, i, 0)),
                    pl.BlockSpec((TB, cols), lambda i, j: (i, 0)),
                    pl.BlockSpec((nseq, npair, LANES, LANES), lambda i, j: (i, 0, 0, 0))]
        out_specs = [pl.BlockSpec((None, TB, width), lambda i, j: (0, i, 0)),
                     pl.BlockSpec((nseq, npair, LANES, LANES), lambda i, j: (i, 0, 0, 0))]
        out_shape = [jax.ShapeDtypeStruct((1, m, width), F32), jax.ShapeDtypeStruct(s0.shape, F32)]
        sem = ("parallel", "arbitrary")
    const = lambda a: pl.BlockSpec(a.shape, lambda i, j: (0,) * a.ndim)
    return pl.pallas_call(
        functools.partial(_rwkv_body, chunk, chained),
        grid=grid,
        in_specs=in_specs + [const(mu), const(vec), const(wlo), const(g2)],
        out_specs=out_specs,
        out_shape=out_shape,
        scratch_shapes=[pltpu.VMEM((1, cols), F32), pltpu.VMEM((npair, LANES, LANES), F32)],
        compiler_params=pltpu.CompilerParams(dimension_semantics=sem, vmem_limit_bytes=VMEM_LIMIT),
        name="rwkv7",
    )(x_rw, prev, s0, mu, vec, wlo, g2)


def _gla_body(chunk, chained, qkv_ref, lga_ref, og_ref, s0_ref, wg2_ref, bg_ref, nw_ref,
              z_ref, sout_ref, s_scr):
    nchunk = TB // chunk
    vw = z_ref.shape[-1]
    nhead = vw // LANES
    npair = nhead // 2
    kw = npair * LANES
    qkv = qkv_ref[...]
    q = qkv[:, 0:kw] * (HEAD ** -0.5)
    k = qkv[:, kw:2 * kw]
    v = qkv[:, 2 * kw:2 * kw + vw]
    gl = _dot(lga_ref[...].astype(BF16), wg2_ref[...]) + bg_ref[...]
    log_a = -_softplus(-gl) * (1.0 / GLA_GATE_TEMP)

    incl, _, incl_m, same_m = _chunk_mats(chunk)
    cums = _dot_exact_lhs(jnp.concatenate([incl_m, same_m], axis=0), log_a)
    cum, cum_c = cums[:TB], cums[TB:]
    qt = q * jnp.exp(cum)
    kt = k * jnp.exp(-cum)
    kp = k * jnp.exp(cum_c - cum)
    e_c = jnp.exp(cum_c)
    lo_half, hi_half = _half_masks()

    def msk(t, h):
        return jnp.where(lo_half if h == 0 else hi_half, t, 0.0)

    if chained:
        @pl.when(pl.program_id(1) == 0)
        def _():
            s_scr[...] = s0_ref[...]

    pairs = range(npair)
    heads = [(p, h) for p in pairs for h in range(2)]
    rows = lambda t, c: t[c * chunk:(c + 1) * chunk]
    kt_t, kp_t, ec_t = ([_tile(t, p) for p in pairs] for t in (kt, kp, e_c))
    v_t = [_tile(v, h) for h in range(nhead)]
    qm = [msk(_tile(qt, p), h) for p, h in heads]
    sc = [_dot1(jnp.concatenate(qm[2 * p:2 * p + 2], axis=0), kt_t[p], NT) for p in pairs]
    intra = [_dot1(jnp.where(incl, sc[p][h * TB:(h + 1) * TB], 0.0), v_t[2 * p + h]) for p, h in heads]
    upd = [[jnp.where(lo_half, _dot3(rows(v_t[2 * p], c), rows(kp_t[p], c), TN),
                      _dot3(rows(v_t[2 * p + 1], c), rows(kp_t[p], c), TN)) for p in pairs]
           for c in range(nchunk)]
    state = [s_scr[p] for p in pairs] if chained else None
    inter = [[] for _ in range(nhead)]
    for c in range(nchunk):
        for p in pairs:
            s = state[p] if chained else s0_ref[c, p]
            io = _dot1(jnp.concatenate([rows(qm[2 * p], c), rows(qm[2 * p + 1], c)], axis=0), s, NT)
            inter[2 * p].append(io[:chunk])
            inter[2 * p + 1].append(io[chunk:])
            s = s * ec_t[p][c * chunk:c * chunk + 1, :] + upd[c][p]
            if chained:
                state[p] = s
            else:
                sout_ref[c, p] = s
    if chained:
        for p in pairs:
            s_scr[p] = state[p]
            sout_ref[p] = state[p]
    o_heads = [intra[h] + jnp.concatenate(inter[h], axis=0) for h in range(nhead)]

    og = og_ref[...]
    for h in range(nhead):
        o = o_heads[h]
        o = o * lax.rsqrt(jnp.mean(o * o, axis=-1, keepdims=True) + NORM_EPS) * nw_ref[...]
        z_ref[:, h * LANES:(h + 1) * LANES] = o * jax.nn.silu(_tile(og, h))


def _gla(qkv, lga, og, s0, wg2, bg, nw, seq_len):
    npair = s0.shape[1]
    vw = og.shape[-1]
    chained = seq_len > TB
    row3 = lambda a: pl.BlockSpec((None, TB, a.shape[-1]), (lambda i, j: (i, j, 0)) if chained else (lambda i, j: (0, i, 0)))
    if chained:
        chunk = PROMPT_CHUNK
        b, t, _ = qkv.shape
        grid = (b, t // TB)
        st = pl.BlockSpec((None, npair, LANES, LANES), lambda i, j: (i, 0, 0, 0))
        z_shape = (b, t, vw)
    else:
        chunk = seq_len
        m = qkv.shape[1]
        grid = (m // TB, 1)
        st = pl.BlockSpec((TB // chunk, npair, LANES, LANES), lambda i, j: (i, 0, 0, 0))
        z_shape = (1, m, vw)
    const = lambda a: pl.BlockSpec(a.shape, lambda i, j: (0,) * a.ndim)
    return pl.pallas_call(
        functools.partial(_gla_body, chunk, chained),
        grid=grid,
        in_specs=[row3(qkv), row3(lga), row3(og), st, const(wg2), const(bg), const(nw)],
        out_specs=[pl.BlockSpec((None, TB, vw), (lambda i, j: (i, j, 0)) if chained else (lambda i, j: (0, i, 0))), st],
        out_shape=[jax.ShapeDtypeStruct(z_shape, F32), jax.ShapeDtypeStruct(s0.shape, F32)],
        scratch_shapes=[pltpu.VMEM((npair, LANES, LANES), F32)],
        compiler_params=pltpu.CompilerParams(dimension_semantics=("parallel", "arbitrary"),
                                             vmem_limit_bytes=VMEM_LIMIT),
        name="gla",
    )(qkv, lga, og, s0, wg2, bg, nw)


def _post_body(chained, seq_len, x_ref, za_ref, zb_ref, gate_ref, aux_ref, woa_ref, wob_ref, wo_ref, g_ref,
               wup_ref, cw_ref, cb_ref, wd_ref, nf_ref, y_ref, tail_ref, carry_ref):
    tm, d = x_ref.shape
    hidden = wd_ref.shape[0]
    ya = _dot(za_ref[...].astype(BF16), woa_ref[...])
    yb = _dot(zb_ref[...].astype(BF16), wob_ref[...])
    gate = gate_ref[...]
    merged = jax.nn.sigmoid(gate[:, :d]) * ya + jax.nn.sigmoid(gate[:, d:]) * yb
    x1 = x_ref[...] + _dot(merged.astype(BF16), wo_ref[...])
    hb = _rmsnorm(x1, g_ref[...]).astype(BF16)

    if chained:
        @pl.when(pl.program_id(1) == 0)
        def _():
            carry_ref[0:6, :] = jnp.zeros((6, carry_ref.shape[1]), F32)
            carry_ref[6:8, :] = aux_ref[...]
    else:
        t = _iota2((tm, 1), 0) & (seq_len - 1)

    def conv(u, cs):
        if chained:
            ext = jnp.concatenate([carry_ref[:, cs], u], axis=0)
            carry_ref[:, cs] = u[tm - 8:, :]
            tail_ref[:, cs] = u[tm - 8:, :]
            inner = cw_ref[1:2, cs] * ext + pltpu.roll(cw_ref[0:1, cs] * ext, 1, axis=0)
            return (cb_ref[:, cs] + cw_ref[2:3, cs] * ext + pltpu.roll(inner, 1, axis=0))[8:]
        else:
            pa = aux_ref[:, cs]
            p1 = jnp.where(t == 0, pltpu.roll(pa, tm - 1, axis=0), pltpu.roll(u, 1, axis=0))
            p2 = jnp.where(t < 2, pa, pltpu.roll(u, 2, axis=0))
            tail_ref[:, cs] = u
        return cb_ref[:, cs] + cw_ref[0:1, cs] * p2 + cw_ref[1:2, cs] * p1 + cw_ref[2:3, cs] * u

    def up(j0):
        cols = (slice(j0, j0 + FFN_CHUNK), slice(hidden + j0, hidden + j0 + FFN_CHUNK))
        return [(_dot(hb, wup_ref[:, cs]), cs) for cs in cols]

    starts = list(range(0, hidden, FFN_CHUNK))
    acc = jnp.zeros((tm, d), F32)
    nxt = up(starts[0])
    for i, j0 in enumerate(starts):
        cur = nxt
        if i + 1 < len(starts):
            nxt = up(starts[i + 1])
        val, gat = (conv(u, cs) for u, cs in cur)
        act = (jax.nn.gelu(gat) * val).astype(BF16)
        acc = acc + _dot(act, wd_ref[j0:j0 + FFN_CHUNK, :])
    y_ref[...] = _rmsnorm(x1 + acc, nf_ref[...])


def _post(x, za, zb, gate, conv_state, w, seq_len):
    b, t, d = x.shape
    m = b * t
    f2 = conv_state.shape[-1]
    chained = seq_len >= TM
    assert seq_len % TM == 0 if chained else (TM // 2) % seq_len == 0
    weights = [w["w_out_a"], w["w_out_b"], w["w_o"], w["norm_ffn"], w["ffn_w_up"], w["ffn_conv_w"],
               w["ffn_conv_b"], w["ffn_w_down"], w["norm_final"]]
    const = lambda a: pl.BlockSpec(a.shape, lambda i, j: (0,) * a.ndim, pipeline_mode=pl.Buffered(1))
    if chained:
        tm = TM
        grid = (b, t // tm)
        rows = lambda a: pl.BlockSpec((None, tm, a.shape[-1]), lambda i, j: (i, j, 0))
        acts = [x, za.reshape(b, t, -1), zb.reshape(b, t, -1), gate.reshape(b, t, -1)]
        aux, aux_spec = conv_state, pl.BlockSpec((None, 2, f2), lambda i, j: (i, 0, 0))
        tail_spec = pl.BlockSpec((None, 8, f2), lambda i, j: (i, 0, 0))
        tail_shape = (b, 8, f2)
    else:
        tm = TM // 2
        grid = (m // tm, 1)
        rows = lambda a: pl.BlockSpec((tm, a.shape[-1]), lambda i, j: (i, 0))
        acts = [x.reshape(m, d), za, zb, gate]
        aux = jnp.pad(conv_state, ((0, 0), (0, seq_len - 2), (0, 0))).reshape(m, f2)
        aux_spec = tail_spec = pl.BlockSpec((tm, f2), lambda i, j: (i, 0))
        tail_shape = (m, f2)
    y, tail = pl.pallas_call(
        functools.partial(_post_body, chained, seq_len),
        grid=grid,
        in_specs=[rows(a) for a in acts] + [aux_spec] + [const(a) for a in weights],
        out_specs=[rows(acts[0]), tail_spec],
        out_shape=[jax.ShapeDtypeStruct(acts[0].shape, F32), jax.ShapeDtypeStruct(tail_shape, F32)],
        scratch_shapes=[pltpu.VMEM((8, f2), F32)],
        compiler_params=pltpu.CompilerParams(dimension_semantics=("parallel", "arbitrary"),
                                             vmem_limit_bytes=VMEM_LIMIT),
        name="post_mixer",
    )(*acts, aux, *weights)
    new_conv = tail[:, -2:] if chained else tail.reshape(b, t, f2)[:, -2:]
    return y.reshape(b, t, d), new_conv


def _wkv_to_pairs(s):
    b, h, n, _ = s.shape
    s = s.reshape(b, h // 2, 2, n, n)
    z = jnp.zeros_like(s[:, :, 0])
    top = jnp.concatenate([s[:, :, 0], z], axis=-1)
    bot = jnp.concatenate([z, s[:, :, 1]], axis=-1)
    return jnp.concatenate([top, bot], axis=-2)


def _wkv_from_pairs(s):
    b, p, n2, _ = s.shape
    n = n2 // 2
    return jnp.stack([s[:, :, :n, :n], s[:, :, n:, n:]], axis=2).reshape(b, 2 * p, n, n)


def _gla_to_pairs(s):
    b, h, dk, dv = s.shape
    return s.reshape(b, h // 2, 2, dk, dv).transpose(0, 1, 4, 2, 3).reshape(b, h // 2, dv, 2 * dk)


def _gla_from_pairs(s):
    b, p, dv, dk2 = s.shape
    return s.reshape(b, p, dv, 2, dk2 // 2).transpose(0, 1, 3, 4, 2).reshape(b, 2 * p, dk2 // 2, dv)


def _trunk(x, shift0, wkv0, gla0, conv0, w):
    b, t, d = x.shape
    m = b * t
    xf = x.reshape(m, d)
    p_rw, p_qkv, p_og, p_gate, p_lga = _norm_proj(xf, w["norm_mix"], w["w_in_parts"])
    shift_cols = p_rw.shape[1]
    new_shift = p_rw.reshape(b, t, shift_cols)[:, -1]

    wkv_pairs = _wkv_to_pairs(wkv0)
    gla_pairs = _gla_to_pairs(gla0)
    if t > TB:
        shape3 = lambda a: a.reshape(b, t, a.shape[1])
        prev = shift0[:, None, :]
    else:
        shape3 = lambda a: a[None]
        prev = jnp.pad(shift0[:, None, :], ((0, 0), (0, t - 1), (0, 0))).reshape(m, shift_cols)
    z_a, wkv_new = _rwkv(shape3(p_rw), prev, wkv_pairs, w["mu_shift"], w["rwkv_vec"], w["rwkv_wlo"],
                         w["rwkv_g2"], t)
    z_b, gla_new = _gla(shape3(p_qkv), shape3(p_lga), shape3(p_og), gla_pairs, w["gla_wg2"], w["gla_bg"],
                        w["gla_norm_w"], t)
    z_a = z_a.reshape(m, z_a.shape[-1])
    z_b = z_b.reshape(m, z_b.shape[-1])

    y, new_conv = _post(x, z_a, z_b, p_gate, conv0, w, t)
    return (y, new_shift[None], _wkv_from_pairs(wkv_new)[None], _gla_from_pairs(gla_new)[None], new_conv[None])


def kernel(x_prompt, x_sample, state_rwkv_shift, state_rwkv_wkv, state_gla, state_ffn_conv, norm_mix, w_in, mu_shift, rwkv_w0, rwkv_w2, rwkv_a0, rwkv_a2, rwkv_g2, rwkv_k_k, rwkv_k_a, rwkv_r_k, rwkv_ln_w, rwkv_ln_b, gla_wg2, gla_bg, gla_norm_w, w_out_a, w_out_b, w_o, norm_ffn, ffn_w_up, ffn_conv_w, ffn_conv_b, ffn_w_down, norm_final):
    assert norm_mix.shape[0] == 1, "single-layer step"
    d = x_prompt.shape[-1]
    width = rwkv_w0.shape[-1]
    shift_cols = mu_shift.shape[-1]
    kw = gla_wg2.shape[-1]
    vw = w_out_b.shape[1]
    lora_g = gla_wg2.shape[1]
    lw, la = rwkv_w2.shape[1], rwkv_a2.shape[1]
    assert lw == HEAD and la == HEAD and rwkv_g2.shape[1] == LANES

    win = w_in[0].astype(BF16)
    c0 = shift_cols
    c1 = c0 + 2 * kw + vw
    c2 = c1 + lora_g
    c3 = c2 + vw
    w_lga = jnp.pad(win[:, c1:c2], ((0, 0), (0, LANES - lora_g)))
    w_in_parts = [win[:, :c0], win[:, c0:c1], win[:, c2:c3], win[:, c3:], w_lga]

    zw = jnp.zeros((lw, width), BF16)
    wlo = jnp.concatenate([jnp.concatenate([rwkv_w2[0].astype(BF16), zw], axis=1),
                           jnp.concatenate([zw, rwkv_a2[0].astype(BF16)], axis=1)], axis=0)
    vec = jnp.stack([rwkv_w0[0], rwkv_a0[0], rwkv_k_k[0], rwkv_k_a[0], rwkv_r_k[0].reshape(width),
                     rwkv_ln_w[0], rwkv_ln_b[0], jnp.zeros((width,), F32)])
    w = dict(
        norm_mix=norm_mix, w_in_parts=w_in_parts, mu_shift=mu_shift, rwkv_vec=vec, rwkv_wlo=wlo,
        rwkv_g2=rwkv_g2[0].astype(BF16),
        gla_wg2=jnp.pad(gla_wg2[0].astype(BF16), ((0, LANES - lora_g), (0, 0))), gla_bg=gla_bg,
        gla_norm_w=gla_norm_w,
        w_out_a=w_out_a[0].astype(BF16), w_out_b=w_out_b[0].astype(BF16), w_o=w_o[0].astype(BF16),
        norm_ffn=norm_ffn, ffn_w_up=ffn_w_up[0].astype(BF16), ffn_conv_w=ffn_conv_w[0],
        ffn_conv_b=ffn_conv_b, ffn_w_down=ffn_w_down[0].astype(BF16), norm_final=norm_final[None],
    )

    bp = x_prompt.shape[0]
    dt = x_prompt.dtype
    zeros = lambda s: jnp.zeros((bp,) + s.shape[2:], dt)
    y_p, shift_p, wkv_p, gla_p, conv_p = _trunk(
        x_prompt, zeros(state_rwkv_shift), zeros(state_rwkv_wkv), zeros(state_gla), zeros(state_ffn_conv), w)
    y_s, shift_s, wkv_s, gla_s, conv_s = _trunk(
        x_sample, state_rwkv_shift[0], state_rwkv_wkv[0], state_gla[0], state_ffn_conv[0], w)
    return (y_p, y_s, shift_p, wkv_p, gla_p, conv_p, shift_s, wkv_s, gla_s, conv_s)
```

```python
import functools
import itertools
import math

import jax
import jax.numpy as jnp
from jax import lax
from jax.experimental import pallas as pl
from jax.experimental.pallas import tpu as pltpu

F32 = jnp.float32
BF16 = jnp.bfloat16
LANES = 128
BF16_ROWS = 16
MXU_DEPTH = 256
TB = 128
TM = 256
NCHUNK = 512
FFN_CHUNK = 256
VMEM_LIMIT = 56 * 1024 * 1024
NORM_EPS = 1e-6
HEAD = 64
GLA_GATE_TEMP = 16.0
PROMPT_CHUNK = 32
STREAMS = 1
STREAM_LAG = 2
INV_BLOCK = 16
NN = ((1,), (0,))
NT = ((1,), (1,))
TN = ((0,), (0,))


def _dot(a, b, dims=NN):
    return lax.dot_general(a, b, (dims, ((), ())), preferred_element_type=F32)


def _split(x):
    hi = x.astype(BF16)
    lo = (x - hi.astype(F32)).astype(BF16)
    return hi, lo


def _dot3(a, b, dims=NN):
    ah, al = a if isinstance(a, tuple) else _split(a)
    bh, bl = b if isinstance(b, tuple) else _split(b)
    ka, kb = dims[0][0], dims[1][0]
    k = ah.shape[ka]
    if k % BF16_ROWS or 2 * k > MXU_DEPTH:
        return _dot(ah, bh, dims) + (_dot(ah, bl, dims) + _dot(al, bh, dims))
    if 3 * k <= MXU_DEPTH:
        return _dot(jnp.concatenate([ah, al, ah], axis=ka), jnp.concatenate([bh, bh, bl], axis=kb), dims)
    return _dot(jnp.concatenate([ah, al], axis=ka), jnp.concatenate([bh, bh], axis=kb), dims) + _dot(ah, bl, dims)


def _dot1(a, b, dims=NN):
    ah = a[0] if isinstance(a, tuple) else a.astype(BF16)
    bh = b[0] if isinstance(b, tuple) else b.astype(BF16)
    return _dot(ah, bh, dims)


def _split3(x):
    a1 = x.astype(BF16)
    r1 = x - a1.astype(F32)
    a2 = r1.astype(BF16)
    a3 = (r1 - a2.astype(F32)).astype(BF16)
    return a1, a2, a3


def _dot_exact_lhs(e, x):
    x1, x2, x3 = _split3(x)
    return _dot(jnp.concatenate([e, e], axis=1), jnp.concatenate([x1, x2], axis=0)) + _dot(e, x3)


def _dot_exact_rhs(x, e):
    return _dot(jnp.concatenate(_split(x), axis=1), jnp.concatenate([e, e], axis=0))


def _iota2(shape, dim):
    return lax.broadcasted_iota(jnp.int32, shape, dim)


def _same_block(n, size):
    sh = int(math.log2(size))
    return (_iota2((n, n), 0) >> sh) == (_iota2((n, n), 1) >> sh)


def _ones_where(mask):
    return jnp.where(mask, 1.0, 0.0).astype(BF16)


def _softplus(y):
    return jnp.maximum(y, 0.0) + jnp.log(1.0 + jnp.exp(-jnp.abs(y)))


def _rmsnorm(x, g):
    return x * lax.rsqrt(jnp.mean(x * x, axis=-1, keepdims=True) + NORM_EPS) * g


def _tile(x, p):
    return x[:, p * LANES:(p + 1) * LANES]


def _chunk_mats(chunk):
    same = _same_block(TB, chunk)
    r = _iota2((TB, TB), 0)
    c = _iota2((TB, TB), 1)
    incl = same & (c <= r)
    strict = same & (c < r)
    return incl, strict, _ones_where(incl), _ones_where(same)


def _half_masks():
    lane = _iota2((1, LANES), 1)
    lo = lane < HEAD
    return lo, jnp.logical_not(lo)


def _norm_proj_body(nw, x_ref, g_ref, *refs):
    w_refs, o_refs = refs[:nw], refs[nw:]
    hb = _rmsnorm(x_ref[...], g_ref[...]).astype(BF16)
    for w_ref, o_ref in zip(w_refs, o_refs):
        n = w_ref.shape[1]
        for n0 in range(0, n, NCHUNK):
            n1 = min(n0 + NCHUNK, n)
            o_ref[:, n0:n1] = _dot(hb, w_ref[:, n0:n1])


def _norm_proj(x, g, weights):
    m, d = x.shape
    nw = len(weights)
    return pl.pallas_call(
        functools.partial(_norm_proj_body, nw),
        grid=(m // TM,),
        in_specs=[pl.BlockSpec((TM, d), lambda i: (i, 0)), pl.BlockSpec((1, d), lambda i: (0, 0))]
        + [pl.BlockSpec(w.shape, lambda i: (0, 0)) for w in weights],
        out_specs=[pl.BlockSpec((TM, w.shape[1]), lambda i: (i, 0)) for w in weights],
        out_shape=[jax.ShapeDtypeStruct((m, w.shape[1]), F32) for w in weights],
        compiler_params=pltpu.CompilerParams(dimension_semantics=("parallel",), vmem_limit_bytes=VMEM_LIMIT),
        name="norm_proj",
    )(x, g, *weights)


def _wkv_pair_load(ref, idx, p):
    a, b = ref[idx + (2 * p,)], ref[idx + (2 * p + 1,)]
    z = jnp.zeros_like(a)
    return jnp.concatenate([jnp.concatenate([a, z], axis=1), jnp.concatenate([z, b], axis=1)], axis=0)


def _wkv_pair_store(ref, idx, p, s):
    n = s.shape[0] // 2
    ref[idx + (2 * p,)] = s[:n, :n]
    ref[idx + (2 * p + 1,)] = s[n:, n:]


def _rwkv_body(chunk, chained, x_ref, prev_ref, s0_ref, mu_ref, vec_ref, wlo_ref, g2_ref,
               z_ref, sout_ref, carry_ref, s_scr):
    nchunk = TB // chunk
    width = z_ref.shape[-1]
    npair = width // LANES
    x = x_ref[...]
    row = _iota2((TB, 1), 0)
    rolled = pltpu.roll(x, 1, axis=0)
    if chained:
        @pl.when(pl.program_id(1) == 0)
        def _():
            carry_ref[...] = prev_ref[...]
            for p in range(npair):
                s_scr[p] = _wkv_pair_load(s0_ref, (), p)
        prev = jnp.where(row == 0, carry_ref[...], rolled)
        carry_ref[...] = x[TB - 1:TB, :]
    else:
        prev = jnp.where((row & (chunk - 1)) == 0, prev_ref[...], rolled)
    xs = x + (prev - x) * mu_ref[...]

    r = xs[:, 0:width]
    k = xs[:, width:2 * width]
    v = xs[:, 2 * width:3 * width]
    lora = xs[:, 3 * width:3 * width + LANES]
    lg = xs[:, 3 * width + LANES:3 * width + 2 * LANES]
    lo_half, hi_half = _half_masks()
    lora = jnp.where(lo_half, jnp.tanh(lora), lora)
    wa = _dot(lora.astype(BF16), wlo_ref[...])
    w0, a0, k_k, k_a = vec_ref[0:1, :], vec_ref[1:2, :], vec_ref[2:3, :], vec_ref[3:4, :]
    r_k, ln_w, ln_b = vec_ref[4:5, :], vec_ref[5:6, :], vec_ref[6:7, :]
    wlog = -_softplus(-(w0 + wa[:, :width])) - 0.5
    logw = -jnp.exp(wlog)
    asig = jax.nn.sigmoid(a0 + wa[:, width:])
    g = _dot(jax.nn.sigmoid(lg).astype(BF16), g2_ref[...])

    seg = _ones_where(_same_block(LANES, HEAD))

    def headsum(t):
        return jnp.concatenate([_dot_exact_rhs(_tile(t, p), seg) for p in range(npair)], axis=1)

    kk = k * k_k
    kk = kk / jnp.maximum(jnp.sqrt(headsum(kk * kk)), 1e-12)
    kh = k * (1.0 + (asig - 1.0) * k_a)
    a_vec = -kk
    b_vec = kk * asig

    incl, strict, incl_m, same_m = _chunk_mats(chunk)
    cums = _dot_exact_lhs(jnp.concatenate([incl_m, same_m], axis=0), logw)
    cum, cum_c = cums[:TB], cums[TB:]
    e_in = jnp.exp(cum)
    e_out = jnp.exp(-cum)
    e_end = jnp.exp(cum_c - cum)
    rt = r * e_in
    at = a_vec * jnp.exp(cum - logw)
    bt = b_vec * e_out
    kt = kh * e_out
    bp = b_vec * e_end
    kp = kh * e_end
    e_c = jnp.exp(cum_c)
    yield

    def msk(t, h):
        return jnp.where(lo_half if h == 0 else hi_half, t, 0.0)

    eye = jnp.where(_iota2((TB, TB), 0) == _iota2((TB, TB), 1), 1.0, 0.0)
    pairs = range(npair)
    heads = [(p, h) for p in pairs for h in range(2)]
    at_t, rt_t, bt_t, kt_t, v_t, bp_t, kp_t, ec_t = ([_tile(t, p) for p in pairs]
                                                     for t in (at, rt, bt, kt, v, bp, kp, e_c))
    bk = [_split(jnp.concatenate([msk(bt_t[p], 0), msk(bt_t[p], 1), msk(kt_t[p], 0), msk(kt_t[p], 1)], axis=0))
          for p in pairs]
    sc_a = [_dot3(at_t[p], bk[p], NT) for p in pairs]
    sc_r = [_dot1(rt_t[p], bk[p], NT) for p in pairs]
    a_ab = [jnp.where(strict, sc_a[p][:, h * TB:(h + 1) * TB], 0.0) for p, h in heads]
    a_ak = [jnp.where(strict, sc_a[p][:, (2 + h) * TB:(3 + h) * TB], 0.0) for p, h in heads]
    p_rb = [jnp.where(incl, sc_r[p][:, h * TB:(h + 1) * TB], 0.0) for p, h in heads]
    p_rk = [jnp.where(incl, sc_r[p][:, (2 + h) * TB:(3 + h) * TB], 0.0) for p, h in heads]
    yield
    base = min(chunk, INV_BLOCK)
    assert chunk in (base, 2 * base)
    in_base = _same_block(TB, base)
    n_in = [jnp.where(in_base, n, 0.0) for n in a_ab]
    t_inv = [eye + n for n in n_in]
    levels = int(math.log2(base))
    powers = [_dot3(n, n) for n in n_in] if levels > 1 else n_in
    for level in range(1, levels):
        if level + 1 < levels:
            both = [_dot3(x, jnp.concatenate([x, t], axis=1)) for x, t in zip(powers, t_inv)]
            powers = [b[:, :TB] for b in both]
            t_inv = [t + b[:, TB:] for t, b in zip(t_inv, both)]
        else:
            t_inv = [t + _dot3(x, t) for x, t in zip(powers, t_inv)]
    if chunk > base:
        n_off = [n - ni for n, ni in zip(a_ab, n_in)]
        t_inv = [t + _dot3(t, _dot3(no, t)) for t, no in zip(t_inv, n_off)]
    yield
    av =[_dot3(jnp.concatenate(a_ak[2 * p:2 * p + 2], axis=1),
                jnp.concatenate([msk(v_t[p], 0), msk(v_t[p], 1)], axis=0)) for p in pairs]
    tay = [_dot3(jnp.concatenate(t_inv[2 * p:2 * p + 2], axis=1),
                 jnp.concatenate([jnp.concatenate([msk(at_t[p], h), msk(av[p], h)], axis=1) for h in range(2)], axis=0))
           for p in pairs]
    ta = [t[:, :LANES] for t in tay]
    yy = [t[:, LANES:] for t in tay]
    qz = [_dot1(jnp.concatenate(p_rb[2 * p:2 * p + 2], axis=1),
                jnp.concatenate([jnp.concatenate([msk(ta[p], h), msk(yy[p], h)], axis=1) for h in range(2)], axis=0))
          for p in pairs]
    zv = [_dot1(jnp.concatenate(p_rk[2 * p:2 * p + 2], axis=1),
                jnp.concatenate([msk(v_t[p], 0), msk(v_t[p], 1)], axis=0)) for p in pairs]
    qq = [rt_t[p] + qz[p][:, :LANES] for p in pairs]
    zz = [qz[p][:, LANES:] + zv[p] for p in pairs]

    bd = _same_block(LANES, HEAD)
    diag = _iota2((LANES, LANES), 0) == _iota2((LANES, LANES), 1)
    rows = lambda t, c: t[c * chunk:(c + 1) * chunk]
    m_mat = [[jnp.where(diag, ec_t[p][c * chunk:c * chunk + 1, :], 0.0)
              + jnp.where(bd, _dot3(rows(ta[p], c), rows(bp_t[p], c), TN), 0.0) for p in pairs]
             for c in range(nchunk)]
    n_mat = [[jnp.where(bd, _dot3(jnp.concatenate([rows(yy[p], c), rows(v_t[p], c)], axis=0),
                                  jnp.concatenate([rows(bp_t[p], c), rows(kp_t[p], c)], axis=0), TN), 0.0)
              for p in pairs] for c in range(nchunk)]
    yield
    state = [s_scr[p] for p in pairs] if chained else None
    o_rows = [[] for _ in pairs]
    for c in range(nchunk):
        for p in pairs:
            s = _split(state[p] if chained else _wkv_pair_load(s0_ref, (c,), p))
            o_rows[p].append(_dot1(rows(qq[p], c), s, NT) + rows(zz[p], c))
            s = _dot3(s, m_mat[c][p]) + n_mat[c][p]
            if chained:
                state[p] = s
            else:
                _wkv_pair_store(sout_ref, (c,), p, s)
    if chained:
        for p in pairs:
            s_scr[p] = state[p]

        @pl.when(pl.program_id(1) == pl.num_programs(1) - 1)
        def _():
            for p in pairs:
                _wkv_pair_store(sout_ref, (), p, state[p])
    o = jnp.concatenate([jnp.concatenate(o_rows[p], axis=0) for p in pairs], axis=1)
    yield

    inv_n = 1.0 / HEAD
    mean = headsum(o) * inv_n
    d = o - mean
    var = headsum(d * d) * inv_n
    o = d * lax.rsqrt(var + 1e-5 * HEAD) * ln_w + ln_b
    o = o + headsum(r * kh * r_k) * v
    z_ref[...] = o * g


def _gla_pair_load(ref, idx, p):
    return jnp.concatenate([ref[idx + (2 * p,)], ref[idx + (2 * p + 1,)]], axis=0).T


def _gla_pair_store(ref, idx, p, s):
    st = s.T
    dk = st.shape[0] // 2
    ref[idx + (2 * p,)] = st[:dk]
    ref[idx + (2 * p + 1,)] = st[dk:]


def _gla_body(chunk, chained, qkv_ref, lga_ref, og_ref, s0_ref, wg2_ref, bg_ref, nw_ref,
              z_ref, sout_ref, s_scr):
    nchunk = TB // chunk
    vw = z_ref.shape[-1]
    nhead = vw // LANES
    npair = nhead // 2
    kw = npair * LANES
    qkv = qkv_ref[...]
    q = qkv[:, 0:kw] * (HEAD ** -0.5)
    k = qkv[:, kw:2 * kw]
    v = qkv[:, 2 * kw:2 * kw + vw]
    gl = _dot(lga_ref[...].astype(BF16), wg2_ref[...]) + bg_ref[...]
    log_a = -_softplus(-gl) * (1.0 / GLA_GATE_TEMP)

    incl, _, incl_m, same_m = _chunk_mats(chunk)
    cums = _dot_exact_lhs(jnp.concatenate([incl_m, same_m], axis=0), log_a)
    cum, cum_c = cums[:TB], cums[TB:]
    qt = q * jnp.exp(cum)
    kt = k * jnp.exp(-cum)
    kp = k * jnp.exp(cum_c - cum)
    e_c = jnp.exp(cum_c)
    yield
    lo_half, hi_half = _half_masks()

    def msk(t, h):
        return jnp.where(lo_half if h == 0 else hi_half, t, 0.0)

    if chained:
        @pl.when(pl.program_id(1) == 0)
        def _():
            for p in range(npair):
                s_scr[p] = _gla_pair_load(s0_ref, (), p)

    pairs = range(npair)
    heads = [(p, h) for p in pairs for h in range(2)]
    rows = lambda t, c: t[c * chunk:(c + 1) * chunk]
    kt_t, kp_t, ec_t = ([_tile(t, p) for p in pairs] for t in (kt, kp, e_c))
    v_t = [_tile(v, h) for h in range(nhead)]
    qm = [msk(_tile(qt, p), h) for p, h in heads]
    sc = [_dot1(jnp.concatenate(qm[2 * p:2 * p + 2], axis=0), kt_t[p], NT) for p in pairs]
    intra = [_dot1(jnp.where(incl, sc[p][h * TB:(h + 1) * TB], 0.0), v_t[2 * p + h]) for p, h in heads]
    upd = [[jnp.where(lo_half, _dot3(rows(v_t[2 * p], c), rows(kp_t[p], c), TN),
                      _dot3(rows(v_t[2 * p + 1], c), rows(kp_t[p], c), TN)) for p in pairs]
           for c in range(nchunk)]
    yield
    state = [s_scr[p] for p in pairs] if chained else None
    inter = [[] for _ in range(nhead)]
    for c in range(nchunk):
        for p in pairs:
            s = state[p] if chained else _gla_pair_load(s0_ref, (c,), p)
            io = _dot1(jnp.concatenate([rows(qm[2 * p], c), rows(qm[2 * p + 1], c)], axis=0), s, NT)
            inter[2 * p].append(io[:chunk])
            inter[2 * p + 1].append(io[chunk:])
            s = s * ec_t[p][c * chunk:c * chunk + 1, :] + upd[c][p]
            if chained:
                state[p] = s
            else:
                _gla_pair_store(sout_ref, (c,), p, s)
    if chained:
        for p in pairs:
            s_scr[p] = state[p]

        @pl.when(pl.program_id(1) == pl.num_programs(1) - 1)
        def _():
            for p in pairs:
                _gla_pair_store(sout_ref, (), p, state[p])
    o_heads = [intra[h] + jnp.concatenate(inter[h], axis=0) for h in range(nhead)]
    yield

    og = og_ref[...]
    for h in range(nhead):
        o = o_heads[h]
        o = o * lax.rsqrt(jnp.mean(o * o, axis=-1, keepdims=True) + NORM_EPS) * nw_ref[...]
        z_ref[:, h * LANES:(h + 1) * LANES] = o * jax.nn.silu(_tile(og, h))


def _mixers_body(chunk, chained, x_ref, prev_ref, wkv0_ref, qkv_ref, lga_ref, og_ref, gla0_ref,
                 mu_ref, vec_ref, wlo_ref, g2_ref, wg2_ref, bg_ref, nw_ref,
                 za_ref, wkv_ref, zb_ref, gla_ref, carry_ref, wkv_scr, gla_scr):
    nseq = TB // chunk

    def view(ref, s, per_seq=False):
        if chained:
            return ref.at[s]
        n = nseq if per_seq else TB
        return ref.at[pl.ds(s * n, n)]

    def stream(s):
        rwkv = _rwkv_body(chunk, chained, view(x_ref, s), view(prev_ref, s), view(wkv0_ref, s, True), mu_ref, vec_ref,
                          wlo_ref, g2_ref, view(za_ref, s), view(wkv_ref, s, True), carry_ref.at[s], wkv_scr.at[s])
        gla = _gla_body(chunk, chained, view(qkv_ref, s), view(lga_ref, s), view(og_ref, s), view(gla0_ref, s, True),
                        wg2_ref, bg_ref, nw_ref, view(zb_ref, s), view(gla_ref, s, True), gla_scr.at[s])
        for _ in itertools.zip_longest(rwkv, gla):
            yield

    nstream = carry_ref.shape[0]
    streams = [stream(s) for s in range(nstream)]
    for s, gen in enumerate(streams):
        for _ in range(STREAM_LAG * (nstream - 1 - s)):
            next(gen, None)
    for _ in itertools.zip_longest(*streams):
        pass


def _mixers(x_rw, prev, wkv0, qkv, lga, og, gla0, w, seq_len):
    width = wkv0.shape[1] * wkv0.shape[2]
    vw = og.shape[-1]
    cols = x_rw.shape[-1]
    chained = seq_len > TB
    if chained:
        chunk = PROMPT_CHUNK
        b, t, _ = x_rw.shape
        nstream = STREAMS
        assert b % nstream == 0
        grid = (b // nstream, t // TB)
        rows = lambda n: pl.BlockSpec((nstream, TB, n), lambda i, j: (i, j, 0))
        prev_spec = pl.BlockSpec((nstream, 1, cols), lambda i, j: (i, 0, 0))
        st = lambda a: pl.BlockSpec((nstream,) + a.shape[1:], lambda i, j: (i, 0, 0, 0))
        lead = (b, t)
    else:
        chunk = seq_len
        m = x_rw.shape[1]
        nstream = 1
        step = nstream * TB
        assert m % step == 0
        grid = (m // step, 1)
        rows = lambda n: pl.BlockSpec((None, step, n), lambda i, j: (0, i, 0))
        prev_spec = pl.BlockSpec((step, cols), lambda i, j: (i, 0))
        st = lambda a: pl.BlockSpec((step // chunk,) + a.shape[1:], lambda i, j: (i, 0, 0, 0))
        lead = (1, m)
    const = lambda a: pl.BlockSpec(a.shape, lambda i, j: (0,) * a.ndim)
    consts = [w["mu_shift"], w["rwkv_vec"], w["rwkv_wlo"], w["rwkv_g2"], w["gla_wg2"], w["gla_bg"], w["gla_norm_w"]]
    return pl.pallas_call(
        functools.partial(_mixers_body, chunk, chained),
        grid=grid,
        in_specs=[rows(cols), prev_spec, st(wkv0), rows(qkv.shape[-1]), rows(lga.shape[-1]), rows(vw), st(gla0)]
        + [const(a) for a in consts],
        out_specs=[rows(width), st(wkv0), rows(vw), st(gla0)],
        out_shape=[jax.ShapeDtypeStruct(lead + (width,), F32), jax.ShapeDtypeStruct(wkv0.shape, F32),
                   jax.ShapeDtypeStruct(lead + (vw,), F32), jax.ShapeDtypeStruct(gla0.shape, F32)],
        scratch_shapes=[pltpu.VMEM((nstream, 1, cols), F32),
                        pltpu.VMEM((nstream, wkv0.shape[1] // 2, LANES, LANES), F32),
                        pltpu.VMEM((nstream, gla0.shape[1] // 2, LANES, LANES), F32)],
        compiler_params=pltpu.CompilerParams(dimension_semantics=("parallel", "arbitrary"),
                                             vmem_limit_bytes=VMEM_LIMIT),
        name="mixers",
    )(x_rw, prev, wkv0, qkv, lga, og, gla0, *consts)


def _post_body(chained, seq_len, x_ref, za_ref, zb_ref, gate_ref, aux_ref, woa_ref, wob_ref, wo_ref, g_ref,
               wup_ref, cw_ref, cb_ref, wd_ref, nf_ref, y_ref, tail_ref, carry_ref):
    tm, d = x_ref.shape
    hidden = wd_ref.shape[0]
    ya = _dot(za_ref[...].astype(BF16), woa_ref[...])
    yb = _dot(zb_ref[...].astype(BF16), wob_ref[...])
    gate = gate_ref[...]
    merged = jax.nn.sigmoid(gate[:, :d]) * ya + jax.nn.sigmoid(gate[:, d:]) * yb
    x1 = x_ref[...] + _dot(merged.astype(BF16), wo_ref[...])
    hb = _rmsnorm(x1, g_ref[...]).astype(BF16)

    if chained:
        @pl.when(pl.program_id(1) == 0)
        def _():
            carry_ref[0:6, :] = jnp.zeros((6, carry_ref.shape[1]), F32)
            carry_ref[6:8, :] = aux_ref[...]
    else:
        t = _iota2((tm, 1), 0) & (seq_len - 1)

    def conv(u, cs):
        if chained:
            ext = jnp.concatenate([carry_ref[:, cs], u], axis=0)
            carry_ref[:, cs] = u[tm - 8:, :]
            tail_ref[:, cs] = u[tm - 8:, :]
            inner = cw_ref[1:2, cs] * ext + pltpu.roll(cw_ref[0:1, cs] * ext, 1, axis=0)
            return (cb_ref[:, cs] + cw_ref[2:3, cs] * ext + pltpu.roll(inner, 1, axis=0))[8:]
        else:
            pa = aux_ref[:, cs]
            p1 = jnp.where(t == 0, pltpu.roll(pa, tm - 1, axis=0), pltpu.roll(u, 1, axis=0))
            p2 = jnp.where(t < 2, pa, pltpu.roll(u, 2, axis=0))
            tail_ref[:, cs] = u
        return cb_ref[:, cs] + cw_ref[0:1, cs] * p2 + cw_ref[1:2, cs] * p1 + cw_ref[2:3, cs] * u

    def up(j0):
        cols = (slice(j0, j0 + FFN_CHUNK), slice(hidden + j0, hidden + j0 + FFN_CHUNK))
        return [(_dot(hb, wup_ref[:, cs]), cs) for cs in cols]

    starts = list(range(0, hidden, FFN_CHUNK))
    acc = jnp.zeros((tm, d), F32)
    nxt = up(starts[0])
    for i, j0 in enumerate(starts):
        cur = nxt
        if i + 1 < len(starts):
            nxt = up(starts[i + 1])
        val, gat = (conv(u, cs) for u, cs in cur)
        act = (jax.nn.gelu(gat) * val).astype(BF16)
        acc = acc + _dot(act, wd_ref[j0:j0 + FFN_CHUNK, :])
    y_ref[...] = _rmsnorm(x1 + acc, nf_ref[...])


def _post(x, za, zb, gate, conv_state, w, seq_len):
    b, t, d = x.shape
    m = b * t
    f2 = conv_state.shape[-1]
    chained = seq_len >= TM
    assert seq_len % TM == 0 if chained else (TM // 2) % seq_len == 0
    weights = [w["w_out_a"], w["w_out_b"], w["w_o"], w["norm_ffn"], w["ffn_w_up"], w["ffn_conv_w"],
               w["ffn_conv_b"], w["ffn_w_down"], w["norm_final"]]
    const = lambda a: pl.BlockSpec(a.shape, lambda i, j: (0,) * a.ndim, pipeline_mode=pl.Buffered(1))
    if chained:
        tm = TM
        grid = (b, t // tm)
        rows = lambda a: pl.BlockSpec((None, tm, a.shape[-1]), lambda i, j: (i, j, 0))
        acts = [x, za.reshape(b, t, -1), zb.reshape(b, t, -1), gate.reshape(b, t, -1)]
        aux, aux_spec = conv_state, pl.BlockSpec((None, 2, f2), lambda i, j: (i, 0, 0))
        tail_spec = pl.BlockSpec((None, 8, f2), lambda i, j: (i, 0, 0))
        tail_shape = (b, 8, f2)
    else:
        tm = TM // 2
        grid = (m // tm, 1)
        rows = lambda a: pl.BlockSpec((tm, a.shape[-1]), lambda i, j: (i, 0))
        acts = [x.reshape(m, d), za, zb, gate]
        aux = jnp.pad(conv_state, ((0, 0), (0, seq_len - 2), (0, 0))).reshape(m, f2)
        aux_spec = tail_spec = pl.BlockSpec((tm, f2), lambda i, j: (i, 0))
        tail_shape = (m, f2)
    y, tail = pl.pallas_call(
        functools.partial(_post_body, chained, seq_len),
        grid=grid,
        in_specs=[rows(a) for a in acts] + [aux_spec] + [const(a) for a in weights],
        out_specs=[rows(acts[0]), tail_spec],
        out_shape=[jax.ShapeDtypeStruct(acts[0].shape, F32), jax.ShapeDtypeStruct(tail_shape, F32)],
        scratch_shapes=[pltpu.VMEM((8, f2), F32)],
        compiler_params=pltpu.CompilerParams(dimension_semantics=("parallel", "arbitrary"),
                                             vmem_limit_bytes=VMEM_LIMIT),
        name="post_mixer",
    )(*acts, aux, *weights)
    new_conv = tail[:, -2:] if chained else tail.reshape(b, t, f2)[:, -2:]
    return y.reshape(b, t, d), new_conv


def _trunk(x, shift0, wkv0, gla0, conv0, w):
    b, t, d = x.shape
    m = b * t
    xf = x.reshape(m, d)
    p_rw, p_qkv, p_og, p_gate, p_lga = _norm_proj(xf, w["norm_mix"], w["w_in_parts"])
    shift_cols = p_rw.shape[1]
    new_shift = p_rw.reshape(b, t, shift_cols)[:, -1]

    if t > TB:
        shape3 = lambda a: a.reshape(b, t, a.shape[1])
        prev = shift0[:, None, :]
    else:
        shape3 = lambda a: a[None]
        prev = jnp.pad(shift0[:, None, :], ((0, 0), (0, t - 1), (0, 0))).reshape(m, shift_cols)
    z_a, wkv_new, z_b, gla_new = _mixers(shape3(p_rw), prev, wkv0, shape3(p_qkv), shape3(p_lga), shape3(p_og),
                                          gla0, w, t)
    z_a = z_a.reshape(m, z_a.shape[-1])
    z_b = z_b.reshape(m, z_b.shape[-1])

    y, new_conv = _post(x, z_a, z_b, p_gate, conv0, w, t)
    return (y, new_shift[None], wkv_new[None], gla_new[None], new_conv[None])


def kernel(x_prompt, x_sample, state_rwkv_shift, state_rwkv_wkv, state_gla, state_ffn_conv, norm_mix, w_in, mu_shift, rwkv_w0, rwkv_w2, rwkv_a0, rwkv_a2, rwkv_g2, rwkv_k_k, rwkv_k_a, rwkv_r_k, rwkv_ln_w, rwkv_ln_b, gla_wg2, gla_bg, gla_norm_w, w_out_a, w_out_b, w_o, norm_ffn, ffn_w_up, ffn_conv_w, ffn_conv_b, ffn_w_down, norm_final):
    assert norm_mix.shape[0] == 1, "single-layer step"
    d = x_prompt.shape[-1]
    width = rwkv_w0.shape[-1]
    shift_cols = mu_shift.shape[-1]
    kw = gla_wg2.shape[-1]
    vw = w_out_b.shape[1]
    lora_g = gla_wg2.shape[1]
    lw, la = rwkv_w2.shape[1], rwkv_a2.shape[1]
    assert lw == HEAD and la == HEAD and rwkv_g2.shape[1] == LANES

    win = w_in[0].astype(BF16)
    c0 = shift_cols
    c1 = c0 + 2 * kw + vw
    c2 = c1 + lora_g
    c3 = c2 + vw
    w_lga = jnp.pad(win[:, c1:c2], ((0, 0), (0, LANES - lora_g)))
    w_in_parts = [win[:, :c0], win[:, c0:c1], win[:, c2:c3], win[:, c3:], w_lga]

    zw = jnp.zeros((lw, width), BF16)
    wlo = jnp.concatenate([jnp.concatenate([rwkv_w2[0].astype(BF16), zw], axis=1),
                           jnp.concatenate([zw, rwkv_a2[0].astype(BF16)], axis=1)], axis=0)
    vec = jnp.stack([rwkv_w0[0], rwkv_a0[0], rwkv_k_k[0], rwkv_k_a[0], rwkv_r_k[0].reshape(width),
                     rwkv_ln_w[0], rwkv_ln_b[0], jnp.zeros((width,), F32)])
    w = dict(
        norm_mix=norm_mix, w_in_parts=w_in_parts, mu_shift=mu_shift, rwkv_vec=vec, rwkv_wlo=wlo,
        rwkv_g2=rwkv_g2[0].astype(BF16),
        gla_wg2=jnp.pad(gla_wg2[0].astype(BF16), ((0, LANES - lora_g), (0, 0))), gla_bg=gla_bg,
        gla_norm_w=gla_norm_w,
        w_out_a=w_out_a[0].astype(BF16), w_out_b=w_out_b[0].astype(BF16), w_o=w_o[0].astype(BF16),
        norm_ffn=norm_ffn, ffn_w_up=ffn_w_up[0].astype(BF16), ffn_conv_w=ffn_conv_w[0],
        ffn_conv_b=ffn_conv_b, ffn_w_down=ffn_w_down[0].astype(BF16), norm_final=norm_final[None],
    )

    bp = x_prompt.shape[0]
    dt = x_prompt.dtype
    zeros = lambda s: jnp.zeros((bp,) + s.shape[2:], dt)
    y_p, shift_p, wkv_p, gla_p, conv_p = _trunk(
        x_prompt, zeros(state_rwkv_shift), zeros(state_rwkv_wkv), zeros(state_gla), zeros(state_ffn_conv), w)
    y_s, shift_s, wkv_s, gla_s, conv_s = _trunk(
        x_sample, state_rwkv_shift[0], state_rwkv_wkv[0], state_gla[0], state_ffn_conv[0], w)
    return (y_p, y_s, shift_p, wkv_p, gla_p, conv_p, shift_s, wkv_s, gla_s, conv_s)
```

```python
import functools
import itertools
import math

import jax
import jax.numpy as jnp
from jax import lax
from jax.experimental import pallas as pl
from jax.experimental.pallas import tpu as pltpu

F32 = jnp.float32
BF16 = jnp.bfloat16
LANES = 128
BF16_ROWS = 16
MXU_DEPTH = 256
TB = 128
TM = 256
POST_TM = 512
POST_TM_SHORT = 128
NCHUNK = 512
FFN_CHUNK = 256
VMEM_LIMIT = 56 * 1024 * 1024
NORM_EPS = 1e-6
HEAD = 64
GLA_GATE_TEMP = 16.0
PROMPT_CHUNK = 32
INV_BLOCK = 16
NN = ((1,), (0,))
NT = ((1,), (1,))
TN = ((0,), (0,))


def _dot(a, b, dims=NN):
    return lax.dot_general(a, b, (dims, ((), ())), preferred_element_type=F32)


def _split(x):
    hi = x.astype(BF16)
    lo = (x - hi.astype(F32)).astype(BF16)
    return hi, lo


def _dot3(a, b, dims=NN):
    ah, al = a if isinstance(a, tuple) else _split(a)
    bh, bl = b if isinstance(b, tuple) else _split(b)
    ka, kb = dims[0][0], dims[1][0]
    k = ah.shape[ka]
    if k % BF16_ROWS or 2 * k > MXU_DEPTH:
        return _dot(ah, bh, dims) + (_dot(ah, bl, dims) + _dot(al, bh, dims))
    if 3 * k <= MXU_DEPTH:
        return _dot(jnp.concatenate([ah, al, ah], axis=ka), jnp.concatenate([bh, bh, bl], axis=kb), dims)
    return _dot(jnp.concatenate([ah, al], axis=ka), jnp.concatenate([bh, bh], axis=kb), dims) + _dot(ah, bl, dims)


def _dot1(a, b, dims=NN):
    ah = a[0] if isinstance(a, tuple) else a.astype(BF16)
    bh = b[0] if isinstance(b, tuple) else b.astype(BF16)
    return _dot(ah, bh, dims)


def _split3(x):
    a1 = x.astype(BF16)
    r1 = x - a1.astype(F32)
    a2 = r1.astype(BF16)
    a3 = (r1 - a2.astype(F32)).astype(BF16)
    return a1, a2, a3


def _dot_exact_lhs(e, x):
    x1, x2, x3 = _split3(x)
    return _dot(jnp.concatenate([e, e], axis=1), jnp.concatenate([x1, x2], axis=0)) + _dot(e, x3)


def _dot_exact_rhs(x, e):
    return _dot(jnp.concatenate(_split(x), axis=1), jnp.concatenate([e, e], axis=0))


def _iota2(shape, dim):
    return lax.broadcasted_iota(jnp.int32, shape, dim)


def _same_block(n, size):
    sh = int(math.log2(size))
    return (_iota2((n, n), 0) >> sh) == (_iota2((n, n), 1) >> sh)


def _ones_where(mask):
    return jnp.where(mask, 1.0, 0.0).astype(BF16)


def _softplus(y):
    return jnp.maximum(y, 0.0) + jnp.log(1.0 + jnp.exp(-jnp.abs(y)))


def _rmsnorm(x, g):
    return x * lax.rsqrt(jnp.mean(x * x, axis=-1, keepdims=True) + NORM_EPS) * g


def _tile(x, p):
    return x[:, p * LANES:(p + 1) * LANES]


def _chunk_mats(chunk):
    same = _same_block(TB, chunk)
    r = _iota2((TB, TB), 0)
    c = _iota2((TB, TB), 1)
    incl = same & (c <= r)
    strict = same & (c < r)
    return incl, strict, _ones_where(incl), _ones_where(same)


def _half_masks():
    lane = _iota2((1, LANES), 1)
    lo = lane < HEAD
    return lo, jnp.logical_not(lo)


def _norm_proj_body(nw, x_ref, g_ref, *refs):
    w_refs, o_refs = refs[:nw], refs[nw:]
    hb = _rmsnorm(x_ref[...], g_ref[...]).astype(BF16)
    for w_ref, o_ref in zip(w_refs, o_refs):
        n = w_ref.shape[1]
        for n0 in range(0, n, NCHUNK):
            n1 = min(n0 + NCHUNK, n)
            o_ref[:, n0:n1] = _dot(hb, w_ref[:, n0:n1])


def _norm_proj(x, g, weights):
    m, d = x.shape
    nw = len(weights)
    return pl.pallas_call(
        functools.partial(_norm_proj_body, nw),
        grid=(m // TM,),
        in_specs=[pl.BlockSpec((TM, d), lambda i: (i, 0)), pl.BlockSpec((1, d), lambda i: (0, 0))]
        + [pl.BlockSpec(w.shape, lambda i: (0, 0)) for w in weights],
        out_specs=[pl.BlockSpec((TM, w.shape[1]), lambda i: (i, 0)) for w in weights],
        out_shape=[jax.ShapeDtypeStruct((m, w.shape[1]), F32) for w in weights],
        compiler_params=pltpu.CompilerParams(dimension_semantics=("parallel",), vmem_limit_bytes=VMEM_LIMIT),
        name="norm_proj",
    )(x, g, *weights)


def _wkv_pair_load(ref, idx, p):
    a, b = ref[idx + (2 * p,)], ref[idx + (2 * p + 1,)]
    z = jnp.zeros_like(a)
    return jnp.concatenate([jnp.concatenate([a, z], axis=1), jnp.concatenate([z, b], axis=1)], axis=0)


def _wkv_pair_store(ref, idx, p, s):
    n = s.shape[0] // 2
    ref[idx + (2 * p,)] = s[:n, :n]
    ref[idx + (2 * p + 1,)] = s[n:, n:]


def _rwkv_body(chunk, chained, x_ref, prev_ref, s0_ref, mu_ref, vec_ref, wlo_ref, g2_ref,
               z_ref, sout_ref, carry_ref, s_scr):
    nchunk = TB // chunk
    width = z_ref.shape[-1]
    npair = width // LANES
    x = x_ref[...]
    row = _iota2((TB, 1), 0)
    rolled = pltpu.roll(x, 1, axis=0)
    if chained:
        @pl.when(pl.program_id(1) == 0)
        def _():
            carry_ref[...] = prev_ref[...]
            for p in range(npair):
                s_scr[p] = _wkv_pair_load(s0_ref, (), p)
        prev = jnp.where(row == 0, carry_ref[...], rolled)
        carry_ref[...] = x[TB - 1:TB, :]
    else:
        prev = jnp.where((row & (chunk - 1)) == 0, prev_ref[...], rolled)
    xs = x + (prev - x) * mu_ref[...]
    yield

    r = xs[:, 0:width]
    k = xs[:, width:2 * width]
    v = xs[:, 2 * width:3 * width]
    lora = xs[:, 3 * width:3 * width + LANES]
    lg = xs[:, 3 * width + LANES:3 * width + 2 * LANES]
    lo_half, hi_half = _half_masks()
    lora = jnp.where(lo_half, jnp.tanh(lora), lora)
    wa = _dot(lora.astype(BF16), wlo_ref[...])
    w0, a0, k_k, k_a = vec_ref[0:1, :], vec_ref[1:2, :], vec_ref[2:3, :], vec_ref[3:4, :]
    r_k, ln_w, ln_b = vec_ref[4:5, :], vec_ref[5:6, :], vec_ref[6:7, :]
    wlog = -_softplus(-(w0 + wa[:, :width])) - 0.5
    logw = -jnp.exp(wlog)
    asig = jax.nn.sigmoid(a0 + wa[:, width:])
    g = _dot(jax.nn.sigmoid(lg).astype(BF16), g2_ref[...])
    yield

    seg = _ones_where(_same_block(LANES, HEAD))

    def headsum(t):
        return jnp.concatenate([_dot_exact_rhs(_tile(t, p), seg) for p in range(npair)], axis=1)

    kk = k * k_k
    kk = kk / jnp.maximum(jnp.sqrt(headsum(kk * kk)), 1e-12)
    kh = k * (1.0 + (asig - 1.0) * k_a)
    a_vec = -kk
    b_vec = kk * asig
    yield

    incl, strict, incl_m, same_m = _chunk_mats(chunk)
    cums = _dot_exact_lhs(jnp.concatenate([incl_m, same_m], axis=0), logw)
    cum, cum_c = cums[:TB], cums[TB:]
    e_in = jnp.exp(cum)
    e_out = jnp.exp(-cum)
    e_end = jnp.exp(cum_c - cum)
    yield
    rt = r * e_in
    at = a_vec * jnp.exp(cum - logw)
    bt = b_vec * e_out
    kt = kh * e_out
    bp = b_vec * e_end
    kp = kh * e_end
    e_c = jnp.exp(cum_c)
    yield

    def msk(t, h):
        return jnp.where(lo_half if h == 0 else hi_half, t, 0.0)

    eye = jnp.where(_iota2((TB, TB), 0) == _iota2((TB, TB), 1), 1.0, 0.0)
    pairs = range(npair)
    heads = [(p, h) for p in pairs for h in range(2)]
    at_t, rt_t, bt_t, kt_t, v_t, bp_t, kp_t, ec_t = ([_tile(t, p) for p in pairs]
                                                     for t in (at, rt, bt, kt, v, bp, kp, e_c))
    bk = [_split(jnp.concatenate([msk(bt_t[p], 0), msk(bt_t[p], 1), msk(kt_t[p], 0), msk(kt_t[p], 1)], axis=0))
          for p in pairs]
    sc_a = [_dot3(at_t[p], bk[p], NT) for p in pairs]
    sc_r = [_dot1(rt_t[p], bk[p], NT) for p in pairs]
    a_ab = [jnp.where(strict, sc_a[p][:, h * TB:(h + 1) * TB], 0.0) for p, h in heads]
    a_ak = [jnp.where(strict, sc_a[p][:, (2 + h) * TB:(3 + h) * TB], 0.0) for p, h in heads]
    p_rb = [jnp.where(incl, sc_r[p][:, h * TB:(h + 1) * TB], 0.0) for p, h in heads]
    p_rk = [jnp.where(incl, sc_r[p][:, (2 + h) * TB:(3 + h) * TB], 0.0) for p, h in heads]
    yield
    base = min(chunk, INV_BLOCK)
    assert chunk in (base, 2 * base)
    in_base = _same_block(TB, base)
    n_in = [jnp.where(in_base, n, 0.0) for n in a_ab]
    t_inv = [eye + n for n in n_in]
    levels = int(math.log2(base))
    powers = [_dot3(n, n) for n in n_in] if levels > 1 else n_in
    yield
    for level in range(1, levels):
        if level + 1 < levels:
            both = [_dot3(x, jnp.concatenate([x, t], axis=1)) for x, t in zip(powers, t_inv)]
            powers = [b[:, :TB] for b in both]
            t_inv = [t + b[:, TB:] for t, b in zip(t_inv, both)]
        else:
            t_inv = [t + _dot3(x, t) for x, t in zip(powers, t_inv)]
        yield
    if chunk > base:
        n_off = [_dot3(n - ni, t) for n, ni, t in zip(a_ab, n_in, t_inv)]
        yield
        t_inv = [t + _dot3(t, no) for t, no in zip(t_inv, n_off)]
    yield
    av = [_dot3(jnp.concatenate(a_ak[2 * p:2 * p + 2], axis=1),
                jnp.concatenate([msk(v_t[p], 0), msk(v_t[p], 1)], axis=0)) for p in pairs]
    tay = [_dot3(jnp.concatenate(t_inv[2 * p:2 * p + 2], axis=1),
                 jnp.concatenate([jnp.concatenate([msk(at_t[p], h), msk(av[p], h)], axis=1) for h in range(2)], axis=0))
           for p in pairs]
    yield
    ta = [t[:, :LANES] for t in tay]
    yy = [t[:, LANES:] for t in tay]
    qz =[_dot1(jnp.concatenate(p_rb[2 * p:2 * p + 2], axis=1),
                jnp.concatenate([jnp.concatenate([msk(ta[p], h), msk(yy[p], h)], axis=1) for h in range(2)], axis=0))
          for p in pairs]
    zv = [_dot1(jnp.concatenate(p_rk[2 * p:2 * p + 2], axis=1),
                jnp.concatenate([msk(v_t[p], 0), msk(v_t[p], 1)], axis=0)) for p in pairs]
    qq = [rt_t[p] + qz[p][:, :LANES] for p in pairs]
    zz = [qz[p][:, LANES:] + zv[p] for p in pairs]

    bd = _same_block(LANES, HEAD)
    diag = _iota2((LANES, LANES), 0) == _iota2((LANES, LANES), 1)
    rows = lambda t, c: t[c * chunk:(c + 1) * chunk]
    m_mat = [[jnp.where(diag, ec_t[p][c * chunk:c * chunk + 1, :], 0.0)
              + jnp.where(bd, _dot3(rows(ta[p], c), rows(bp_t[p], c), TN), 0.0) for p in pairs]
             for c in range(nchunk)]
    yield
    n_mat = [[jnp.where(bd, _dot3(jnp.concatenate([rows(yy[p], c), rows(v_t[p], c)], axis=0),
                                  jnp.concatenate([rows(bp_t[p], c), rows(kp_t[p], c)], axis=0), TN), 0.0)
              for p in pairs] for c in range(nchunk)]
    yield
    state = [s_scr[p] for p in pairs] if chained else None
    o_rows = [[] for _ in pairs]
    for c in range(nchunk):
        for p in pairs:
            s = _split(state[p] if chained else _wkv_pair_load(s0_ref, (c,), p))
            o_rows[p].append(_dot1(rows(qq[p], c), s, NT) + rows(zz[p], c))
            s = _dot3(s, m_mat[c][p]) + n_mat[c][p]
            if chained:
                state[p] = s
            else:
                _wkv_pair_store(sout_ref, (c,), p, s)
        if chained:
            yield
    if chained:
        for p in pairs:
            s_scr[p] = state[p]

        @pl.when(pl.program_id(1) == pl.num_programs(1) - 1)
        def _():
            for p in pairs:
                _wkv_pair_store(sout_ref, (), p, state[p])
    o = jnp.concatenate([jnp.concatenate(o_rows[p], axis=0) for p in pairs], axis=1)
    yield

    inv_n = 1.0 / HEAD
    mean = headsum(o) * inv_n
    d = o - mean
    var = headsum(d * d) * inv_n
    o = d * lax.rsqrt(var + 1e-5 * HEAD) * ln_w + ln_b
    o = o + headsum(r * kh * r_k) * v
    z_ref[...] = o * g


def _gla_pair_load(ref, idx, p):
    return jnp.concatenate([ref[idx + (2 * p,)], ref[idx + (2 * p + 1,)]], axis=0).T


def _gla_pair_store(ref, idx, p, s):
    st = s.T
    dk = st.shape[0] // 2
    ref[idx + (2 * p,)] = st[:dk]
    ref[idx + (2 * p + 1,)] = st[dk:]


def _gla_body(chunk, chained, qkv_ref, lga_ref, og_ref, s0_ref, wg2_ref, bg_ref, nw_ref,
              z_ref, sout_ref, s_scr):
    nchunk = TB // chunk
    vw = z_ref.shape[-1]
    nhead = vw // LANES
    npair = nhead // 2
    kw = npair * LANES
    qkv = qkv_ref[...]
    q = qkv[:, 0:kw] * (HEAD ** -0.5)
    k = qkv[:, kw:2 * kw]
    v = qkv[:, 2 * kw:2 * kw + vw]
    gl = _dot(lga_ref[...].astype(BF16), wg2_ref[...]) + bg_ref[...]
    log_a = -_softplus(-gl) * (1.0 / GLA_GATE_TEMP)

    incl, _, incl_m, same_m = _chunk_mats(chunk)
    cums = _dot_exact_lhs(jnp.concatenate([incl_m, same_m], axis=0), log_a)
    cum, cum_c = cums[:TB], cums[TB:]
    qt = q * jnp.exp(cum)
    kt = k * jnp.exp(-cum)
    kp = k * jnp.exp(cum_c - cum)
    e_c = jnp.exp(cum_c)
    yield
    lo_half, hi_half = _half_masks()

    def msk(t, h):
        return jnp.where(lo_half if h == 0 else hi_half, t, 0.0)

    if chained:
        @pl.when(pl.program_id(1) == 0)
        def _():
            for p in range(npair):
                s_scr[p] = _gla_pair_load(s0_ref, (), p)

    pairs = range(npair)
    heads = [(p, h) for p in pairs for h in range(2)]
    rows = lambda t, c: t[c * chunk:(c + 1) * chunk]
    kt_t, kp_t, ec_t = ([_tile(t, p) for p in pairs] for t in (kt, kp, e_c))
    v_t = [_tile(v, h) for h in range(nhead)]
    qm = [msk(_tile(qt, p), h) for p, h in heads]
    sc = [_dot1(jnp.concatenate(qm[2 * p:2 * p + 2], axis=0), kt_t[p], NT) for p in pairs]
    intra = [_dot1(jnp.where(incl, sc[p][h * TB:(h + 1) * TB], 0.0), v_t[2 * p + h]) for p, h in heads]
    upd = [[jnp.where(lo_half, _dot3(rows(v_t[2 * p], c), rows(kp_t[p], c), TN),
                      _dot3(rows(v_t[2 * p + 1], c), rows(kp_t[p], c), TN)) for p in pairs]
           for c in range(nchunk)]
    yield
    state = [s_scr[p] for p in pairs] if chained else None
    inter = [[] for _ in range(nhead)]
    for c in range(nchunk):
        for p in pairs:
            s = state[p] if chained else _gla_pair_load(s0_ref, (c,), p)
            io = _dot1(jnp.concatenate([rows(qm[2 * p], c), rows(qm[2 * p + 1], c)], axis=0), s, NT)
            inter[2 * p].append(io[:chunk])
            inter[2 * p + 1].append(io[chunk:])
            s = s * ec_t[p][c * chunk:c * chunk + 1, :] + upd[c][p]
            if chained:
                state[p] = s
            else:
                _gla_pair_store(sout_ref, (c,), p, s)
    if chained:
        for p in pairs:
            s_scr[p] = state[p]

        @pl.when(pl.program_id(1) == pl.num_programs(1) - 1)
        def _():
            for p in pairs:
                _gla_pair_store(sout_ref, (), p, state[p])
    o_heads = [intra[h] + jnp.concatenate(inter[h], axis=0) for h in range(nhead)]
    yield

    og = og_ref[...]
    for h in range(nhead):
        o = o_heads[h]
        o = o * lax.rsqrt(jnp.mean(o * o, axis=-1, keepdims=True) + NORM_EPS) * nw_ref[...]
        z_ref[:, h * LANES:(h + 1) * LANES] = o * jax.nn.silu(_tile(og, h))


def _mixers_body(chunk, chained, x_ref, prev_ref, wkv0_ref, qkv_ref, lga_ref, og_ref, gla0_ref,
                 mu_ref, vec_ref, wlo_ref, g2_ref, wg2_ref, bg_ref, nw_ref,
                 za_ref, wkv_ref, zb_ref, gla_ref, carry_ref, wkv_scr, gla_scr):
    rwkv = _rwkv_body(chunk, chained, x_ref, prev_ref, wkv0_ref, mu_ref, vec_ref, wlo_ref, g2_ref, za_ref, wkv_ref,
                      carry_ref, wkv_scr)
    gla = _gla_body(chunk, chained, qkv_ref, lga_ref, og_ref, gla0_ref, wg2_ref, bg_ref, nw_ref, zb_ref, gla_ref,
                    gla_scr)
    for _ in itertools.zip_longest(rwkv, gla):
        pass


def _mixers(x_rw, prev, wkv0, qkv, lga, og, gla0, w, seq_len):
    width = wkv0.shape[1] * wkv0.shape[2]
    vw = og.shape[-1]
    cols = x_rw.shape[-1]
    chained = seq_len > TB
    if chained:
        chunk = PROMPT_CHUNK
        b, t, _ = x_rw.shape
        grid = (b, t // TB)
        rows = lambda n: pl.BlockSpec((None, TB, n), lambda i, j: (i, j, 0))
        prev_spec = pl.BlockSpec((None, 1, cols), lambda i, j: (i, 0, 0))
        st = lambda a: pl.BlockSpec((None,) + a.shape[1:], lambda i, j: (i, 0, 0, 0))
        lead = (b, t)
    else:
        chunk = seq_len
        m = x_rw.shape[1]
        assert m % TB == 0
        grid = (m // TB, 1)
        rows = lambda n: pl.BlockSpec((None, TB, n), lambda i, j: (0, i, 0))
        prev_spec = pl.BlockSpec((TB, cols), lambda i, j: (i, 0))
        st = lambda a: pl.BlockSpec((TB // chunk,) + a.shape[1:], lambda i, j: (i, 0, 0, 0))
        lead = (1, m)
    const = lambda a: pl.BlockSpec(a.shape, lambda i, j: (0,) * a.ndim)
    consts = [w["mu_shift"], w["rwkv_vec"], w["rwkv_wlo"], w["rwkv_g2"], w["gla_wg2"], w["gla_bg"], w["gla_norm_w"]]
    return pl.pallas_call(
        functools.partial(_mixers_body, chunk, chained),
        grid=grid,
        in_specs=[rows(cols), prev_spec, st(wkv0), rows(qkv.shape[-1]), rows(lga.shape[-1]), rows(vw), st(gla0)]
        + [const(a) for a in consts],
        out_specs=[rows(width), st(wkv0), rows(vw), st(gla0)],
        out_shape=[jax.ShapeDtypeStruct(lead + (width,), F32), jax.ShapeDtypeStruct(wkv0.shape, F32),
                   jax.ShapeDtypeStruct(lead + (vw,), F32), jax.ShapeDtypeStruct(gla0.shape, F32)],
        scratch_shapes=[pltpu.VMEM((1, cols), F32), pltpu.VMEM((wkv0.shape[1] // 2, LANES, LANES), F32),
                        pltpu.VMEM((gla0.shape[1] // 2, LANES, LANES), F32)],
        compiler_params=pltpu.CompilerParams(dimension_semantics=("parallel", "arbitrary"),
                                             vmem_limit_bytes=VMEM_LIMIT),
        name="mixers",
    )(x_rw, prev, wkv0, qkv, lga, og, gla0, *consts)


def _post_body(chained, seq_len, x_ref, za_ref, zb_ref, gate_ref, aux_ref, woa_ref, wob_ref, wo_ref, g_ref,
               wup_ref, cw_ref, cb_ref, wd_ref, nf_ref, y_ref, tail_ref, carry_ref, act_ref):
    tm, d = x_ref.shape
    hidden = wd_ref.shape[0]
    ya = _dot(za_ref[...].astype(BF16), woa_ref[...])
    yb = _dot(zb_ref[...].astype(BF16), wob_ref[...])
    gate = gate_ref[...]
    merged = jax.nn.sigmoid(gate[:, :d]) * ya + jax.nn.sigmoid(gate[:, d:]) * yb
    x1 = x_ref[...] + _dot(merged.astype(BF16), wo_ref[...])
    hb = _rmsnorm(x1, g_ref[...]).astype(BF16)

    if chained:
        @pl.when(pl.program_id(1) == 0)
        def _():
            carry_ref[0:6, :] = jnp.zeros((6, carry_ref.shape[1]), F32)
            carry_ref[6:8, :] = aux_ref[...]
    else:
        t = _iota2((tm, 1), 0) & (seq_len - 1)

    def conv(u, cs):
        if chained:
            ext = jnp.concatenate([carry_ref[:, cs], u], axis=0)
            carry_ref[:, cs] = u[tm - 8:, :]
            tail_ref[:, cs] = u[tm - 8:, :]
            inner = cw_ref[1:2, cs] * ext + pltpu.roll(cw_ref[0:1, cs] * ext, 1, axis=0)
            return (cb_ref[:, cs] + cw_ref[2:3, cs] * ext + pltpu.roll(inner, 1, axis=0))[8:]
        else:
            pa = aux_ref[:, cs]
            p1 = jnp.where(t == 0, pltpu.roll(pa, tm - 1, axis=0), pltpu.roll(u, 1, axis=0))
            p2 = jnp.where(t < 2, pa, pltpu.roll(u, 2, axis=0))
            tail_ref[:, cs] = u
        return cb_ref[:, cs] + cw_ref[0:1, cs] * p2 + cw_ref[1:2, cs] * p1 + cw_ref[2:3, cs] * u

    def up(j0):
        cols = (slice(j0, j0 + FFN_CHUNK), slice(hidden + j0, hidden + j0 + FFN_CHUNK))
        return [(_dot(hb, wup_ref[:, cs]), cs) for cs in cols]

    starts = list(range(0, hidden, FFN_CHUNK))
    group_end = [starts[(len(starts) + 1) // 2 - 1], starts[-1]]
    out = x1
    nxt = up(starts[0])
    k0 = 0
    for i, j0 in enumerate(starts):
        cur = nxt
        if i + 1 < len(starts):
            nxt = up(starts[i + 1])
        val, gat = (conv(u, cs) for u, cs in cur)
        act_ref[:, j0:j0 + FFN_CHUNK] = (jax.nn.gelu(gat) * val).astype(BF16)
        if j0 in group_end:
            k1 = j0 + FFN_CHUNK
            out = out + _dot(act_ref[:, k0:k1], wd_ref[k0:k1, :])
            k0 = k1
    y_ref[...] = _rmsnorm(out, nf_ref[...])


def _post(x, za, zb, gate, conv_state, w, seq_len):
    b, t, d = x.shape
    m = b * t
    f2 = conv_state.shape[-1]
    chained = seq_len >= POST_TM
    assert seq_len % POST_TM == 0 if chained else POST_TM_SHORT % seq_len == 0
    weights = [w["w_out_a"], w["w_out_b"], w["w_o"], w["norm_ffn"], w["ffn_w_up"], w["ffn_conv_w"],
               w["ffn_conv_b"], w["ffn_w_down"], w["norm_final"]]
    const = lambda a: pl.BlockSpec(a.shape, lambda i, j: (0,) * a.ndim, pipeline_mode=pl.Buffered(1))
    if chained:
        tm = POST_TM
        grid = (b, t // tm)
        rows = lambda a: pl.BlockSpec((None, tm, a.shape[-1]), lambda i, j: (i, j, 0))
        acts = [x, za.reshape(b, t, -1), zb.reshape(b, t, -1), gate.reshape(b, t, -1)]
        aux, aux_spec = conv_state, pl.BlockSpec((None, 2, f2), lambda i, j: (i, 0, 0))
        tail_spec = pl.BlockSpec((None, 8, f2), lambda i, j: (i, 0, 0))
        tail_shape = (b, 8, f2)
    else:
        tm = POST_TM_SHORT
        grid = (m // tm, 1)
        rows = lambda a: pl.BlockSpec((tm, a.shape[-1]), lambda i, j: (i, 0))
        acts = [x.reshape(m, d), za, zb, gate]
        aux = jnp.pad(conv_state, ((0, 0), (0, seq_len - 2), (0, 0))).reshape(m, f2)
        aux_spec = tail_spec = pl.BlockSpec((tm, f2), lambda i, j: (i, 0))
        tail_shape = (m, f2)
    y, tail = pl.pallas_call(
        functools.partial(_post_body, chained, seq_len),
        grid=grid,
        in_specs=[rows(a) for a in acts] + [aux_spec] + [const(a) for a in weights],
        out_specs=[rows(acts[0]), tail_spec],
        out_shape=[jax.ShapeDtypeStruct(acts[0].shape, F32), jax.ShapeDtypeStruct(tail_shape, F32)],
        scratch_shapes=[pltpu.VMEM((8, f2), F32), pltpu.VMEM((tm, f2 // 2), BF16)],
        compiler_params=pltpu.CompilerParams(dimension_semantics=("parallel", "arbitrary"),
                                             vmem_limit_bytes=VMEM_LIMIT),
        name="post_mixer",
    )(*acts, aux, *weights)
    new_conv = tail[:, -2:] if chained else tail.reshape(b, t, f2)[:, -2:]
    return y.reshape(b, t, d), new_conv


def _trunk(x, shift0, wkv0, gla0, conv0, w):
    b, t, d = x.shape
    m = b * t
    xf = x.reshape(m, d)
    p_rw, p_qkv, p_og, p_gate, p_lga = _norm_proj(xf, w["norm_mix"], w["w_in_parts"])
    shift_cols = p_rw.shape[1]
    new_shift = p_rw.reshape(b, t, shift_cols)[:, -1]

    if t > TB:
        shape3 = lambda a: a.reshape(b, t, a.shape[1])
        prev = shift0[:, None, :]
    else:
        shape3 = lambda a: a[None]
        prev = jnp.pad(shift0[:, None, :], ((0, 0), (0, t - 1), (0, 0))).reshape(m, shift_cols)
    z_a, wkv_new, z_b, gla_new = _mixers(shape3(p_rw), prev, wkv0, shape3(p_qkv), shape3(p_lga), shape3(p_og),
                                          gla0, w, t)
    z_a = z_a.reshape(m, z_a.shape[-1])
    z_b = z_b.reshape(m, z_b.shape[-1])

    y, new_conv = _post(x, z_a, z_b, p_gate, conv0, w, t)
    return (y, new_shift[None], wkv_new[None], gla_new[None], new_conv[None])


def kernel(x_prompt, x_sample, state_rwkv_shift, state_rwkv_wkv, state_gla, state_ffn_conv, norm_mix, w_in, mu_shift, rwkv_w0, rwkv_w2, rwkv_a0, rwkv_a2, rwkv_g2, rwkv_k_k, rwkv_k_a, rwkv_r_k, rwkv_ln_w, rwkv_ln_b, gla_wg2, gla_bg, gla_norm_w, w_out_a, w_out_b, w_o, norm_ffn, ffn_w_up, ffn_conv_w, ffn_conv_b, ffn_w_down, norm_final):
    assert norm_mix.shape[0] == 1, "single-layer step"
    d = x_prompt.shape[-1]
    width = rwkv_w0.shape[-1]
    shift_cols = mu_shift.shape[-1]
    kw = gla_wg2.shape[-1]
    vw = w_out_b.shape[1]
    lora_g = gla_wg2.shape[1]
    lw, la = rwkv_w2.shape[1], rwkv_a2.shape[1]
    assert lw == HEAD and la == HEAD and rwkv_g2.shape[1] == LANES

    win = w_in[0].astype(BF16)
    c0 = shift_cols
    c1 = c0 + 2 * kw + vw
    c2 = c1 + lora_g
    c3 = c2 + vw
    w_lga = jnp.pad(win[:, c1:c2], ((0, 0), (0, LANES - lora_g)))
    w_in_parts = [win[:, :c0], win[:, c0:c1], win[:, c2:c3], win[:, c3:], w_lga]

    zw = jnp.zeros((lw, width), BF16)
    wlo = jnp.concatenate([jnp.concatenate([rwkv_w2[0].astype(BF16), zw], axis=1),
                           jnp.concatenate([zw, rwkv_a2[0].astype(BF16)], axis=1)], axis=0)
    vec = jnp.stack([rwkv_w0[0], rwkv_a0[0], rwkv_k_k[0], rwkv_k_a[0], rwkv_r_k[0].reshape(width),
                     rwkv_ln_w[0], rwkv_ln_b[0], jnp.zeros((width,), F32)])
    w = dict(
        norm_mix=norm_mix, w_in_parts=w_in_parts, mu_shift=mu_shift, rwkv_vec=vec, rwkv_wlo=wlo,
        rwkv_g2=rwkv_g2[0].astype(BF16),
        gla_wg2=jnp.pad(gla_wg2[0].astype(BF16), ((0, LANES - lora_g), (0, 0))), gla_bg=gla_bg,
        gla_norm_w=gla_norm_w,
        w_out_a=w_out_a[0].astype(BF16), w_out_b=w_out_b[0].astype(BF16), w_o=w_o[0].astype(BF16),
        norm_ffn=norm_ffn, ffn_w_up=ffn_w_up[0].astype(BF16), ffn_conv_w=ffn_conv_w[0],
        ffn_conv_b=ffn_conv_b, ffn_w_down=ffn_w_down[0].astype(BF16), norm_final=norm_final[None],
    )

    bp = x_prompt.shape[0]
    dt = x_prompt.dtype
    zeros = lambda s: jnp.zeros((bp,) + s.shape[2:], dt)
    y_p, shift_p, wkv_p, gla_p, conv_p = _trunk(
        x_prompt, zeros(state_rwkv_shift), zeros(state_rwkv_wkv), zeros(state_gla), zeros(state_ffn_conv), w)
    y_s, shift_s, wkv_s, gla_s, conv_s = _trunk(
        x_sample, state_rwkv_shift[0], state_rwkv_wkv[0], state_gla[0], state_ffn_conv[0], w)
    return (y_p, y_s, shift_p, wkv_p, gla_p, conv_p, shift_s, wkv_s, gla_s, conv_s)
```

```python
import functools
import itertools
import math

import jax
import jax.numpy as jnp
from jax import lax
from jax.experimental import pallas as pl
from jax.experimental.pallas import tpu as pltpu

F32 = jnp.float32
BF16 = jnp.bfloat16
LANES = 128
BF16_ROWS = 16
MXU_DEPTH = 256
TB = 128
TM = 512
POST_TM = 512
POST_TM_SHORT = 128
NCHUNK = 512
FFN_CHUNK = 256
VMEM_LIMIT = 56 * 1024 * 1024
NORM_EPS = 1e-6
HEAD = 64
GLA_GATE_TEMP = 16.0
PROMPT_CHUNK = 32
INV_BLOCK = 16
NN = ((1,), (0,))
NT = ((1,), (1,))
TN = ((0,), (0,))


def _dot(a, b, dims=NN):
    return lax.dot_general(a, b, (dims, ((), ())), preferred_element_type=F32)


def _split(x):
    hi = x.astype(BF16)
    lo = (x - hi.astype(F32)).astype(BF16)
    return hi, lo


def _dot3(a, b, dims=NN):
    ka, kb = dims[0][0], dims[1][0]
    if dims == TN and not isinstance(a, tuple) and not isinstance(b, tuple) and a.shape[0] % BF16_ROWS:
        pad = BF16_ROWS - a.shape[0] % BF16_ROWS
        a = jnp.concatenate([a, jnp.zeros((pad, a.shape[1]), F32)], axis=0)
        b = jnp.concatenate([b, jnp.zeros((pad, b.shape[1]), F32)], axis=0)
    ah, al = a if isinstance(a, tuple) else _split(a)
    bh, bl = b if isinstance(b, tuple) else _split(b)
    k = ah.shape[ka]
    if k % BF16_ROWS or 2 * k > MXU_DEPTH:
        return _dot(ah, bh, dims) + (_dot(ah, bl, dims) + _dot(al, bh, dims))
    if 3 * k <= MXU_DEPTH:
        return _dot(jnp.concatenate([ah, al, ah], axis=ka), jnp.concatenate([bh, bh, bl], axis=kb), dims)
    return _dot(jnp.concatenate([ah, al], axis=ka), jnp.concatenate([bh, bh], axis=kb), dims) + _dot(ah, bl, dims)


def _dot1(a, b, dims=NN):
    ah = a[0] if isinstance(a, tuple) else a.astype(BF16)
    bh = b[0] if isinstance(b, tuple) else b.astype(BF16)
    return _dot(ah, bh, dims)


def _split3(x):
    a1 = x.astype(BF16)
    r1 = x - a1.astype(F32)
    a2 = r1.astype(BF16)
    a3 = (r1 - a2.astype(F32)).astype(BF16)
    return a1, a2, a3


def _dot_exact_lhs(e, x):
    x1, x2, x3 = _split3(x)
    return _dot(jnp.concatenate([e, e], axis=1), jnp.concatenate([x1, x2], axis=0)) + _dot(e, x3)


def _dot_exact_rhs(x, e):
    return _dot(jnp.concatenate(_split(x), axis=1), jnp.concatenate([e, e], axis=0))


def _iota2(shape, dim):
    return lax.broadcasted_iota(jnp.int32, shape, dim)


def _same_block(n, size):
    sh = int(math.log2(size))
    return (_iota2((n, n), 0) >> sh) == (_iota2((n, n), 1) >> sh)


def _ones_where(mask):
    return jnp.where(mask, 1.0, 0.0).astype(BF16)


def _softplus(y):
    return jnp.maximum(y, 0.0) + jnp.log(1.0 + jnp.exp(-jnp.abs(y)))


def _rmsnorm(x, g):
    return x * lax.rsqrt(jnp.mean(x * x, axis=-1, keepdims=True) + NORM_EPS) * g


def _tile(x, p):
    return x[:, p * LANES:(p + 1) * LANES]


def _chunk_mats(chunk):
    same = _same_block(TB, chunk)
    r = _iota2((TB, TB), 0)
    c = _iota2((TB, TB), 1)
    incl = same & (c <= r)
    strict = same & (c < r)
    return incl, strict, _ones_where(incl), _ones_where(same)


def _half_masks():
    lane = _iota2((1, LANES), 1)
    lo = lane < HEAD
    return lo, jnp.logical_not(lo)


def _norm_proj_body(nw, x_ref, g_ref, *refs):
    w_refs, o_refs = refs[:nw], refs[nw:]
    hb = _rmsnorm(x_ref[...], g_ref[...]).astype(BF16)
    for w_ref, o_ref in zip(w_refs, o_refs):
        n = w_ref.shape[1]
        for n0 in range(0, n, NCHUNK):
            n1 = min(n0 + NCHUNK, n)
            o_ref[:, n0:n1] = _dot(hb, w_ref[:, n0:n1])


def _norm_proj(x, g, weights):
    m, d = x.shape
    nw = len(weights)
    return pl.pallas_call(
        functools.partial(_norm_proj_body, nw),
        grid=(m // TM,),
        in_specs=[pl.BlockSpec((TM, d), lambda i: (i, 0)), pl.BlockSpec((1, d), lambda i: (0, 0))]
        + [pl.BlockSpec(w.shape, lambda i: (0, 0), pipeline_mode=pl.Buffered(1)) for w in weights],
        out_specs=[pl.BlockSpec((TM, w.shape[1]), lambda i: (i, 0)) for w in weights],
        out_shape=[jax.ShapeDtypeStruct((m, w.shape[1]), F32) for w in weights],
        compiler_params=pltpu.CompilerParams(dimension_semantics=("parallel",), vmem_limit_bytes=VMEM_LIMIT),
        name="norm_proj",
    )(x, g, *weights)


def _wkv_pair_load(ref, idx, p):
    a, b = ref[idx + (2 * p,)], ref[idx + (2 * p + 1,)]
    z = jnp.zeros_like(a)
    return jnp.concatenate([jnp.concatenate([a, z], axis=1), jnp.concatenate([z, b], axis=1)], axis=0)


def _wkv_pair_store(ref, idx, p, s):
    n = s.shape[0] // 2
    ref[idx + (2 * p,)] = s[:n, :n]
    ref[idx + (2 * p + 1,)] = s[n:, n:]


def _rwkv_body(chunk, chained, x_ref, prev_ref, s0_ref, mu_ref, vec_ref, wlo_ref, g2_ref,
               z_ref, sout_ref, carry_ref, s_scr):
    nchunk = TB // chunk
    width = z_ref.shape[-1]
    npair = width // LANES
    x = x_ref[...]
    row = _iota2((TB, 1), 0)
    rolled = pltpu.roll(x, 1, axis=0)
    if chained:
        @pl.when(pl.program_id(1) == 0)
        def _():
            carry_ref[...] = prev_ref[...]
            for p in range(npair):
                s_scr[p] = _wkv_pair_load(s0_ref, (), p)
        prev = jnp.where(row == 0, carry_ref[...], rolled)
        carry_ref[...] = x[TB - 1:TB, :]
    else:
        prev = jnp.where((row & (chunk - 1)) == 0, prev_ref[...], rolled)
    xs = x + (prev - x) * mu_ref[...]
    yield

    r = xs[:, 0:width]
    k = xs[:, width:2 * width]
    v = xs[:, 2 * width:3 * width]
    lora = xs[:, 3 * width:3 * width + LANES]
    lg = xs[:, 3 * width + LANES:3 * width + 2 * LANES]
    lo_half, hi_half = _half_masks()
    lora = jnp.where(lo_half, jnp.tanh(lora), lora)
    wa = _dot(lora.astype(BF16), wlo_ref[...])
    w0, a0, k_k, k_a = vec_ref[0:1, :], vec_ref[1:2, :], vec_ref[2:3, :], vec_ref[3:4, :]
    r_k, ln_w, ln_b = vec_ref[4:5, :], vec_ref[5:6, :], vec_ref[6:7, :]
    wlog = -_softplus(-(w0 + wa[:, :width])) - 0.5
    logw = -jnp.exp(wlog)
    asig = jax.nn.sigmoid(a0 + wa[:, width:])
    g = _dot(jax.nn.sigmoid(lg).astype(BF16), g2_ref[...])
    yield

    seg = _ones_where(_same_block(LANES, HEAD))

    def headsum(t):
        return jnp.concatenate([_dot_exact_rhs(_tile(t, p), seg) for p in range(npair)], axis=1)

    kk = k * k_k
    kk = kk / jnp.maximum(jnp.sqrt(headsum(kk * kk)), 1e-12)
    kh = k * (1.0 + (asig - 1.0) * k_a)
    a_vec = -kk
    b_vec = kk * asig
    yield

    incl, strict, incl_m, same_m = _chunk_mats(chunk)
    cums = _dot_exact_lhs(jnp.concatenate([incl_m, same_m], axis=0), logw)
    cum, cum_c = cums[:TB], cums[TB:]
    e_in = jnp.exp(cum)
    e_out = jnp.exp(-cum)
    e_end = jnp.exp(cum_c - cum)
    yield
    rt = r * e_in
    at = a_vec * jnp.exp(cum - logw)
    bt = b_vec * e_out
    kt = kh * e_out
    bp = b_vec * e_end
    kp = kh * e_end
    e_c = jnp.exp(cum_c)
    yield

    def msk(t, h):
        return jnp.where(lo_half if h == 0 else hi_half, t, 0.0)

    eye = jnp.where(_iota2((TB, TB), 0) == _iota2((TB, TB), 1), 1.0, 0.0)
    pairs = range(npair)
    heads = [(p, h) for p in pairs for h in range(2)]
    at_t, rt_t, bt_t, kt_t, v_t, bp_t, kp_t, ec_t = ([_tile(t, p) for p in pairs]
                                                     for t in (at, rt, bt, kt, v, bp, kp, e_c))
    bk = [_split(jnp.concatenate([msk(bt_t[p], 0), msk(bt_t[p], 1), msk(kt_t[p], 0), msk(kt_t[p], 1)], axis=0))
          for p in pairs]
    sc_a = [_dot3(at_t[p], bk[p], NT) for p in pairs]
    sc_r = [_dot1(rt_t[p], bk[p], NT) for p in pairs]
    a_ab = [jnp.where(strict, sc_a[p][:, h * TB:(h + 1) * TB], 0.0) for p, h in heads]
    a_ak = [jnp.where(strict, sc_a[p][:, (2 + h) * TB:(3 + h) * TB], 0.0) for p, h in heads]
    p_rb = [jnp.where(incl, sc_r[p][:, h * TB:(h + 1) * TB], 0.0) for p, h in heads]
    p_rk = [jnp.where(incl, sc_r[p][:, (2 + h) * TB:(3 + h) * TB], 0.0) for p, h in heads]
    yield
    base = min(chunk, INV_BLOCK)
    assert chunk in (base, 2 * base)
    in_base = _same_block(TB, base)
    n_in = [jnp.where(in_base, n, 0.0) for n in a_ab]
    t_inv = [eye + n for n in n_in]
    levels = int(math.log2(base))
    powers = [_dot3(n, n) for n in n_in] if levels > 1 else n_in
    yield
    for level in range(1, levels):
        if level + 1 < levels:
            both = [_dot3(x, jnp.concatenate([x, t], axis=1)) for x, t in zip(powers, t_inv)]
            powers = [b[:, :TB] for b in both]
            t_inv = [t + b[:, TB:] for t, b in zip(t_inv, both)]
        else:
            t_inv = [t + _dot3(x, t) for x, t in zip(powers, t_inv)]
        yield
    if chunk > base:
        n_off = [_dot3(n - ni, t) for n, ni, t in zip(a_ab, n_in, t_inv)]
        yield
        t_inv = [t + _dot3(t, no) for t, no in zip(t_inv, n_off)]
    yield
    av = [_dot3(jnp.concatenate(a_ak[2 * p:2 * p + 2], axis=1),
                jnp.concatenate([msk(v_t[p], 0), msk(v_t[p], 1)], axis=0)) for p in pairs]
    tay = [_dot3(jnp.concatenate(t_inv[2 * p:2 * p + 2], axis=1),
                 jnp.concatenate([jnp.concatenate([msk(at_t[p], h), msk(av[p], h)], axis=1) for h in range(2)], axis=0))
           for p in pairs]
    yield
    ta = [t[:, :LANES] for t in tay]
    yy = [t[:, LANES:] for t in tay]
    qz =[_dot1(jnp.concatenate(p_rb[2 * p:2 * p + 2], axis=1),
                jnp.concatenate([jnp.concatenate([msk(ta[p], h), msk(yy[p], h)], axis=1) for h in range(2)], axis=0))
          for p in pairs]
    zv = [_dot1(jnp.concatenate(p_rk[2 * p:2 * p + 2], axis=1),
                jnp.concatenate([msk(v_t[p], 0), msk(v_t[p], 1)], axis=0)) for p in pairs]
    qq = [rt_t[p] + qz[p][:, :LANES] for p in pairs]
    zz = [qz[p][:, LANES:] + zv[p] for p in pairs]

    bd = _same_block(LANES, HEAD)
    diag = _iota2((LANES, LANES), 0) == _iota2((LANES, LANES), 1)
    rows = lambda t, c: t[c * chunk:(c + 1) * chunk]
    m_mat = [[jnp.where(diag, ec_t[p][c * chunk:c * chunk + 1, :], 0.0)
              + jnp.where(bd, _dot3(rows(ta[p], c), rows(bp_t[p], c), TN), 0.0) for p in pairs]
             for c in range(nchunk)]
    yield
    n_mat = [[jnp.where(bd, _dot3(jnp.concatenate([rows(yy[p], c), rows(v_t[p], c)], axis=0),
                                  jnp.concatenate([rows(bp_t[p], c), rows(kp_t[p], c)], axis=0), TN), 0.0)
              for p in pairs] for c in range(nchunk)]
    yield
    state = [s_scr[p] for p in pairs] if chained else None
    o_rows = [[] for _ in pairs]
    for c in range(nchunk):
        for p in pairs:
            s = _split(state[p] if chained else _wkv_pair_load(s0_ref, (c,), p))
            o_rows[p].append(_dot1(rows(qq[p], c), s, NT) + rows(zz[p], c))
            s = _dot3(s, m_mat[c][p]) + n_mat[c][p]
            if chained:
                state[p] = s
            else:
                _wkv_pair_store(sout_ref, (c,), p, s)
        if chained:
            yield
    if chained:
        for p in pairs:
            s_scr[p] = state[p]

        @pl.when(pl.program_id(1) == pl.num_programs(1) - 1)
        def _():
            for p in pairs:
                _wkv_pair_store(sout_ref, (), p, state[p])
    o = jnp.concatenate([jnp.concatenate(o_rows[p], axis=0) for p in pairs], axis=1)
    yield

    inv_n = 1.0 / HEAD
    mean = headsum(o) * inv_n
    d = o - mean
    var = headsum(d * d) * inv_n
    o = d * lax.rsqrt(var + 1e-5 * HEAD) * ln_w + ln_b
    o = o + headsum(r * kh * r_k) * v
    z_ref[...] = o * g


def _gla_pair_load(ref, idx, p):
    return jnp.concatenate([ref[idx + (2 * p,)], ref[idx + (2 * p + 1,)]], axis=0).T


def _gla_pair_store(ref, idx, p, s):
    st = s.T
    dk = st.shape[0] // 2
    ref[idx + (2 * p,)] = st[:dk]
    ref[idx + (2 * p + 1,)] = st[dk:]


def _gla_body(chunk, chained, qkv_ref, lga_ref, og_ref, s0_ref, wg2_ref, bg_ref, nw_ref,
              z_ref, sout_ref, s_scr):
    nchunk = TB // chunk
    vw = z_ref.shape[-1]
    nhead = vw // LANES
    npair = nhead // 2
    kw = npair * LANES
    qkv = qkv_ref[...]
    q = qkv[:, 0:kw] * (HEAD ** -0.5)
    k = qkv[:, kw:2 * kw]
    v = qkv[:, 2 * kw:2 * kw + vw]
    gl = _dot(lga_ref[...].astype(BF16), wg2_ref[...]) + bg_ref[...]
    log_a = -_softplus(-gl) * (1.0 / GLA_GATE_TEMP)

    incl, _, incl_m, same_m = _chunk_mats(chunk)
    cums = _dot_exact_lhs(jnp.concatenate([incl_m, same_m], axis=0), log_a)
    cum, cum_c = cums[:TB], cums[TB:]
    qt = q * jnp.exp(cum)
    kt = k * jnp.exp(-cum)
    kp = k * jnp.exp(cum_c - cum)
    e_c = jnp.exp(cum_c)
    yield
    lo_half, hi_half = _half_masks()

    def msk(t, h):
        return jnp.where(lo_half if h == 0 else hi_half, t, 0.0)

    if chained:
        @pl.when(pl.program_id(1) == 0)
        def _():
            for p in range(npair):
                s_scr[p] = _gla_pair_load(s0_ref, (), p)

    pairs = range(npair)
    heads = [(p, h) for p in pairs for h in range(2)]
    rows = lambda t, c: t[c * chunk:(c + 1) * chunk]
    kt_t, kp_t, ec_t = ([_tile(t, p) for p in pairs] for t in (kt, kp, e_c))
    v_t = [_tile(v, h) for h in range(nhead)]
    qm = [msk(_tile(qt, p), h) for p, h in heads]
    sc = [_dot1(jnp.concatenate(qm[2 * p:2 * p + 2], axis=0), kt_t[p], NT) for p in pairs]
    intra = [_dot1(jnp.where(incl, sc[p][h * TB:(h + 1) * TB], 0.0), v_t[2 * p + h]) for p, h in heads]
    upd = [[jnp.where(lo_half, _dot3(rows(v_t[2 * p], c), rows(kp_t[p], c), TN),
                      _dot3(rows(v_t[2 * p + 1], c), rows(kp_t[p], c), TN)) for p in pairs]
           for c in range(nchunk)]
    yield
    state = [s_scr[p] for p in pairs] if chained else None
    inter = [[] for _ in range(nhead)]
    for c in range(nchunk):
        for p in pairs:
            s = state[p] if chained else _gla_pair_load(s0_ref, (c,), p)
            io = _dot1(jnp.concatenate([rows(qm[2 * p], c), rows(qm[2 * p + 1], c)], axis=0), s, NT)
            inter[2 * p].append(io[:chunk])
            inter[2 * p + 1].append(io[chunk:])
            s = s * ec_t[p][c * chunk:c * chunk + 1, :] + upd[c][p]
            if chained:
                state[p] = s
            else:
                _gla_pair_store(sout_ref, (c,), p, s)
    if chained:
        for p in pairs:
            s_scr[p] = state[p]

        @pl.when(pl.program_id(1) == pl.num_programs(1) - 1)
        def _():
            for p in pairs:
                _gla_pair_store(sout_ref, (), p, state[p])
    o_heads = [intra[h] + jnp.concatenate(inter[h], axis=0) for h in range(nhead)]
    yield

    og = og_ref[...]
    for h in range(nhead):
        o = o_heads[h]
        o = o * lax.rsqrt(jnp.mean(o * o, axis=-1, keepdims=True) + NORM_EPS) * nw_ref[...]
        z_ref[:, h * LANES:(h + 1) * LANES] = o * jax.nn.silu(_tile(og, h))


def _mixers_body(chunk, chained, x_ref, prev_ref, wkv0_ref, qkv_ref, lga_ref, og_ref, gla0_ref,
                 mu_ref, vec_ref, wlo_ref, g2_ref, wg2_ref, bg_ref, nw_ref,
                 za_ref, wkv_ref, zb_ref, gla_ref, carry_ref, wkv_scr, gla_scr):
    rwkv = _rwkv_body(chunk, chained, x_ref, prev_ref, wkv0_ref, mu_ref, vec_ref, wlo_ref, g2_ref, za_ref, wkv_ref,
                      carry_ref, wkv_scr)
    gla = _gla_body(chunk, chained, qkv_ref, lga_ref, og_ref, gla0_ref, wg2_ref, bg_ref, nw_ref, zb_ref, gla_ref,
                    gla_scr)
    for _ in itertools.zip_longest(rwkv, gla):
        pass


def _mixers(x_rw, prev, wkv0, qkv, lga, og, gla0, w, seq_len):
    width = wkv0.shape[1] * wkv0.shape[2]
    vw = og.shape[-1]
    cols = x_rw.shape[-1]
    chained = seq_len > TB
    if chained:
        chunk = PROMPT_CHUNK
        b, t, _ = x_rw.shape
        grid = (b, t // TB)
        rows = lambda n: pl.BlockSpec((None, TB, n), lambda i, j: (i, j, 0))
        prev_spec = pl.BlockSpec((None, 1, cols), lambda i, j: (i, 0, 0))
        st = lambda a: pl.BlockSpec((None,) + a.shape[1:], lambda i, j: (i, 0, 0, 0))
        lead = (b, t)
    else:
        chunk = seq_len
        m = x_rw.shape[1]
        assert m % TB == 0
        grid = (m // TB, 1)
        rows = lambda n: pl.BlockSpec((None, TB, n), lambda i, j: (0, i, 0))
        prev_spec = pl.BlockSpec((TB, cols), lambda i, j: (i, 0))
        st = lambda a: pl.BlockSpec((TB // chunk,) + a.shape[1:], lambda i, j: (i, 0, 0, 0))
        lead = (1, m)
    const = lambda a: pl.BlockSpec(a.shape, lambda i, j: (0,) * a.ndim)
    consts = [w["mu_shift"], w["rwkv_vec"], w["rwkv_wlo"], w["rwkv_g2"], w["gla_wg2"], w["gla_bg"], w["gla_norm_w"]]
    return pl.pallas_call(
        functools.partial(_mixers_body, chunk, chained),
        grid=grid,
        in_specs=[rows(cols), prev_spec, st(wkv0), rows(qkv.shape[-1]), rows(lga.shape[-1]), rows(vw), st(gla0)]
        + [const(a) for a in consts],
        out_specs=[rows(width), st(wkv0), rows(vw), st(gla0)],
        out_shape=[jax.ShapeDtypeStruct(lead + (width,), F32), jax.ShapeDtypeStruct(wkv0.shape, F32),
                   jax.ShapeDtypeStruct(lead + (vw,), F32), jax.ShapeDtypeStruct(gla0.shape, F32)],
        scratch_shapes=[pltpu.VMEM((1, cols), F32), pltpu.VMEM((wkv0.shape[1] // 2, LANES, LANES), F32),
                        pltpu.VMEM((gla0.shape[1] // 2, LANES, LANES), F32)],
        compiler_params=pltpu.CompilerParams(dimension_semantics=("parallel", "arbitrary"),
                                             vmem_limit_bytes=VMEM_LIMIT),
        name="mixers",
    )(x_rw, prev, wkv0, qkv, lga, og, gla0, *consts)


def _post_body(chained, seq_len, x_ref, za_ref, zb_ref, gate_ref, aux_ref, woa_ref, wob_ref, wo_ref, g_ref,
               wup_ref, cw_ref, cb_ref, wd_ref, nf_ref, y_ref, tail_ref, carry_ref, act_ref):
    tm, d = x_ref.shape
    hidden = wd_ref.shape[0]
    ya = _dot(za_ref[...].astype(BF16), woa_ref[...])
    yb = _dot(zb_ref[...].astype(BF16), wob_ref[...])
    gate = gate_ref[...]
    merged = jax.nn.sigmoid(gate[:, :d]) * ya + jax.nn.sigmoid(gate[:, d:]) * yb
    x1 = x_ref[...] + _dot(merged.astype(BF16), wo_ref[...])
    hb = _rmsnorm(x1, g_ref[...]).astype(BF16)

    if chained:
        @pl.when(pl.program_id(1) == 0)
        def _():
            carry_ref[0:6, :] = jnp.zeros((6, carry_ref.shape[1]), F32)
            carry_ref[6:8, :] = aux_ref[...]
    else:
        t = _iota2((tm, 1), 0) & (seq_len - 1)

    def conv(u, cs):
        if chained:
            ext = jnp.concatenate([carry_ref[:, cs], u], axis=0)
            carry_ref[:, cs] = u[tm - 8:, :]
            tail_ref[:, cs] = u[tm - 8:, :]
            inner = cw_ref[1:2, cs] * ext + pltpu.roll(cw_ref[0:1, cs] * ext, 1, axis=0)
            return (cb_ref[:, cs] + cw_ref[2:3, cs] * ext + pltpu.roll(inner, 1, axis=0))[8:]
        else:
            pa = aux_ref[:, cs]
            p1 = jnp.where(t == 0, pltpu.roll(pa, tm - 1, axis=0), pltpu.roll(u, 1, axis=0))
            p2 = jnp.where(t < 2, pa, pltpu.roll(u, 2, axis=0))
            tail_ref[:, cs] = u
        return cb_ref[:, cs] + cw_ref[0:1, cs] * p2 + cw_ref[1:2, cs] * p1 + cw_ref[2:3, cs] * u

    def up(j0):
        cols = (slice(j0, j0 + FFN_CHUNK), slice(hidden + j0, hidden + j0 + FFN_CHUNK))
        return [(_dot(hb, wup_ref[:, cs]), cs) for cs in cols]

    starts = list(range(0, hidden, FFN_CHUNK))
    group_end = [starts[(len(starts) + 1) // 2 - 1], starts[-1]]
    out = x1
    nxt = up(starts[0])
    k0 = 0
    for i, j0 in enumerate(starts):
        cur = nxt
        if i + 1 < len(starts):
            nxt = up(starts[i + 1])
        val, gat = (conv(u, cs) for u, cs in cur)
        act_ref[:, j0:j0 + FFN_CHUNK] = (jax.nn.gelu(gat) * val).astype(BF16)
        if j0 in group_end:
            k1 = j0 + FFN_CHUNK
            out = out + _dot(act_ref[:, k0:k1], wd_ref[k0:k1, :])
            k0 = k1
    y_ref[...] = _rmsnorm(out, nf_ref[...])


def _post(x, za, zb, gate, conv_state, w, seq_len):
    b, t, d = x.shape
    m = b * t
    f2 = conv_state.shape[-1]
    chained = seq_len >= POST_TM
    assert seq_len % POST_TM == 0 if chained else POST_TM_SHORT % seq_len == 0
    weights = [w["w_out_a"], w["w_out_b"], w["w_o"], w["norm_ffn"], w["ffn_w_up"], w["ffn_conv_w"],
               w["ffn_conv_b"], w["ffn_w_down"], w["norm_final"]]
    const = lambda a: pl.BlockSpec(a.shape, lambda i, j: (0,) * a.ndim, pipeline_mode=pl.Buffered(1))
    if chained:
        tm = POST_TM
        grid = (b, t // tm)
        rows = lambda a: pl.BlockSpec((None, tm, a.shape[-1]), lambda i, j: (i, j, 0))
        acts = [x, za.reshape(b, t, -1), zb.reshape(b, t, -1), gate.reshape(b, t, -1)]
        aux, aux_spec = conv_state, pl.BlockSpec((None, 2, f2), lambda i, j: (i, 0, 0))
        tail_spec = pl.BlockSpec((None, 8, f2), lambda i, j: (i, 0, 0))
        tail_shape = (b, 8, f2)
    else:
        tm = POST_TM_SHORT
        grid = (m // tm, 1)
        rows = lambda a: pl.BlockSpec((tm, a.shape[-1]), lambda i, j: (i, 0))
        acts = [x.reshape(m, d), za, zb, gate]
        aux = jnp.pad(conv_state, ((0, 0), (0, seq_len - 2), (0, 0))).reshape(m, f2)
        aux_spec = tail_spec = pl.BlockSpec((tm, f2), lambda i, j: (i, 0))
        tail_shape = (m, f2)
    y, tail = pl.pallas_call(
        functools.partial(_post_body, chained, seq_len),
        grid=grid,
        in_specs=[rows(a) for a in acts] + [aux_spec] + [const(a) for a in weights],
        out_specs=[rows(acts[0]), tail_spec],
        out_shape=[jax.ShapeDtypeStruct(acts[0].shape, F32), jax.ShapeDtypeStruct(tail_shape, F32)],
        scratch_shapes=[pltpu.VMEM((8, f2), F32), pltpu.VMEM((tm, f2 // 2), BF16)],
        compiler_params=pltpu.CompilerParams(dimension_semantics=("parallel", "arbitrary"),
                                             vmem_limit_bytes=VMEM_LIMIT),
        name="post_mixer",
    )(*acts, aux, *weights)
    new_conv = tail[:, -2:] if chained else tail.reshape(b, t, f2)[:, -2:]
    return y.reshape(b, t, d), new_conv


def _trunk(x, shift0, wkv0, gla0, conv0, w):
    b, t, d = x.shape
    m = b * t
    xf = x.reshape(m, d)
    p_rw, p_qkv, p_og, p_gate, p_lga = _norm_proj(xf, w["norm_mix"], w["w_in_parts"])
    shift_cols = p_rw.shape[1]
    new_shift = p_rw.reshape(b, t, shift_cols)[:, -1]

    if t > TB:
        shape3 = lambda a: a.reshape(b, t, a.shape[1])
        prev = shift0[:, None, :]
    else:
        shape3 = lambda a: a[None]
        prev = jnp.pad(shift0[:, None, :], ((0, 0), (0, t - 1), (0, 0))).reshape(m, shift_cols)
    z_a, wkv_new, z_b, gla_new = _mixers(shape3(p_rw), prev, wkv0, shape3(p_qkv), shape3(p_lga), shape3(p_og),
                                          gla0, w, t)
    z_a = z_a.reshape(m, z_a.shape[-1])
    z_b = z_b.reshape(m, z_b.shape[-1])

    y, new_conv = _post(x, z_a, z_b, p_gate, conv0, w, t)
    return (y, new_shift[None], wkv_new[None], gla_new[None], new_conv[None])


def kernel(x_prompt, x_sample, state_rwkv_shift, state_rwkv_wkv, state_gla, state_ffn_conv, norm_mix, w_in, mu_shift, rwkv_w0, rwkv_w2, rwkv_a0, rwkv_a2, rwkv_g2, rwkv_k_k, rwkv_k_a, rwkv_r_k, rwkv_ln_w, rwkv_ln_b, gla_wg2, gla_bg, gla_norm_w, w_out_a, w_out_b, w_o, norm_ffn, ffn_w_up, ffn_conv_w, ffn_conv_b, ffn_w_down, norm_final):
    assert norm_mix.shape[0] == 1, "single-layer step"
    d = x_prompt.shape[-1]
    width = rwkv_w0.shape[-1]
    shift_cols = mu_shift.shape[-1]
    kw = gla_wg2.shape[-1]
    vw = w_out_b.shape[1]
    lora_g = gla_wg2.shape[1]
    lw, la = rwkv_w2.shape[1], rwkv_a2.shape[1]
    assert lw == HEAD and la == HEAD and rwkv_g2.shape[1] == LANES

    win = w_in[0].astype(BF16)
    c0 = shift_cols
    c1 = c0 + 2 * kw + vw
    c2 = c1 + lora_g
    c3 = c2 + vw
    w_lga = jnp.pad(win[:, c1:c2], ((0, 0), (0, LANES - lora_g)))
    w_in_parts = [win[:, :c0], win[:, c0:c1], win[:, c2:c3], win[:, c3:], w_lga]

    zw = jnp.zeros((lw, width), BF16)
    wlo = jnp.concatenate([jnp.concatenate([rwkv_w2[0].astype(BF16), zw], axis=1),
                           jnp.concatenate([zw, rwkv_a2[0].astype(BF16)], axis=1)], axis=0)
    vec = jnp.stack([rwkv_w0[0], rwkv_a0[0], rwkv_k_k[0], rwkv_k_a[0], rwkv_r_k[0].reshape(width),
                     rwkv_ln_w[0], rwkv_ln_b[0], jnp.zeros((width,), F32)])
    w = dict(
        norm_mix=norm_mix, w_in_parts=w_in_parts, mu_shift=mu_shift, rwkv_vec=vec, rwkv_wlo=wlo,
        rwkv_g2=rwkv_g2[0].astype(BF16),
        gla_wg2=jnp.pad(gla_wg2[0].astype(BF16), ((0, LANES - lora_g), (0, 0))), gla_bg=gla_bg,
        gla_norm_w=gla_norm_w,
        w_out_a=w_out_a[0].astype(BF16), w_out_b=w_out_b[0].astype(BF16), w_o=w_o[0].astype(BF16),
        norm_ffn=norm_ffn, ffn_w_up=ffn_w_up[0].astype(BF16), ffn_conv_w=ffn_conv_w[0],
        ffn_conv_b=ffn_conv_b, ffn_w_down=ffn_w_down[0].astype(BF16), norm_final=norm_final[None],
    )

    bp = x_prompt.shape[0]
    dt = x_prompt.dtype
    zeros = lambda s: jnp.zeros((bp,) + s.shape[2:], dt)
    y_p, shift_p, wkv_p, gla_p, conv_p = _trunk(
        x_prompt, zeros(state_rwkv_shift), zeros(state_rwkv_wkv), zeros(state_gla), zeros(state_ffn_conv), w)
    y_s, shift_s, wkv_s, gla_s, conv_s = _trunk(
        x_sample, state_rwkv_shift[0], state_rwkv_wkv[0], state_gla[0], state_ffn_conv[0], w)
    return (y_p, y_s, shift_p, wkv_p, gla_p, conv_p, shift_s, wkv_s, gla_s, conv_s)
```

```python
import functools
import itertools
import math

import jax
import jax.numpy as jnp
from jax import lax
from jax.experimental import pallas as pl
from jax.experimental.pallas import tpu as pltpu

F32 = jnp.float32
BF16 = jnp.bfloat16
LANES = 128
BF16_ROWS = 16
MXU_DEPTH = 256
TB = 128
TM = 512
POST_TM = 512
POST_TM_SHORT = 128
NCHUNK = 512
FFN_CHUNK = 256
VMEM_LIMIT = 56 * 1024 * 1024
NORM_EPS = 1e-6
HEAD = 64
GLA_GATE_TEMP = 16.0
PROMPT_CHUNK = 32
INV_BLOCK = 16
NN = ((1,), (0,))
NT = ((1,), (1,))
TN = ((0,), (0,))


def _dot(a, b, dims=NN):
    return lax.dot_general(a, b, (dims, ((), ())), preferred_element_type=F32)


def _split(x):
    hi = x.astype(BF16)
    lo = (x - hi.astype(F32)).astype(BF16)
    return hi, lo


def _dot3(a, b, dims=NN):
    ka, kb = dims[0][0], dims[1][0]
    if dims == TN and not isinstance(a, tuple) and not isinstance(b, tuple) and a.shape[0] % BF16_ROWS:
        pad = BF16_ROWS - a.shape[0] % BF16_ROWS
        a = jnp.concatenate([a, jnp.zeros((pad, a.shape[1]), F32)], axis=0)
        b = jnp.concatenate([b, jnp.zeros((pad, b.shape[1]), F32)], axis=0)
    ah, al = a if isinstance(a, tuple) else _split(a)
    bh, bl = b if isinstance(b, tuple) else _split(b)
    k = ah.shape[ka]
    if dims == NN and k == MXU_DEPTH and 2 * bh.shape[1] <= MXU_DEPTH:
        n = bh.shape[1]
        wide = _dot(ah, jnp.concatenate([bh, bl], axis=1))
        return wide[:, :n] + (wide[:, n:] + _dot(al, bh))
    if k % BF16_ROWS or 2 * k > MXU_DEPTH:
        return _dot(ah, bh, dims) + (_dot(ah, bl, dims) + _dot(al, bh, dims))
    if 3 * k <= MXU_DEPTH:
        return _dot(jnp.concatenate([ah, al, ah], axis=ka), jnp.concatenate([bh, bh, bl], axis=kb), dims)
    n = bh.shape[1]
    if dims == NN and 2 * n <= MXU_DEPTH:
        wide = _dot(jnp.concatenate([ah, al], axis=1),
                    jnp.concatenate([jnp.concatenate([bh, bl], axis=1),
                                     jnp.concatenate([bh, jnp.zeros_like(bl)], axis=1)], axis=0))
        return wide[:, :n] + wide[:, n:]
    return _dot(jnp.concatenate([ah, al], axis=ka), jnp.concatenate([bh, bh], axis=kb), dims) + _dot(ah, bl, dims)


def _dot1(a, b, dims=NN):
    ah = a[0] if isinstance(a, tuple) else a.astype(BF16)
    bh = b[0] if isinstance(b, tuple) else b.astype(BF16)
    return _dot(ah, bh, dims)


def _split3(x):
    a1 = x.astype(BF16)
    r1 = x - a1.astype(F32)
    a2 = r1.astype(BF16)
    a3 = (r1 - a2.astype(F32)).astype(BF16)
    return a1, a2, a3


def _dot_exact_lhs(e, x):
    x1, x2, x3 = _split3(x)
    return _dot(jnp.concatenate([e, e], axis=1), jnp.concatenate([x1, x2], axis=0)) + _dot(e, x3)


def _dot_exact_rhs(x, e):
    return _dot(jnp.concatenate(_split(x), axis=1), jnp.concatenate([e, e], axis=0))


def _iota2(shape, dim):
    return lax.broadcasted_iota(jnp.int32, shape, dim)


def _same_block(n, size):
    sh = int(math.log2(size))
    return (_iota2((n, n), 0) >> sh) == (_iota2((n, n), 1) >> sh)


def _ones_where(mask):
    return jnp.where(mask, 1.0, 0.0).astype(BF16)


def _softplus(y):
    return jnp.maximum(y, 0.0) + jnp.log(1.0 + jnp.exp(-jnp.abs(y)))


def _rmsnorm(x, g):
    return x * lax.rsqrt(jnp.mean(x * x, axis=-1, keepdims=True) + NORM_EPS) * g


def _tile(x, p):
    return x[:, p * LANES:(p + 1) * LANES]


def _chunk_masks(chunk):
    same = _same_block(TB, chunk)
    r = _iota2((TB, TB), 0)
    c = _iota2((TB, TB), 1)
    return same & (c <= r), same & (c < r)


def _chunk_cumsum(x, incl, chunk):
    cum = _dot_exact_lhs(_ones_where(incl), x)
    total = jnp.concatenate([jnp.broadcast_to(cum[c + chunk - 1:c + chunk, :], (chunk, x.shape[1]))
                             for c in range(0, TB, chunk)], axis=0)
    return cum, total


def _half_masks():
    lane = _iota2((1, LANES), 1)
    lo = lane < HEAD
    return lo, jnp.logical_not(lo)


def _norm_proj_body(nw, x_ref, g_ref, *refs):
    w_refs, o_refs = refs[:nw], refs[nw:]
    hb = _rmsnorm(x_ref[...], g_ref[...]).astype(BF16)
    for w_ref, o_ref in zip(w_refs, o_refs):
        n = w_ref.shape[1]
        for n0 in range(0, n, NCHUNK):
            n1 = min(n0 + NCHUNK, n)
            o_ref[:, n0:n1] = _dot(hb, w_ref[:, n0:n1])


def _norm_proj(x, g, weights):
    m, d = x.shape
    nw = len(weights)
    return pl.pallas_call(
        functools.partial(_norm_proj_body, nw),
        grid=(m // TM,),
        in_specs=[pl.BlockSpec((TM, d), lambda i: (i, 0)), pl.BlockSpec((1, d), lambda i: (0, 0))]
        + [pl.BlockSpec(w.shape, lambda i: (0, 0), pipeline_mode=pl.Buffered(1)) for w in weights],
        out_specs=[pl.BlockSpec((TM, w.shape[1]), lambda i: (i, 0)) for w in weights],
        out_shape=[jax.ShapeDtypeStruct((m, w.shape[1]), F32) for w in weights],
        compiler_params=pltpu.CompilerParams(dimension_semantics=("parallel",), vmem_limit_bytes=VMEM_LIMIT),
        name="norm_proj",
    )(x, g, *weights)


def _wkv_pair_load(ref, idx, p):
    a, b = ref[idx + (2 * p,)], ref[idx + (2 * p + 1,)]
    z = jnp.zeros_like(a)
    return jnp.concatenate([jnp.concatenate([a, z], axis=1), jnp.concatenate([z, b], axis=1)], axis=0)


def _wkv_pair_store(ref, idx, p, s):
    n = s.shape[0] // 2
    ref[idx + (2 * p,)] = s[:n, :n]
    ref[idx + (2 * p + 1,)] = s[n:, n:]


def _rwkv_body(chunk, chained, x_ref, prev_ref, s0_ref, mu_ref, vec_ref, wlo_ref, g2_ref,
               z_ref, sout_ref, carry_ref, s_scr):
    nchunk = TB // chunk
    width = z_ref.shape[-1]
    npair = width // LANES
    x = x_ref[...]
    row = _iota2((TB, 1), 0)
    rolled = pltpu.roll(x, 1, axis=0)
    if chained:
        @pl.when(pl.program_id(1) == 0)
        def _():
            carry_ref[...] = prev_ref[...]
            for p in range(npair):
                s_scr[p] = _wkv_pair_load(s0_ref, (), p)
        prev = jnp.where(row == 0, carry_ref[...], rolled)
        carry_ref[...] = x[TB - 1:TB, :]
    else:
        prev = jnp.where((row & (chunk - 1)) == 0, prev_ref[...], rolled)
    xs = x + (prev - x) * mu_ref[...]
    yield

    r = xs[:, 0:width]
    k = xs[:, width:2 * width]
    v = xs[:, 2 * width:3 * width]
    lora = xs[:, 3 * width:3 * width + LANES]
    lg = xs[:, 3 * width + LANES:3 * width + 2 * LANES]
    lo_half, hi_half = _half_masks()
    lora = jnp.where(lo_half, jnp.tanh(lora), lora)
    wa = _dot(lora.astype(BF16), wlo_ref[...])
    w0, a0, k_k, k_a = vec_ref[0:1, :], vec_ref[1:2, :], vec_ref[2:3, :], vec_ref[3:4, :]
    r_k, ln_w, ln_b = vec_ref[4:5, :], vec_ref[5:6, :], vec_ref[6:7, :]
    wlog = -_softplus(-(w0 + wa[:, :width])) - 0.5
    logw = -jnp.exp(wlog)
    asig = jax.nn.sigmoid(a0 + wa[:, width:])
    g = _dot(jax.nn.sigmoid(lg).astype(BF16), g2_ref[...])
    yield

    seg = _ones_where(_same_block(LANES, HEAD))

    def headsum(t):
        return jnp.concatenate([_dot_exact_rhs(_tile(t, p), seg) for p in range(npair)], axis=1)

    kk = k * k_k
    kk = kk / jnp.maximum(jnp.sqrt(headsum(kk * kk)), 1e-12)
    kh = k * (1.0 + (asig - 1.0) * k_a)
    a_vec = -kk
    b_vec = kk * asig
    yield

    incl, strict = _chunk_masks(chunk)
    cum, cum_c = _chunk_cumsum(logw, incl, chunk)
    e_in = jnp.exp(cum)
    e_out = jnp.exp(-cum)
    e_end = jnp.exp(cum_c - cum)
    yield
    rt = r * e_in
    at = a_vec * jnp.exp(cum - logw)
    bt = b_vec * e_out
    kt = kh * e_out
    bp = b_vec * e_end
    kp = kh * e_end
    e_c = jnp.exp(cum_c)
    yield

    def msk(t, h):
        keep = lo_half if h == 0 else hi_half
        if isinstance(t, tuple):
            return tuple(jnp.where(keep, x, jnp.zeros_like(x)) for x in t)
        return jnp.where(keep, t, 0.0)

    def cat(parts, axis):
        return tuple(jnp.concatenate([part[i] for part in parts], axis=axis) for i in range(2))

    by_head = lambda t: cat([msk(t, 0), msk(t, 1)], 0)

    eye = jnp.where(_iota2((TB, TB), 0) == _iota2((TB, TB), 1), 1.0, 0.0)
    pairs = range(npair)
    heads = [(p, h) for p in pairs for h in range(2)]
    at_t, rt_t, bt_t, kt_t, v_t, bp_t, kp_t, ec_t = ([_tile(t, p) for p in pairs]
                                                     for t in (at, rt, bt, kt, v, bp, kp, e_c))
    at_s = [_split(t) for t in at_t]
    v_h = [by_head(_split(t)) for t in v_t]
    bk = [cat([by_head(_split(bt_t[p])), by_head(_split(kt_t[p]))], 0) for p in pairs]
    sc_a = [_dot3(at_s[p], bk[p], NT) for p in pairs]
    sc_r = [_dot1(rt_t[p], bk[p], NT) for p in pairs]
    a_ab = [jnp.where(strict, sc_a[p][:, h * TB:(h + 1) * TB], 0.0) for p, h in heads]
    a_ak = [jnp.where(strict, sc_a[p][:, (2 + h) * TB:(3 + h) * TB], 0.0) for p, h in heads]
    p_rb = [jnp.where(incl, sc_r[p][:, h * TB:(h + 1) * TB], 0.0) for p, h in heads]
    p_rk = [jnp.where(incl, sc_r[p][:, (2 + h) * TB:(3 + h) * TB], 0.0) for p, h in heads]
    yield
    base = min(chunk, INV_BLOCK)
    assert chunk in (base, 2 * base)
    in_base = _same_block(TB, base)
    n_in = [jnp.where(in_base, n, 0.0) for n in a_ab]
    t_inv = [eye + n for n in n_in]
    levels = int(math.log2(base))
    splits = lambda mats: [_split(mat) for mat in mats]
    powers = [_dot3(n_s, n_s) for n_s in splits(n_in)] if levels > 1 else n_in
    yield
    for level in range(1, levels):
        x_s, t_s = splits(powers), splits(t_inv)
        if level + 1 < levels:
            both = [_dot3(xs, cat([xs, ts], 1)) for xs, ts in zip(x_s, t_s)]
            powers = [b[:, :TB] for b in both]
            t_inv = [t + b[:, TB:] for t, b in zip(t_inv, both)]
        else:
            t_inv = [t + _dot3(xs, ts) for t, xs, ts in zip(t_inv, x_s, t_s)]
        yield
    if chunk > base:
        t_s = splits(t_inv)
        n_off = [_dot3(n - ni, ts) for n, ni, ts in zip(a_ab, n_in, t_s)]
        yield
        t_inv = [t + _dot3(ts, no) for t, ts, no in zip(t_inv, t_s, n_off)]
    yield
    av = [_dot3(jnp.concatenate(a_ak[2 * p:2 * p + 2], axis=1), v_h[p]) for p in pairs]
    av_s = splits(av)
    tay = [_dot3(jnp.concatenate(t_inv[2 * p:2 * p + 2], axis=1),
                 cat([cat([msk(at_s[p], h), msk(av_s[p], h)], 1) for h in range(2)], 0))
           for p in pairs]
    yield
    ta = [t[:, :LANES] for t in tay]
    yy = [t[:, LANES:] for t in tay]
    qz = [_dot1(jnp.concatenate(p_rb[2 * p:2 * p + 2], axis=1),
                jnp.concatenate([jnp.concatenate([msk(ta[p], h), msk(yy[p], h)], axis=1) for h in range(2)], axis=0))
          for p in pairs]
    zv = [_dot1(jnp.concatenate(p_rk[2 * p:2 * p + 2], axis=1), v_h[p]) for p in pairs]
    qq = [rt_t[p] + qz[p][:, :LANES] for p in pairs]
    zz = [qz[p][:, LANES:] + zv[p] for p in pairs]

    bd = _same_block(LANES, HEAD)
    diag = _iota2((LANES, LANES), 0) == _iota2((LANES, LANES), 1)
    rows = lambda t, c: t[c * chunk:(c + 1) * chunk]
    m_mat = [[jnp.where(diag, ec_t[p][c * chunk:c * chunk + 1, :], 0.0)
              + jnp.where(bd, _dot3(rows(ta[p], c), rows(bp_t[p], c), TN), 0.0) for p in pairs]
             for c in range(nchunk)]
    yield
    n_mat = [[jnp.where(bd, _dot3(jnp.concatenate([rows(yy[p], c), rows(v_t[p], c)], axis=0),
                                  jnp.concatenate([rows(bp_t[p], c), rows(kp_t[p], c)], axis=0), TN), 0.0)
              for p in pairs] for c in range(nchunk)]
    yield
    state = [s_scr[p] for p in pairs] if chained else None
    o_rows = [[] for _ in pairs]
    for c in range(nchunk):
        for p in pairs:
            s = _split(state[p] if chained else _wkv_pair_load(s0_ref, (c,), p))
            o_rows[p].append(_dot1(rows(qq[p], c), s, NT) + rows(zz[p], c))
            s = _dot3(s, m_mat[c][p]) + n_mat[c][p]
            if chained:
                state[p] = s
            else:
                _wkv_pair_store(sout_ref, (c,), p, s)
        if chained:
            yield
    if chained:
        for p in pairs:
            s_scr[p] = state[p]

        @pl.when(pl.program_id(1) == pl.num_programs(1) - 1)
        def _():
            for p in pairs:
                _wkv_pair_store(sout_ref, (), p, state[p])
    o = jnp.concatenate([jnp.concatenate(o_rows[p], axis=0) for p in pairs], axis=1)
    yield

    inv_n = 1.0 / HEAD
    mean = headsum(o) * inv_n
    d = o - mean
    var = headsum(d * d) * inv_n
    o = d * lax.rsqrt(var + 1e-5 * HEAD) * ln_w + ln_b
    o = o + headsum(r * kh * r_k) * v
    z_ref[...] = o * g


def _gla_pair_load(ref, idx, p):
    return jnp.concatenate([ref[idx + (2 * p,)], ref[idx + (2 * p + 1,)]], axis=0).T


def _gla_pair_store(ref, idx, p, s):
    st = s.T
    dk = st.shape[0] // 2
    ref[idx + (2 * p,)] = st[:dk]
    ref[idx + (2 * p + 1,)] = st[dk:]


def _gla_body(chunk, chained, qkv_ref, lga_ref, og_ref, s0_ref, wg2_ref, bg_ref, nw_ref,
              z_ref, sout_ref, s_scr):
    nchunk = TB // chunk
    vw = z_ref.shape[-1]
    nhead = vw // LANES
    npair = nhead // 2
    kw = npair * LANES
    qkv = qkv_ref[...]
    q = qkv[:, 0:kw] * (HEAD ** -0.5)
    k = qkv[:, kw:2 * kw]
    v = qkv[:, 2 * kw:2 * kw + vw]
    gl = _dot(lga_ref[...].astype(BF16), wg2_ref[...]) + bg_ref[...]
    log_a = -_softplus(-gl) * (1.0 / GLA_GATE_TEMP)

    incl, _ = _chunk_masks(chunk)
    cum, cum_c = _chunk_cumsum(log_a, incl, chunk)
    qt = q * jnp.exp(cum)
    kt = k * jnp.exp(-cum)
    kp = k * jnp.exp(cum_c - cum)
    e_c = jnp.exp(cum_c)
    yield
    lo_half, hi_half = _half_masks()

    def msk(t, h):
        return jnp.where(lo_half if h == 0 else hi_half, t, 0.0)

    if chained:
        @pl.when(pl.program_id(1) == 0)
        def _():
            for p in range(npair):
                s_scr[p] = _gla_pair_load(s0_ref, (), p)

    pairs = range(npair)
    heads = [(p, h) for p in pairs for h in range(2)]
    rows = lambda t, c: t[c * chunk:(c + 1) * chunk]
    kt_t, kp_t, ec_t = ([_tile(t, p) for p in pairs] for t in (kt, kp, e_c))
    v_t = [_tile(v, h) for h in range(nhead)]
    qm = [msk(_tile(qt, p), h) for p, h in heads]
    sc = [_dot1(jnp.concatenate(qm[2 * p:2 * p + 2], axis=0), kt_t[p], NT) for p in pairs]
    intra = [_dot1(jnp.where(incl, sc[p][h * TB:(h + 1) * TB], 0.0), v_t[2 * p + h]) for p, h in heads]
    upd = [[jnp.where(lo_half, _dot3(rows(v_t[2 * p], c), rows(kp_t[p], c), TN),
                      _dot3(rows(v_t[2 * p + 1], c), rows(kp_t[p], c), TN)) for p in pairs]
           for c in range(nchunk)]
    yield
    state = [s_scr[p] for p in pairs] if chained else None
    inter = [[] for _ in range(nhead)]
    for c in range(nchunk):
        for p in pairs:
            s = state[p] if chained else _gla_pair_load(s0_ref, (c,), p)
            io = _dot1(jnp.concatenate([rows(qm[2 * p], c), rows(qm[2 * p + 1], c)], axis=0), s, NT)
            inter[2 * p].append(io[:chunk])
            inter[2 * p + 1].append(io[chunk:])
            s = s * ec_t[p][c * chunk:c * chunk + 1, :] + upd[c][p]
            if chained:
                state[p] = s
            else:
                _gla_pair_store(sout_ref, (c,), p, s)
    if chained:
        for p in pairs:
            s_scr[p] = state[p]

        @pl.when(pl.program_id(1) == pl.num_programs(1) - 1)
        def _():
            for p in pairs:
                _gla_pair_store(sout_ref, (), p, state[p])
    o_heads = [intra[h] + jnp.concatenate(inter[h], axis=0) for h in range(nhead)]
    yield

    og = og_ref[...]
    for h in range(nhead):
        o = o_heads[h]
        o = o * lax.rsqrt(jnp.mean(o * o, axis=-1, keepdims=True) + NORM_EPS) * nw_ref[...]
        z_ref[:, h * LANES:(h + 1) * LANES] = o * jax.nn.silu(_tile(og, h))


def _mixers_body(chunk, chained, x_ref, prev_ref, wkv0_ref, qkv_ref, lga_ref, og_ref, gla0_ref,
                 mu_ref, vec_ref, wlo_ref, g2_ref, wg2_ref, bg_ref, nw_ref,
                 za_ref, wkv_ref, zb_ref, gla_ref, carry_ref, wkv_scr, gla_scr):
    rwkv = _rwkv_body(chunk, chained, x_ref, prev_ref, wkv0_ref, mu_ref, vec_ref, wlo_ref, g2_ref, za_ref, wkv_ref,
                      carry_ref, wkv_scr)
    gla = _gla_body(chunk, chained, qkv_ref, lga_ref, og_ref, gla0_ref, wg2_ref, bg_ref, nw_ref, zb_ref, gla_ref,
                    gla_scr)
    for _ in itertools.zip_longest(rwkv, gla):
        pass


def _mixers(x_rw, prev, wkv0, qkv, lga, og, gla0, w, seq_len):
    width = wkv0.shape[1] * wkv0.shape[2]
    vw = og.shape[-1]
    cols = x_rw.shape[-1]
    chained = seq_len > TB
    if chained:
        chunk = PROMPT_CHUNK
        b, t, _ = x_rw.shape
        grid = (b, t // TB)
        rows = lambda n: pl.BlockSpec((None, TB, n), lambda i, j: (i, j, 0))
        prev_spec = pl.BlockSpec((None, 1, cols), lambda i, j: (i, 0, 0))
        st = lambda a: pl.BlockSpec((None,) + a.shape[1:], lambda i, j: (i, 0, 0, 0))
        lead = (b, t)
    else:
        chunk = seq_len
        m = x_rw.shape[1]
        assert m % TB == 0
        grid = (m // TB, 1)
        rows = lambda n: pl.BlockSpec((None, TB, n), lambda i, j: (0, i, 0))
        prev_spec = pl.BlockSpec((TB, cols), lambda i, j: (i, 0))
        st = lambda a: pl.BlockSpec((TB // chunk,) + a.shape[1:], lambda i, j: (i, 0, 0, 0))
        lead = (1, m)
    const = lambda a: pl.BlockSpec(a.shape, lambda i, j: (0,) * a.ndim)
    consts = [w["mu_shift"], w["rwkv_vec"], w["rwkv_wlo"], w["rwkv_g2"], w["gla_wg2"], w["gla_bg"], w["gla_norm_w"]]
    return pl.pallas_call(
        functools.partial(_mixers_body, chunk, chained),
        grid=grid,
        in_specs=[rows(cols), prev_spec, st(wkv0), rows(qkv.shape[-1]), rows(lga.shape[-1]), rows(vw), st(gla0)]
        + [const(a) for a in consts],
        out_specs=[rows(width), st(wkv0), rows(vw), st(gla0)],
        out_shape=[jax.ShapeDtypeStruct(lead + (width,), F32), jax.ShapeDtypeStruct(wkv0.shape, F32),
                   jax.ShapeDtypeStruct(lead + (vw,), F32), jax.ShapeDtypeStruct(gla0.shape, F32)],
        scratch_shapes=[pltpu.VMEM((1, cols), F32), pltpu.VMEM((wkv0.shape[1] // 2, LANES, LANES), F32),
                        pltpu.VMEM((gla0.shape[1] // 2, LANES, LANES), F32)],
        compiler_params=pltpu.CompilerParams(dimension_semantics=("parallel", "arbitrary"),
                                             vmem_limit_bytes=VMEM_LIMIT),
        name="mixers",
    )(x_rw, prev, wkv0, qkv, lga, og, gla0, *consts)


def _post_body(chained, seq_len, x_ref, za_ref, zb_ref, gate_ref, aux_ref, woa_ref, wob_ref, wo_ref, g_ref,
               wup_ref, cw_ref, cb_ref, wd_ref, nf_ref, y_ref, tail_ref, carry_ref, act_ref):
    tm, d = x_ref.shape
    hidden = wd_ref.shape[0]
    ya = _dot(za_ref[...].astype(BF16), woa_ref[...])
    yb = _dot(zb_ref[...].astype(BF16), wob_ref[...])
    gate = gate_ref[...]
    merged = jax.nn.sigmoid(gate[:, :d]) * ya + jax.nn.sigmoid(gate[:, d:]) * yb
    x1 = x_ref[...] + _dot(merged.astype(BF16), wo_ref[...])
    hb = _rmsnorm(x1, g_ref[...]).astype(BF16)

    if chained:
        @pl.when(pl.program_id(1) == 0)
        def _():
            carry_ref[0:6, :] = jnp.zeros((6, carry_ref.shape[1]), F32)
            carry_ref[6:8, :] = aux_ref[...]
    else:
        t = _iota2((tm, 1), 0) & (seq_len - 1)

    def conv(u, cs):
        if chained:
            ext = jnp.concatenate([carry_ref[:, cs], u], axis=0)
            carry_ref[:, cs] = u[tm - 8:, :]
            tail_ref[:, cs] = u[tm - 8:, :]
            inner = cw_ref[1:2, cs] * ext + pltpu.roll(cw_ref[0:1, cs] * ext, 1, axis=0)
            return (cb_ref[:, cs] + cw_ref[2:3, cs] * ext + pltpu.roll(inner, 1, axis=0))[8:]
        else:
            pa = aux_ref[:, cs]
            p1 = jnp.where(t == 0, pltpu.roll(pa, tm - 1, axis=0), pltpu.roll(u, 1, axis=0))
            p2 = jnp.where(t < 2, pa, pltpu.roll(u, 2, axis=0))
            tail_ref[:, cs] = u
        return cb_ref[:, cs] + cw_ref[0:1, cs] * p2 + cw_ref[1:2, cs] * p1 + cw_ref[2:3, cs] * u

    def up(j0):
        cols = (slice(j0, j0 + FFN_CHUNK), slice(hidden + j0, hidden + j0 + FFN_CHUNK))
        return [(_dot(hb, wup_ref[:, cs]), cs) for cs in cols]

    starts = list(range(0, hidden, FFN_CHUNK))
    group_end = [starts[(len(starts) + 1) // 2 - 1], starts[-1]]
    out = x1
    nxt = up(starts[0])
    k0 = 0
    for i, j0 in enumerate(starts):
        cur = nxt
        if i + 1 < len(starts):
            nxt = up(starts[i + 1])
        val, gat = (conv(u, cs) for u, cs in cur)
        act_ref[:, j0:j0 + FFN_CHUNK] = (jax.nn.gelu(gat) * val).astype(BF16)
        if j0 in group_end:
            k1 = j0 + FFN_CHUNK
            out = out + _dot(act_ref[:, k0:k1], wd_ref[k0:k1, :])
            k0 = k1
    y_ref[...] = _rmsnorm(out, nf_ref[...])


def _post(x, za, zb, gate, conv_state, w, seq_len):
    b, t, d = x.shape
    m = b * t
    f2 = conv_state.shape[-1]
    chained = seq_len >= POST_TM
    assert seq_len % POST_TM == 0 if chained else POST_TM_SHORT % seq_len == 0
    weights = [w["w_out_a"], w["w_out_b"], w["w_o"], w["norm_ffn"], w["ffn_w_up"], w["ffn_conv_w"],
               w["ffn_conv_b"], w["ffn_w_down"], w["norm_final"]]
    const = lambda a: pl.BlockSpec(a.shape, lambda i, j: (0,) * a.ndim, pipeline_mode=pl.Buffered(1))
    if chained:
        tm = POST_TM
        grid = (b, t // tm)
        rows = lambda a: pl.BlockSpec((None, tm, a.shape[-1]), lambda i, j: (i, j, 0))
        acts = [x, za.reshape(b, t, -1), zb.reshape(b, t, -1), gate.reshape(b, t, -1)]
        aux, aux_spec = conv_state, pl.BlockSpec((None, 2, f2), lambda i, j: (i, 0, 0))
        tail_spec = pl.BlockSpec((None, 8, f2), lambda i, j: (i, 0, 0))
        tail_shape = (b, 8, f2)
    else:
        tm = POST_TM_SHORT
        grid = (m // tm, 1)
        rows = lambda a: pl.BlockSpec((tm, a.shape[-1]), lambda i, j: (i, 0))
        acts = [x.reshape(m, d), za, zb, gate]
        aux = jnp.pad(conv_state, ((0, 0), (0, seq_len - 2), (0, 0))).reshape(m, f2)
        aux_spec = tail_spec = pl.BlockSpec((tm, f2), lambda i, j: (i, 0))
        tail_shape = (m, f2)
    y, tail = pl.pallas_call(
        functools.partial(_post_body, chained, seq_len),
        grid=grid,
        in_specs=[rows(a) for a in acts] + [aux_spec] + [const(a) for a in weights],
        out_specs=[rows(acts[0]), tail_spec],
        out_shape=[jax.ShapeDtypeStruct(acts[0].shape, F32), jax.ShapeDtypeStruct(tail_shape, F32)],
        scratch_shapes=[pltpu.VMEM((8, f2), F32), pltpu.VMEM((tm, f2 // 2), BF16)],
        compiler_params=pltpu.CompilerParams(dimension_semantics=("parallel", "arbitrary"),
                                             vmem_limit_bytes=VMEM_LIMIT),
        name="post_mixer",
    )(*acts, aux, *weights)
    new_conv = tail[:, -2:] if chained else tail.reshape(b, t, f2)[:, -2:]
    return y.reshape(b, t, d), new_conv


def _trunk(x, shift0, wkv0, gla0, conv0, w):
    b, t, d = x.shape
    m = b * t
    xf = x.reshape(m, d)
    p_rw, p_qkv, p_og, p_gate, p_lga = _norm_proj(xf, w["norm_mix"], w["w_in_parts"])
    shift_cols = p_rw.shape[1]
    new_shift = p_rw.reshape(b, t, shift_cols)[:, -1]

    if t > TB:
        shape3 = lambda a: a.reshape(b, t, a.shape[1])
        prev = shift0[:, None, :]
    else:
        shape3 = lambda a: a[None]
        prev = jnp.pad(shift0[:, None, :], ((0, 0), (0, t - 1), (0, 0))).reshape(m, shift_cols)
    z_a, wkv_new, z_b, gla_new = _mixers(shape3(p_rw), prev, wkv0, shape3(p_qkv), shape3(p_lga), shape3(p_og),
                                          gla0, w, t)
    z_a = z_a.reshape(m, z_a.shape[-1])
    z_b = z_b.reshape(m, z_b.shape[-1])

    y, new_conv = _post(x, z_a, z_b, p_gate, conv0, w, t)
    return (y, new_shift[None], wkv_new[None], gla_new[None], new_conv[None])


def kernel(x_prompt, x_sample, state_rwkv_shift, state_rwkv_wkv, state_gla, state_ffn_conv, norm_mix, w_in, mu_shift, rwkv_w0, rwkv_w2, rwkv_a0, rwkv_a2, rwkv_g2, rwkv_k_k, rwkv_k_a, rwkv_r_k, rwkv_ln_w, rwkv_ln_b, gla_wg2, gla_bg, gla_norm_w, w_out_a, w_out_b, w_o, norm_ffn, ffn_w_up, ffn_conv_w, ffn_conv_b, ffn_w_down, norm_final):
    assert norm_mix.shape[0] == 1, "single-layer step"
    d = x_prompt.shape[-1]
    width = rwkv_w0.shape[-1]
    shift_cols = mu_shift.shape[-1]
    kw = gla_wg2.shape[-1]
    vw = w_out_b.shape[1]
    lora_g = gla_wg2.shape[1]
    lw, la = rwkv_w2.shape[1], rwkv_a2.shape[1]
    assert lw == HEAD and la == HEAD and rwkv_g2.shape[1] == LANES

    win = w_in[0].astype(BF16)
    c0 = shift_cols
    c1 = c0 + 2 * kw + vw
    c2 = c1 + lora_g
    c3 = c2 + vw
    w_lga = jnp.pad(win[:, c1:c2], ((0, 0), (0, LANES - lora_g)))
    w_in_parts = [win[:, :c0], win[:, c0:c1], win[:, c2:c3], win[:, c3:], w_lga]

    zw = jnp.zeros((lw, width), BF16)
    wlo = jnp.concatenate([jnp.concatenate([rwkv_w2[0].astype(BF16), zw], axis=1),
                           jnp.concatenate([zw, rwkv_a2[0].astype(BF16)], axis=1)], axis=0)
    vec = jnp.stack([rwkv_w0[0], rwkv_a0[0], rwkv_k_k[0], rwkv_k_a[0], rwkv_r_k[0].reshape(width),
                     rwkv_ln_w[0], rwkv_ln_b[0], jnp.zeros((width,), F32)])
    w = dict(
        norm_mix=norm_mix, w_in_parts=w_in_parts, mu_shift=mu_shift, rwkv_vec=vec, rwkv_wlo=wlo,
        rwkv_g2=rwkv_g2[0].astype(BF16),
        gla_wg2=jnp.pad(gla_wg2[0].astype(BF16), ((0, LANES - lora_g), (0, 0))), gla_bg=gla_bg,
        gla_norm_w=gla_norm_w,
        w_out_a=w_out_a[0].astype(BF16), w_out_b=w_out_b[0].astype(BF16), w_o=w_o[0].astype(BF16),
        norm_ffn=norm_ffn, ffn_w_up=ffn_w_up[0].astype(BF16), ffn_conv_w=ffn_conv_w[0],
        ffn_conv_b=ffn_conv_b, ffn_w_down=ffn_w_down[0].astype(BF16), norm_final=norm_final[None],
    )

    bp = x_prompt.shape[0]
    dt = x_prompt.dtype
    zeros = lambda s: jnp.zeros((bp,) + s.shape[2:], dt)
    y_p, shift_p, wkv_p, gla_p, conv_p = _trunk(
        x_prompt, zeros(state_rwkv_shift), zeros(state_rwkv_wkv), zeros(state_gla), zeros(state_ffn_conv), w)
    y_s, shift_s, wkv_s, gla_s, conv_s = _trunk(
        x_sample, state_rwkv_shift[0], state_rwkv_wkv[0], state_gla[0], state_ffn_conv[0], w)
    return (y_p, y_s, shift_p, wkv_p, gla_p, conv_p, shift_s, wkv_s, gla_s, conv_s)
```

```python
import functools
import itertools
import math

import jax
import jax.numpy as jnp
from jax import lax
from jax.experimental import pallas as pl
from jax.experimental.pallas import tpu as pltpu

F32 = jnp.float32
BF16 = jnp.bfloat16
LANES = 128
BF16_ROWS = 16
MXU_DEPTH = 256
TB = 128
TM = 512
POST_TM = 512
POST_TM_SHORT = 128
NCHUNK = 512
FFN_CHUNK = 256
VMEM_LIMIT = 56 * 1024 * 1024
NORM_EPS = 1e-6
HEAD = 64
GLA_GATE_TEMP = 16.0
PROMPT_CHUNK = 32
INV_BLOCK = 16
NN = ((1,), (0,))
NT = ((1,), (1,))
TN = ((0,), (0,))


def _dot(a, b, dims=NN):
    return lax.dot_general(a, b, (dims, ((), ())), preferred_element_type=F32)


def _split(x):
    hi = x.astype(BF16)
    lo = (x - hi.astype(F32)).astype(BF16)
    return hi, lo


def _dot3(a, b, dims=NN):
    ka, kb = dims[0][0], dims[1][0]
    if dims == TN and not isinstance(a, tuple) and not isinstance(b, tuple) and a.shape[0] % BF16_ROWS:
        pad = BF16_ROWS - a.shape[0] % BF16_ROWS
        a = jnp.concatenate([a, jnp.zeros((pad, a.shape[1]), F32)], axis=0)
        b = jnp.concatenate([b, jnp.zeros((pad, b.shape[1]), F32)], axis=0)
    ah, al = a if isinstance(a, tuple) else _split(a)
    bh, bl = b if isinstance(b, tuple) else _split(b)
    k = ah.shape[ka]
    if dims == NN and k == MXU_DEPTH and 2 * bh.shape[1] <= MXU_DEPTH:
        n = bh.shape[1]
        wide = _dot(ah, jnp.concatenate([bh, bl], axis=1))
        return wide[:, :n] + (wide[:, n:] + _dot(al, bh))
    if k % BF16_ROWS or 2 * k > MXU_DEPTH:
        return _dot(ah, bh, dims) + (_dot(ah, bl, dims) + _dot(al, bh, dims))
    if 3 * k <= MXU_DEPTH:
        return _dot(jnp.concatenate([ah, al, ah], axis=ka), jnp.concatenate([bh, bh, bl], axis=kb), dims)
    n = bh.shape[1]
    if dims == NN and 2 * n <= MXU_DEPTH:
        wide = _dot(jnp.concatenate([ah, al], axis=1),
                    jnp.concatenate([jnp.concatenate([bh, bl], axis=1),
                                     jnp.concatenate([bh, jnp.zeros_like(bl)], axis=1)], axis=0))
        return wide[:, :n] + wide[:, n:]
    return _dot(jnp.concatenate([ah, al], axis=ka), jnp.concatenate([bh, bh], axis=kb), dims) + _dot(ah, bl, dims)


def _dot1(a, b, dims=NN):
    ah = a[0] if isinstance(a, tuple) else a.astype(BF16)
    bh = b[0] if isinstance(b, tuple) else b.astype(BF16)
    return _dot(ah, bh, dims)


def _split3(x):
    a1 = x.astype(BF16)
    r1 = x - a1.astype(F32)
    a2 = r1.astype(BF16)
    a3 = (r1 - a2.astype(F32)).astype(BF16)
    return a1, a2, a3


def _dot_exact_lhs(e, x):
    x1, x2, x3 = _split3(x)
    return _dot(jnp.concatenate([e, e], axis=1), jnp.concatenate([x1, x2], axis=0)) + _dot(e, x3)


def _dot_exact_rhs(x, e):
    return _dot(jnp.concatenate(_split(x), axis=1), jnp.concatenate([e, e], axis=0))


def _iota2(shape, dim):
    return lax.broadcasted_iota(jnp.int32, shape, dim)


def _same_block(n, size):
    sh = int(math.log2(size))
    return (_iota2((n, n), 0) >> sh) == (_iota2((n, n), 1) >> sh)


def _ones_where(mask):
    return jnp.where(mask, 1.0, 0.0).astype(BF16)


def _softplus(y):
    return jnp.maximum(y, 0.0) + jnp.log(1.0 + jnp.exp(-jnp.abs(y)))


def _rmsnorm(x, g):
    return x * lax.rsqrt(jnp.mean(x * x, axis=-1, keepdims=True) + NORM_EPS) * g


def _tile(x, p):
    return x[:, p * LANES:(p + 1) * LANES]


def _chunk_masks(chunk):
    same = _same_block(TB, chunk)
    r = _iota2((TB, TB), 0)
    c = _iota2((TB, TB), 1)
    return same & (c <= r), same & (c < r)


def _chunk_cumsum(x, incl, chunk):
    cum = _dot_exact_lhs(_ones_where(incl), x)
    total = jnp.concatenate([jnp.broadcast_to(cum[c + chunk - 1:c + chunk, :], (chunk, x.shape[1]))
                             for c in range(0, TB, chunk)], axis=0)
    return cum, total


def _half_masks():
    lane = _iota2((1, LANES), 1)
    lo = lane < HEAD
    return lo, jnp.logical_not(lo)


def _norm_proj_body(nw, x_ref, g_ref, *refs):
    w_refs, o_refs = refs[:nw], refs[nw:]
    hb = _rmsnorm(x_ref[...], g_ref[...]).astype(BF16)
    for w_ref, o_ref in zip(w_refs, o_refs):
        n = w_ref.shape[1]
        for n0 in range(0, n, NCHUNK):
            n1 = min(n0 + NCHUNK, n)
            o_ref[:, n0:n1] = _dot(hb, w_ref[:, n0:n1])


def _norm_proj(x, g, weights):
    m, d = x.shape
    nw = len(weights)
    return pl.pallas_call(
        functools.partial(_norm_proj_body, nw),
        grid=(m // TM,),
        in_specs=[pl.BlockSpec((TM, d), lambda i: (i, 0)), pl.BlockSpec((1, d), lambda i: (0, 0))]
        + [pl.BlockSpec(w.shape, lambda i: (0, 0), pipeline_mode=pl.Buffered(1)) for w in weights],
        out_specs=[pl.BlockSpec((TM, w.shape[1]), lambda i: (i, 0)) for w in weights],
        out_shape=[jax.ShapeDtypeStruct((m, w.shape[1]), F32) for w in weights],
        compiler_params=pltpu.CompilerParams(dimension_semantics=("parallel",), vmem_limit_bytes=VMEM_LIMIT),
        name="norm_proj",
    )(x, g, *weights)


def _wkv_pair_load(ref, idx, p):
    a, b = ref[idx + (2 * p,)], ref[idx + (2 * p + 1,)]
    z = jnp.zeros_like(a)
    return jnp.concatenate([jnp.concatenate([a, z], axis=1), jnp.concatenate([z, b], axis=1)], axis=0)


def _wkv_pair_store(ref, idx, p, s):
    n = s.shape[0] // 2
    ref[idx + (2 * p,)] = s[:n, :n]
    ref[idx + (2 * p + 1,)] = s[n:, n:]


def _rwkv_body(chunk, chained, x_ref, prev_ref, s0_ref, mu_ref, vec_ref, wlo_ref, g2_ref,
               z_ref, sout_ref, carry_ref, s_scr):
    nchunk = TB // chunk
    width = z_ref.shape[-1]
    npair = width // LANES
    x = x_ref[...]
    row = _iota2((TB, 1), 0)
    rolled = pltpu.roll(x, 1, axis=0)
    if chained:
        @pl.when(pl.program_id(1) == 0)
        def _():
            carry_ref[...] = prev_ref[...]
            for p in range(npair):
                s_scr[p] = _wkv_pair_load(s0_ref, (), p)
        prev = jnp.where(row == 0, carry_ref[...], rolled)
        carry_ref[...] = x[TB - 1:TB, :]
    else:
        prev = jnp.where((row & (chunk - 1)) == 0, prev_ref[...], rolled)
    xs = x + (prev - x) * mu_ref[...]
    yield

    r = xs[:, 0:width]
    k = xs[:, width:2 * width]
    v = xs[:, 2 * width:3 * width]
    lora = xs[:, 3 * width:3 * width + LANES]
    lg = xs[:, 3 * width + LANES:3 * width + 2 * LANES]
    lo_half, hi_half = _half_masks()
    lora = jnp.where(lo_half, jnp.tanh(lora), lora)
    wa = _dot(lora.astype(BF16), wlo_ref[...])
    w0, a0, k_k, k_a = vec_ref[0:1, :], vec_ref[1:2, :], vec_ref[2:3, :], vec_ref[3:4, :]
    r_k, ln_w, ln_b = vec_ref[4:5, :], vec_ref[5:6, :], vec_ref[6:7, :]
    wlog = -_softplus(-(w0 + wa[:, :width])) - 0.5
    logw = -jnp.exp(wlog)
    asig = jax.nn.sigmoid(a0 + wa[:, width:])
    g = _dot(jax.nn.sigmoid(lg).astype(BF16), g2_ref[...])
    yield

    seg = _ones_where(_same_block(LANES, HEAD))

    def headsum(t):
        return jnp.concatenate([_dot_exact_rhs(_tile(t, p), seg) for p in range(npair)], axis=1)

    kk = k * k_k
    kk = kk / jnp.maximum(jnp.sqrt(headsum(kk * kk)), 1e-12)
    kh = k * (1.0 + (asig - 1.0) * k_a)
    a_vec = -kk
    b_vec = kk * asig
    yield

    incl, strict = _chunk_masks(chunk)
    cum, cum_c = _chunk_cumsum(logw, incl, chunk)
    e_in = jnp.exp(cum)
    e_out = jnp.exp(-cum)
    e_end = jnp.exp(cum_c - cum)
    yield
    rt = r * e_in
    at = a_vec * jnp.exp(cum - logw)
    bt = b_vec * e_out
    kt = kh * e_out
    bp = b_vec * e_end
    kp = kh * e_end
    e_c = jnp.exp(cum_c)
    yield

    def msk(t, h):
        keep = lo_half if h == 0 else hi_half
        if isinstance(t, tuple):
            return tuple(jnp.where(keep, x, jnp.zeros_like(x)) for x in t)
        return jnp.where(keep, t, 0.0)

    def cat(parts, axis):
        return tuple(jnp.concatenate([part[i] for part in parts], axis=axis) for i in range(2))

    by_head = lambda t: cat([msk(t, 0), msk(t, 1)], 0)

    eye = jnp.where(_iota2((TB, TB), 0) == _iota2((TB, TB), 1), 1.0, 0.0)
    pairs = range(npair)
    heads = [(p, h) for p in pairs for h in range(2)]
    at_t, rt_t, bt_t, kt_t, v_t, bp_t, kp_t, ec_t = ([_tile(t, p) for p in pairs]
                                                     for t in (at, rt, bt, kt, v, bp, kp, e_c))
    at_s = [_split(t) for t in at_t]
    v_h = [by_head(_split(t)) for t in v_t]
    bk = [cat([by_head(_split(bt_t[p])), by_head(_split(kt_t[p]))], 0) for p in pairs]
    sc_a = [_dot3(at_s[p], bk[p], NT) for p in pairs]
    sc_r = [_dot1(rt_t[p], bk[p], NT) for p in pairs]
    a_ab = [jnp.where(strict, sc_a[p][:, h * TB:(h + 1) * TB], 0.0) for p, h in heads]
    a_ak = [jnp.where(strict, sc_a[p][:, (2 + h) * TB:(3 + h) * TB], 0.0) for p, h in heads]
    p_rb = [jnp.where(incl, sc_r[p][:, h * TB:(h + 1) * TB], 0.0) for p, h in heads]
    p_rk = [jnp.where(incl, sc_r[p][:, (2 + h) * TB:(3 + h) * TB], 0.0) for p, h in heads]
    yield
    base = min(chunk, INV_BLOCK)
    assert chunk in (base, 2 * base)
    in_base = _same_block(TB, base)
    n_in = [jnp.where(in_base, n, 0.0) for n in a_ab]
    t_inv = [eye + n for n in n_in]
    levels = int(math.log2(base))
    splits = lambda mats: [_split(mat) for mat in mats]
    powers = [_dot3(n_s, n_s) for n_s in splits(n_in)] if levels > 1 else n_in
    yield
    for level in range(1, levels):
        x_s, t_s = splits(powers), splits(t_inv)
        if level + 1 < levels:
            both = [_dot3(xs, cat([xs, ts], 1)) for xs, ts in zip(x_s, t_s)]
            powers = [b[:, :TB] for b in both]
            t_inv = [t + b[:, TB:] for t, b in zip(t_inv, both)]
        else:
            t_inv = [t + _dot3(xs, ts) for t, xs, ts in zip(t_inv, x_s, t_s)]
        yield
    if chunk > base:
        t_s = splits(t_inv)
        n_off = [_dot3(n - ni, ts) for n, ni, ts in zip(a_ab, n_in, t_s)]
        yield
        t_inv = [t + _dot3(ts, no) for t, ts, no in zip(t_inv, t_s, n_off)]
    yield
    av = [_dot3(jnp.concatenate(a_ak[2 * p:2 * p + 2], axis=1), v_h[p]) for p in pairs]
    av_s = splits(av)
    tay = [_dot3(jnp.concatenate(t_inv[2 * p:2 * p + 2], axis=1),
                 cat([cat([msk(at_s[p], h), msk(av_s[p], h)], 1) for h in range(2)], 0))
           for p in pairs]
    yield
    ta = [t[:, :LANES] for t in tay]
    yy = [t[:, LANES:] for t in tay]
    qz = [_dot1(jnp.concatenate(p_rb[2 * p:2 * p + 2], axis=1),
                jnp.concatenate([jnp.concatenate([msk(ta[p], h), msk(yy[p], h)], axis=1) for h in range(2)], axis=0))
          for p in pairs]
    zv = [_dot1(jnp.concatenate(p_rk[2 * p:2 * p + 2], axis=1), v_h[p]) for p in pairs]
    qq = [rt_t[p] + qz[p][:, :LANES] for p in pairs]
    zz = [qz[p][:, LANES:] + zv[p] for p in pairs]

    bd = _same_block(LANES, HEAD)
    diag = _iota2((LANES, LANES), 0) == _iota2((LANES, LANES), 1)
    rows = lambda t, c: t[c * chunk:(c + 1) * chunk]
    m_mat = [[jnp.where(diag, ec_t[p][c * chunk:c * chunk + 1, :], 0.0)
              + jnp.where(bd, _dot3(rows(ta[p], c), rows(bp_t[p], c), TN), 0.0) for p in pairs]
             for c in range(nchunk)]
    yield
    n_mat = [[jnp.where(bd, _dot3(jnp.concatenate([rows(yy[p], c), rows(v_t[p], c)], axis=0),
                                  jnp.concatenate([rows(bp_t[p], c), rows(kp_t[p], c)], axis=0), TN), 0.0)
              for p in pairs] for c in range(nchunk)]
    yield
    state = [s_scr[p] for p in pairs] if chained else None
    o_rows = [[] for _ in pairs]
    for c in range(nchunk):
        for p in pairs:
            s = _split(state[p] if chained else _wkv_pair_load(s0_ref, (c,), p))
            o_rows[p].append(_dot1(rows(qq[p], c), s, NT) + rows(zz[p], c))
            s = _dot3(s, m_mat[c][p]) + n_mat[c][p]
            if chained:
                state[p] = s
            else:
                _wkv_pair_store(sout_ref, (c,), p, s)
        if chained:
            yield
    if chained:
        for p in pairs:
            s_scr[p] = state[p]

        @pl.when(pl.program_id(1) == pl.num_programs(1) - 1)
        def _():
            for p in pairs:
                _wkv_pair_store(sout_ref, (), p, state[p])
    o = jnp.concatenate([jnp.concatenate(o_rows[p], axis=0) for p in pairs], axis=1)
    yield

    inv_n = 1.0 / HEAD
    mean = headsum(o) * inv_n
    d = o - mean
    var = headsum(d * d) * inv_n
    o = d * lax.rsqrt(var + 1e-5 * HEAD) * ln_w + ln_b
    o = o + headsum(r * kh * r_k) * v
    z_ref[...] = o * g


def _gla_pair_load(ref, idx, p):
    return jnp.concatenate([ref[idx + (2 * p,)], ref[idx + (2 * p + 1,)]], axis=0).T


def _gla_pair_store(ref, idx, p, s):
    st = s.T
    dk = st.shape[0] // 2
    ref[idx + (2 * p,)] = st[:dk]
    ref[idx + (2 * p + 1,)] = st[dk:]


def _gla_body(chunk, chained, qkv_ref, lga_ref, og_ref, s0_ref, wg2_ref, bg_ref, nw_ref,
              z_ref, sout_ref, s_scr):
    nchunk = TB // chunk
    vw = z_ref.shape[-1]
    nhead = vw // LANES
    npair = nhead // 2
    kw = npair * LANES
    qkv = qkv_ref[...]
    q = qkv[:, 0:kw] * (HEAD ** -0.5)
    k = qkv[:, kw:2 * kw]
    v = qkv[:, 2 * kw:2 * kw + vw]
    gl = _dot(lga_ref[...].astype(BF16), wg2_ref[...]) + bg_ref[...]
    log_a = -_softplus(-gl) * (1.0 / GLA_GATE_TEMP)

    incl, _ = _chunk_masks(chunk)
    cum, cum_c = _chunk_cumsum(log_a, incl, chunk)
    qt = q * jnp.exp(cum)
    kt = k * jnp.exp(-cum)
    kp = k * jnp.exp(cum_c - cum)
    e_c = jnp.exp(cum_c)
    yield
    lo_half, hi_half = _half_masks()

    def msk(t, h):
        return jnp.where(lo_half if h == 0 else hi_half, t, 0.0)

    if chained:
        @pl.when(pl.program_id(1) == 0)
        def _():
            for p in range(npair):
                s_scr[p] = _gla_pair_load(s0_ref, (), p)

    pairs = range(npair)
    heads = [(p, h) for p in pairs for h in range(2)]
    rows = lambda t, c: t[c * chunk:(c + 1) * chunk]
    kt_t, kp_t, ec_t = ([_tile(t, p) for p in pairs] for t in (kt, kp, e_c))
    v_t = [_tile(v, h) for h in range(nhead)]
    qm = [msk(_tile(qt, p), h) for p, h in heads]
    sc = [_dot1(jnp.concatenate(qm[2 * p:2 * p + 2], axis=0), kt_t[p], NT) for p in pairs]
    intra = [_dot1(jnp.where(incl, sc[p][h * TB:(h + 1) * TB], 0.0), v_t[2 * p + h]) for p, h in heads]
    upd = [[jnp.where(lo_half, _dot3(rows(v_t[2 * p], c), rows(kp_t[p], c), TN),
                      _dot3(rows(v_t[2 * p + 1], c), rows(kp_t[p], c), TN)) for p in pairs]
           for c in range(nchunk)]
    yield
    state = [s_scr[p] for p in pairs] if chained else None
    inter = [[] for _ in range(nhead)]
    for c in range(nchunk):
        for p in pairs:
            s = state[p] if chained else _gla_pair_load(s0_ref, (c,), p)
            io = _dot3(jnp.concatenate([rows(qm[2 * p], c), rows(qm[2 * p + 1], c)], axis=0), s, NT)
            inter[2 * p].append(io[:chunk])
            inter[2 * p + 1].append(io[chunk:])
            s = s * ec_t[p][c * chunk:c * chunk + 1, :] + upd[c][p]
            if chained:
                state[p] = s
            else:
                _gla_pair_store(sout_ref, (c,), p, s)
    if chained:
        for p in pairs:
            s_scr[p] = state[p]

        @pl.when(pl.program_id(1) == pl.num_programs(1) - 1)
        def _():
            for p in pairs:
                _gla_pair_store(sout_ref, (), p, state[p])
    o_heads = [intra[h] + jnp.concatenate(inter[h], axis=0) for h in range(nhead)]
    yield

    og = og_ref[...]
    for h in range(nhead):
        o = o_heads[h]
        o = o * lax.rsqrt(jnp.mean(o * o, axis=-1, keepdims=True) + NORM_EPS) * nw_ref[...]
        z_ref[:, h * LANES:(h + 1) * LANES] = o * jax.nn.silu(_tile(og, h))


def _mixers_body(chunk, chained, x_ref, prev_ref, wkv0_ref, qkv_ref, lga_ref, og_ref, gla0_ref,
                 mu_ref, vec_ref, wlo_ref, g2_ref, wg2_ref, bg_ref, nw_ref,
                 za_ref, wkv_ref, zb_ref, gla_ref, carry_ref, wkv_scr, gla_scr):
    rwkv = _rwkv_body(chunk, chained, x_ref, prev_ref, wkv0_ref, mu_ref, vec_ref, wlo_ref, g2_ref, za_ref, wkv_ref,
                      carry_ref, wkv_scr)
    gla = _gla_body(chunk, chained, qkv_ref, lga_ref, og_ref, gla0_ref, wg2_ref, bg_ref, nw_ref, zb_ref, gla_ref,
                    gla_scr)
    for _ in itertools.zip_longest(rwkv, gla):
        pass


def _mixers(x_rw, prev, wkv0, qkv, lga, og, gla0, w, seq_len):
    width = wkv0.shape[1] * wkv0.shape[2]
    vw = og.shape[-1]
    cols = x_rw.shape[-1]
    chained = seq_len > TB
    if chained:
        chunk = PROMPT_CHUNK
        b, t, _ = x_rw.shape
        grid = (b, t // TB)
        rows = lambda n: pl.BlockSpec((None, TB, n), lambda i, j: (i, j, 0))
        prev_spec = pl.BlockSpec((None, 1, cols), lambda i, j: (i, 0, 0))
        st = lambda a: pl.BlockSpec((None,) + a.shape[1:], lambda i, j: (i, 0, 0, 0))
        lead = (b, t)
    else:
        chunk = seq_len
        m = x_rw.shape[1]
        assert m % TB == 0
        grid = (m // TB, 1)
        rows = lambda n: pl.BlockSpec((None, TB, n), lambda i, j: (0, i, 0))
        prev_spec = pl.BlockSpec((TB, cols), lambda i, j: (i, 0))
        st = lambda a: pl.BlockSpec((TB // chunk,) + a.shape[1:], lambda i, j: (i, 0, 0, 0))
        lead = (1, m)
    const = lambda a: pl.BlockSpec(a.shape, lambda i, j: (0,) * a.ndim)
    consts = [w["mu_shift"], w["rwkv_vec"], w["rwkv_wlo"], w["rwkv_g2"], w["gla_wg2"], w["gla_bg"], w["gla_norm_w"]]
    return pl.pallas_call(
        functools.partial(_mixers_body, chunk, chained),
        grid=grid,
        in_specs=[rows(cols), prev_spec, st(wkv0), rows(qkv.shape[-1]), rows(lga.shape[-1]), rows(vw), st(gla0)]
        + [const(a) for a in consts],
        out_specs=[rows(width), st(wkv0), rows(vw), st(gla0)],
        out_shape=[jax.ShapeDtypeStruct(lead + (width,), F32), jax.ShapeDtypeStruct(wkv0.shape, F32),
                   jax.ShapeDtypeStruct(lead + (vw,), F32), jax.ShapeDtypeStruct(gla0.shape, F32)],
        scratch_shapes=[pltpu.VMEM((1, cols), F32), pltpu.VMEM((wkv0.shape[1] // 2, LANES, LANES), F32),
                        pltpu.VMEM((gla0.shape[1] // 2, LANES, LANES), F32)],
        compiler_params=pltpu.CompilerParams(dimension_semantics=("parallel", "arbitrary"),
                                             vmem_limit_bytes=VMEM_LIMIT),
        name="mixers",
    )(x_rw, prev, wkv0, qkv, lga, og, gla0, *consts)


def _post_body(chained, seq_len, x_ref, za_ref, zb_ref, gate_ref, aux_ref, woa_ref, wob_ref, wo_ref, g_ref,
               wup_ref, cw_ref, cb_ref, wd_ref, nf_ref, y_ref, tail_ref, carry_ref, act_ref):
    tm, d = x_ref.shape
    hidden = wd_ref.shape[0]
    ya = _dot(za_ref[...].astype(BF16), woa_ref[...])
    yb = _dot(zb_ref[...].astype(BF16), wob_ref[...])
    gate = gate_ref[...]
    merged = jax.nn.sigmoid(gate[:, :d]) * ya + jax.nn.sigmoid(gate[:, d:]) * yb
    x1 = x_ref[...] + _dot(merged.astype(BF16), wo_ref[...])
    hb = _rmsnorm(x1, g_ref[...]).astype(BF16)

    if chained:
        @pl.when(pl.program_id(1) == 0)
        def _():
            carry_ref[0:6, :] = jnp.zeros((6, carry_ref.shape[1]), F32)
            carry_ref[6:8, :] = aux_ref[...]
    else:
        t = _iota2((tm, 1), 0) & (seq_len - 1)

    def conv(u, cs):
        if chained:
            ext = jnp.concatenate([carry_ref[:, cs], u], axis=0)
            carry_ref[:, cs] = u[tm - 8:, :]
            tail_ref[:, cs] = u[tm - 8:, :]
            inner = cw_ref[1:2, cs] * ext + pltpu.roll(cw_ref[0:1, cs] * ext, 1, axis=0)
            return (cb_ref[:, cs] + cw_ref[2:3, cs] * ext + pltpu.roll(inner, 1, axis=0))[8:]
        else:
            pa = aux_ref[:, cs]
            p1 = jnp.where(t == 0, pltpu.roll(pa, tm - 1, axis=0), pltpu.roll(u, 1, axis=0))
            p2 = jnp.where(t < 2, pa, pltpu.roll(u, 2, axis=0))
            tail_ref[:, cs] = u
        return cb_ref[:, cs] + cw_ref[0:1, cs] * p2 + cw_ref[1:2, cs] * p1 + cw_ref[2:3, cs] * u

    def up(j0):
        cols = (slice(j0, j0 + FFN_CHUNK), slice(hidden + j0, hidden + j0 + FFN_CHUNK))
        return [(_dot(hb, wup_ref[:, cs]), cs) for cs in cols]

    starts = list(range(0, hidden, FFN_CHUNK))
    group_end = [starts[(len(starts) + 1) // 2 - 1], starts[-1]]
    out = x1
    nxt = up(starts[0])
    k0 = 0
    for i, j0 in enumerate(starts):
        cur = nxt
        if i + 1 < len(starts):
            nxt = up(starts[i + 1])
        val, gat = (conv(u, cs) for u, cs in cur)
        act_ref[:, j0:j0 + FFN_CHUNK] = (jax.nn.gelu(gat) * val).astype(BF16)
        if j0 in group_end:
            k1 = j0 + FFN_CHUNK
            out = out + _dot(act_ref[:, k0:k1], wd_ref[k0:k1, :])
            k0 = k1
    y_ref[...] = _rmsnorm(out, nf_ref[...])


def _post(x, za, zb, gate, conv_state, w, seq_len):
    b, t, d = x.shape
    m = b * t
    f2 = conv_state.shape[-1]
    chained = seq_len >= POST_TM
    assert seq_len % POST_TM == 0 if chained else POST_TM_SHORT % seq_len == 0
    weights = [w["w_out_a"], w["w_out_b"], w["w_o"], w["norm_ffn"], w["ffn_w_up"], w["ffn_conv_w"],
               w["ffn_conv_b"], w["ffn_w_down"], w["norm_final"]]
    const = lambda a: pl.BlockSpec(a.shape, lambda i, j: (0,) * a.ndim, pipeline_mode=pl.Buffered(1))
    if chained:
        tm = POST_TM
        grid = (b, t // tm)
        rows = lambda a: pl.BlockSpec((None, tm, a.shape[-1]), lambda i, j: (i, j, 0))
        acts = [x, za.reshape(b, t, -1), zb.reshape(b, t, -1), gate.reshape(b, t, -1)]
        aux, aux_spec = conv_state, pl.BlockSpec((None, 2, f2), lambda i, j: (i, 0, 0))
        tail_spec = pl.BlockSpec((None, 8, f2), lambda i, j: (i, 0, 0))
        tail_shape = (b, 8, f2)
    else:
        tm = POST_TM_SHORT
        grid = (m // tm, 1)
        rows = lambda a: pl.BlockSpec((tm, a.shape[-1]), lambda i, j: (i, 0))
        acts = [x.reshape(m, d), za, zb, gate]
        aux = jnp.pad(conv_state, ((0, 0), (0, seq_len - 2), (0, 0))).reshape(m, f2)
        aux_spec = tail_spec = pl.BlockSpec((tm, f2), lambda i, j: (i, 0))
        tail_shape = (m, f2)
    y, tail = pl.pallas_call(
        functools.partial(_post_body, chained, seq_len),
        grid=grid,
        in_specs=[rows(a) for a in acts] + [aux_spec] + [const(a) for a in weights],
        out_specs=[rows(acts[0]), tail_spec],
        out_shape=[jax.ShapeDtypeStruct(acts[0].shape, F32), jax.ShapeDtypeStruct(tail_shape, F32)],
        scratch_shapes=[pltpu.VMEM((8, f2), F32), pltpu.VMEM((tm, f2 // 2), BF16)],
        compiler_params=pltpu.CompilerParams(dimension_semantics=("parallel", "arbitrary"),
                                             vmem_limit_bytes=VMEM_LIMIT),
        name="post_mixer",
    )(*acts, aux, *weights)
    new_conv = tail[:, -2:] if chained else tail.reshape(b, t, f2)[:, -2:]
    return y.reshape(b, t, d), new_conv


def _trunk(x, shift0, wkv0, gla0, conv0, w):
    b, t, d = x.shape
    m = b * t
    xf = x.reshape(m, d)
    p_rw, p_qkv, p_og, p_gate, p_lga = _norm_proj(xf, w["norm_mix"], w["w_in_parts"])
    shift_cols = p_rw.shape[1]
    new_shift = p_rw.reshape(b, t, shift_cols)[:, -1]

    if t > TB:
        shape3 = lambda a: a.reshape(b, t, a.shape[1])
        prev = shift0[:, None, :]
    else:
        shape3 = lambda a: a[None]
        prev = jnp.pad(shift0[:, None, :], ((0, 0), (0, t - 1), (0, 0))).reshape(m, shift_cols)
    z_a, wkv_new, z_b, gla_new = _mixers(shape3(p_rw), prev, wkv0, shape3(p_qkv), shape3(p_lga), shape3(p_og),
                                          gla0, w, t)
    z_a = z_a.reshape(m, z_a.shape[-1])
    z_b = z_b.reshape(m, z_b.shape[-1])

    y, new_conv = _post(x, z_a, z_b, p_gate, conv0, w, t)
    return (y, new_shift[None], wkv_new[None], gla_new[None], new_conv[None])


def kernel(x_prompt, x_sample, state_rwkv_shift, state_rwkv_wkv, state_gla, state_ffn_conv, norm_mix, w_in, mu_shift, rwkv_w0, rwkv_w2, rwkv_a0, rwkv_a2, rwkv_g2, rwkv_k_k, rwkv_k_a, rwkv_r_k, rwkv_ln_w, rwkv_ln_b, gla_wg2, gla_bg, gla_norm_w, w_out_a, w_out_b, w_o, norm_ffn, ffn_w_up, ffn_conv_w, ffn_conv_b, ffn_w_down, norm_final):
    assert norm_mix.shape[0] == 1, "single-layer step"
    d = x_prompt.shape[-1]
    width = rwkv_w0.shape[-1]
    shift_cols = mu_shift.shape[-1]
    kw = gla_wg2.shape[-1]
    vw = w_out_b.shape[1]
    lora_g = gla_wg2.shape[1]
    lw, la = rwkv_w2.shape[1], rwkv_a2.shape[1]
    assert lw == HEAD and la == HEAD and rwkv_g2.shape[1] == LANES

    win = w_in[0].astype(BF16)
    c0 = shift_cols
    c1 = c0 + 2 * kw + vw
    c2 = c1 + lora_g
    c3 = c2 + vw
    w_lga = jnp.pad(win[:, c1:c2], ((0, 0), (0, LANES - lora_g)))
    w_in_parts = [win[:, :c0], win[:, c0:c1], win[:, c2:c3], win[:, c3:], w_lga]

    zw = jnp.zeros((lw, width), BF16)
    wlo = jnp.concatenate([jnp.concatenate([rwkv_w2[0].astype(BF16), zw], axis=1),
                           jnp.concatenate([zw, rwkv_a2[0].astype(BF16)], axis=1)], axis=0)
    vec = jnp.stack([rwkv_w0[0], rwkv_a0[0], rwkv_k_k[0], rwkv_k_a[0], rwkv_r_k[0].reshape(width),
                     rwkv_ln_w[0], rwkv_ln_b[0], jnp.zeros((width,), F32)])
    w = dict(
        norm_mix=norm_mix, w_in_parts=w_in_parts, mu_shift=mu_shift, rwkv_vec=vec, rwkv_wlo=wlo,
        rwkv_g2=rwkv_g2[0].astype(BF16),
        gla_wg2=jnp.pad(gla_wg2[0].astype(BF16), ((0, LANES - lora_g), (0, 0))), gla_bg=gla_bg,
        gla_norm_w=gla_norm_w,
        w_out_a=w_out_a[0].astype(BF16), w_out_b=w_out_b[0].astype(BF16), w_o=w_o[0].astype(BF16),
        norm_ffn=norm_ffn, ffn_w_up=ffn_w_up[0].astype(BF16), ffn_conv_w=ffn_conv_w[0],
        ffn_conv_b=ffn_conv_b, ffn_w_down=ffn_w_down[0].astype(BF16), norm_final=norm_final[None],
    )

    bp = x_prompt.shape[0]
    dt = x_prompt.dtype
    zeros = lambda s: jnp.zeros((bp,) + s.shape[2:], dt)
    y_p, shift_p, wkv_p, gla_p, conv_p = _trunk(
        x_prompt, zeros(state_rwkv_shift), zeros(state_rwkv_wkv), zeros(state_gla), zeros(state_ffn_conv), w)
    y_s, shift_s, wkv_s, gla_s, conv_s = _trunk(
        x_sample, state_rwkv_shift[0], state_rwkv_wkv[0], state_gla[0], state_ffn_conv[0], w)
    return (y_p, y_s, shift_p, wkv_p, gla_p, conv_p, shift_s, wkv_s, gla_s, conv_s)
```

```python
import functools
import itertools
import math

import jax
import jax.numpy as jnp
from jax import lax
from jax.experimental import pallas as pl
from jax.experimental.pallas import tpu as pltpu

F32 = jnp.float32
BF16 = jnp.bfloat16
LANES = 128
BF16_ROWS = 16
MXU_DEPTH = 256
TB = 128
TM = 512
POST_TM = 512
POST_TM_SHORT = 128
NCHUNK = 512
FFN_CHUNK = 256
VMEM_LIMIT = 56 * 1024 * 1024
NORM_EPS = 1e-6
HEAD = 64
GLA_GATE_TEMP = 16.0
PROMPT_CHUNK = 32
INV_BLOCK = 16
NN = ((1,), (0,))
NT = ((1,), (1,))
TN = ((0,), (0,))


def _dot(a, b, dims=NN):
    return lax.dot_general(a, b, (dims, ((), ())), preferred_element_type=F32)


def _split(x):
    hi = x.astype(BF16)
    lo = (x - hi.astype(F32)).astype(BF16)
    return hi, lo


def _dot3(a, b, dims=NN):
    ka, kb = dims[0][0], dims[1][0]
    if dims == TN and not isinstance(a, tuple) and not isinstance(b, tuple) and a.shape[0] % BF16_ROWS:
        pad = BF16_ROWS - a.shape[0] % BF16_ROWS
        a = jnp.concatenate([a, jnp.zeros((pad, a.shape[1]), F32)], axis=0)
        b = jnp.concatenate([b, jnp.zeros((pad, b.shape[1]), F32)], axis=0)
    ah, al = a if isinstance(a, tuple) else _split(a)
    bh, bl = b if isinstance(b, tuple) else _split(b)
    k = ah.shape[ka]
    if dims == NN and k == MXU_DEPTH and 2 * bh.shape[1] <= MXU_DEPTH:
        n = bh.shape[1]
        wide = _dot(ah, jnp.concatenate([bh, bl], axis=1))
        return wide[:, :n] + (wide[:, n:] + _dot(al, bh))
    if k % BF16_ROWS or 2 * k > MXU_DEPTH:
        return _dot(ah, bh, dims) + (_dot(ah, bl, dims) + _dot(al, bh, dims))
    if 3 * k <= MXU_DEPTH:
        return _dot(jnp.concatenate([ah, al, ah], axis=ka), jnp.concatenate([bh, bh, bl], axis=kb), dims)
    n = bh.shape[1]
    if dims == NN and 2 * n <= MXU_DEPTH:
        wide = _dot(jnp.concatenate([ah, al], axis=1),
                    jnp.concatenate([jnp.concatenate([bh, bl], axis=1),
                                     jnp.concatenate([bh, jnp.zeros_like(bl)], axis=1)], axis=0))
        return wide[:, :n] + wide[:, n:]
    return _dot(jnp.concatenate([ah, al], axis=ka), jnp.concatenate([bh, bh], axis=kb), dims) + _dot(ah, bl, dims)


def _dot1(a, b, dims=NN):
    ah = a[0] if isinstance(a, tuple) else a.astype(BF16)
    bh = b[0] if isinstance(b, tuple) else b.astype(BF16)
    return _dot(ah, bh, dims)


def _split3(x):
    a1 = x.astype(BF16)
    r1 = x - a1.astype(F32)
    a2 = r1.astype(BF16)
    a3 = (r1 - a2.astype(F32)).astype(BF16)
    return a1, a2, a3


def _dot_exact_lhs(e, x):
    x1, x2, x3 = _split3(x)
    return _dot(jnp.concatenate([e, e], axis=1), jnp.concatenate([x1, x2], axis=0)) + _dot(e, x3)


def _dot_exact_rhs(x, e):
    return _dot(jnp.concatenate(_split(x), axis=1), jnp.concatenate([e, e], axis=0))


def _iota2(shape, dim):
    return lax.broadcasted_iota(jnp.int32, shape, dim)


def _same_block(n, size):
    sh = int(math.log2(size))
    return (_iota2((n, n), 0) >> sh) == (_iota2((n, n), 1) >> sh)


def _ones_where(mask):
    return jnp.where(mask, 1.0, 0.0).astype(BF16)


def _softplus(y):
    return jnp.maximum(y, 0.0) + jnp.log(1.0 + jnp.exp(-jnp.abs(y)))


def _rmsnorm(x, g):
    return x * lax.rsqrt(jnp.mean(x * x, axis=-1, keepdims=True) + NORM_EPS) * g


def _tile(x, p):
    return x[:, p * LANES:(p + 1) * LANES]


def _chunk_masks(chunk):
    same = _same_block(TB, chunk)
    r = _iota2((TB, TB), 0)
    c = _iota2((TB, TB), 1)
    return same & (c <= r), same & (c < r)


def _chunk_cumsum(x, incl, chunk):
    cum = _dot_exact_lhs(_ones_where(incl), x)
    total = jnp.concatenate([jnp.broadcast_to(cum[c + chunk - 1:c + chunk, :], (chunk, x.shape[1]))
                             for c in range(0, TB, chunk)], axis=0)
    return cum, total


def _half_masks():
    lane = _iota2((1, LANES), 1)
    lo = lane < HEAD
    return lo, jnp.logical_not(lo)


def _norm_proj_body(nw, x_ref, g_ref, *refs):
    w_refs, o_refs = refs[:nw], refs[nw:]
    hb = _rmsnorm(x_ref[...], g_ref[...]).astype(BF16)
    for w_ref, o_ref in zip(w_refs, o_refs):
        n = w_ref.shape[1]
        for n0 in range(0, n, NCHUNK):
            n1 = min(n0 + NCHUNK, n)
            o_ref[:, n0:n1] = _dot(hb, w_ref[:, n0:n1])


def _norm_proj(x, g, weights):
    m, d = x.shape
    nw = len(weights)
    return pl.pallas_call(
        functools.partial(_norm_proj_body, nw),
        grid=(m // TM,),
        in_specs=[pl.BlockSpec((TM, d), lambda i: (i, 0)), pl.BlockSpec((1, d), lambda i: (0, 0))]
        + [pl.BlockSpec(w.shape, lambda i: (0, 0), pipeline_mode=pl.Buffered(1)) for w in weights],
        out_specs=[pl.BlockSpec((TM, w.shape[1]), lambda i: (i, 0)) for w in weights],
        out_shape=[jax.ShapeDtypeStruct((m, w.shape[1]), F32) for w in weights],
        compiler_params=pltpu.CompilerParams(dimension_semantics=("parallel",), vmem_limit_bytes=VMEM_LIMIT),
        name="norm_proj",
    )(x, g, *weights)


def _wkv_pair_load(ref, idx, p):
    a, b = ref[idx + (2 * p,)], ref[idx + (2 * p + 1,)]
    z = jnp.zeros_like(a)
    return jnp.concatenate([jnp.concatenate([a, z], axis=1), jnp.concatenate([z, b], axis=1)], axis=0)


def _wkv_pair_store(ref, idx, p, s):
    n = s.shape[0] // 2
    ref[idx + (2 * p,)] = s[:n, :n]
    ref[idx + (2 * p + 1,)] = s[n:, n:]


def _rwkv_body(chunk, chained, x_ref, prev_ref, s0_ref, mu_ref, vec_ref, wlo_ref, g2_ref,
               z_ref, sout_ref, carry_ref, s_scr):
    nchunk = TB // chunk
    width = z_ref.shape[-1]
    npair = width // LANES
    x = x_ref[...]
    row = _iota2((TB, 1), 0)
    rolled = pltpu.roll(x, 1, axis=0)
    if chained:
        @pl.when(pl.program_id(1) == 0)
        def _():
            carry_ref[...] = prev_ref[...]
            for p in range(npair):
                s_scr[p] = _wkv_pair_load(s0_ref, (), p)
        prev = jnp.where(row == 0, carry_ref[...], rolled)
        carry_ref[...] = x[TB - 1:TB, :]
    else:
        prev = jnp.where((row & (chunk - 1)) == 0, prev_ref[...], rolled)
    xs = x + (prev - x) * mu_ref[...]
    yield

    r = xs[:, 0:width]
    k = xs[:, width:2 * width]
    v = xs[:, 2 * width:3 * width]
    lora = xs[:, 3 * width:3 * width + LANES]
    lg = xs[:, 3 * width + LANES:3 * width + 2 * LANES]
    lo_half, hi_half = _half_masks()
    lora = jnp.where(lo_half, jnp.tanh(lora), lora)
    wa = _dot(lora.astype(BF16), wlo_ref[...])
    w0, a0, k_k, k_a = vec_ref[0:1, :], vec_ref[1:2, :], vec_ref[2:3, :], vec_ref[3:4, :]
    r_k, ln_w, ln_b = vec_ref[4:5, :], vec_ref[5:6, :], vec_ref[6:7, :]
    wlog = -_softplus(-(w0 + wa[:, :width])) - 0.5
    logw = -jnp.exp(wlog)
    asig = jax.nn.sigmoid(a0 + wa[:, width:])
    g = _dot(jax.nn.sigmoid(lg).astype(BF16), g2_ref[...])
    yield

    seg = _ones_where(_same_block(LANES, HEAD))

    def headsum(t):
        return jnp.concatenate([_dot_exact_rhs(_tile(t, p), seg) for p in range(npair)], axis=1)

    kk = k * k_k
    kk = kk / jnp.maximum(jnp.sqrt(headsum(kk * kk)), 1e-12)
    kh = k * (1.0 + (asig - 1.0) * k_a)
    a_vec = -kk
    b_vec = kk * asig
    yield

    incl, strict = _chunk_masks(chunk)
    cum, cum_c = _chunk_cumsum(logw, incl, chunk)
    e_in = jnp.exp(cum)
    e_out = jnp.exp(-cum)
    e_end = jnp.exp(cum_c - cum)
    yield
    rt = r * e_in
    at = a_vec * jnp.exp(cum - logw)
    bt = b_vec * e_out
    kt = kh * e_out
    bp = b_vec * e_end
    kp = kh * e_end
    e_c = jnp.exp(cum_c)
    yield

    def msk(t, h):
        keep = lo_half if h == 0 else hi_half
        if isinstance(t, tuple):
            return tuple(jnp.where(keep, x, jnp.zeros_like(x)) for x in t)
        return jnp.where(keep, t, 0.0)

    def cat(parts, axis):
        return tuple(jnp.concatenate([part[i] for part in parts], axis=axis) for i in range(2))

    by_head = lambda t: cat([msk(t, 0), msk(t, 1)], 0)

    eye = jnp.where(_iota2((TB, TB), 0) == _iota2((TB, TB), 1), 1.0, 0.0)
    pairs = range(npair)
    heads = [(p, h) for p in pairs for h in range(2)]
    at_t, rt_t, bt_t, kt_t, v_t, bp_t, kp_t, ec_t = ([_tile(t, p) for p in pairs]
                                                     for t in (at, rt, bt, kt, v, bp, kp, e_c))
    at_s = [_split(t) for t in at_t]
    v_h = [by_head(_split(t)) for t in v_t]
    bk = [cat([by_head(_split(bt_t[p])), by_head(_split(kt_t[p]))], 0) for p in pairs]
    sc_a = [_dot3(at_s[p], bk[p], NT) for p in pairs]
    sc_r = [_dot1(rt_t[p], bk[p], NT) for p in pairs]
    a_ab = [jnp.where(strict, sc_a[p][:, h * TB:(h + 1) * TB], 0.0) for p, h in heads]
    a_ak = [jnp.where(strict, sc_a[p][:, (2 + h) * TB:(3 + h) * TB], 0.0) for p, h in heads]
    p_rb = [jnp.where(incl, sc_r[p][:, h * TB:(h + 1) * TB], 0.0) for p, h in heads]
    p_rk = [jnp.where(incl, sc_r[p][:, (2 + h) * TB:(3 + h) * TB], 0.0) for p, h in heads]
    yield
    base = min(chunk, INV_BLOCK)
    assert chunk in (base, 2 * base)
    in_base = _same_block(TB, base)
    n_in = [jnp.where(in_base, n, 0.0) for n in a_ab]
    t_inv = [eye + n for n in n_in]
    levels = int(math.log2(base))
    splits = lambda mats: [_split(mat) for mat in mats]
    powers = [_dot3(n_s, n_s) for n_s in splits(n_in)] if levels > 1 else n_in
    yield
    for level in range(1, levels):
        x_s, t_s = splits(powers), splits(t_inv)
        if level + 1 < levels:
            both = [_dot3(xs, cat([xs, ts], 1)) for xs, ts in zip(x_s, t_s)]
            powers = [b[:, :TB] for b in both]
            t_inv = [t + b[:, TB:] for t, b in zip(t_inv, both)]
        else:
            t_inv = [t + _dot3(xs, ts) for t, xs, ts in zip(t_inv, x_s, t_s)]
        yield
    if chunk > base:
        t_s = splits(t_inv)
        n_off = [_dot3(n - ni, ts) for n, ni, ts in zip(a_ab, n_in, t_s)]
        yield
        t_inv = [t + _dot3(ts, no) for t, ts, no in zip(t_inv, t_s, n_off)]
    yield
    av = [_dot3(jnp.concatenate(a_ak[2 * p:2 * p + 2], axis=1), v_h[p]) for p in pairs]
    av_s = splits(av)
    tay = [_dot3(jnp.concatenate(t_inv[2 * p:2 * p + 2], axis=1),
                 cat([cat([msk(at_s[p], h), msk(av_s[p], h)], 1) for h in range(2)], 0))
           for p in pairs]
    yield
    ta = [t[:, :LANES] for t in tay]
    yy = [t[:, LANES:] for t in tay]
    qz = [_dot1(jnp.concatenate(p_rb[2 * p:2 * p + 2], axis=1),
                jnp.concatenate([jnp.concatenate([msk(ta[p], h), msk(yy[p], h)], axis=1) for h in range(2)], axis=0))
          for p in pairs]
    zv = [_dot1(jnp.concatenate(p_rk[2 * p:2 * p + 2], axis=1), v_h[p]) for p in pairs]
    qq = [rt_t[p] + qz[p][:, :LANES] for p in pairs]
    zz = [qz[p][:, LANES:] + zv[p] for p in pairs]

    bd = _same_block(LANES, HEAD)
    diag = _iota2((LANES, LANES), 0) == _iota2((LANES, LANES), 1)
    rows = lambda t, c: t[c * chunk:(c + 1) * chunk]
    m_mat = [[jnp.where(diag, ec_t[p][c * chunk:c * chunk + 1, :], 0.0)
              + jnp.where(bd, _dot3(rows(ta[p], c), rows(bp_t[p], c), TN), 0.0) for p in pairs]
             for c in range(nchunk)]
    yield
    n_mat = [[jnp.where(bd, _dot3(jnp.concatenate([rows(yy[p], c), rows(v_t[p], c)], axis=0),
                                  jnp.concatenate([rows(bp_t[p], c), rows(kp_t[p], c)], axis=0), TN), 0.0)
              for p in pairs] for c in range(nchunk)]
    yield
    state = [s_scr[p] for p in pairs] if chained else None
    o_rows = [[] for _ in pairs]
    for c in range(nchunk):
        for p in pairs:
            s = _split(state[p] if chained else _wkv_pair_load(s0_ref, (c,), p))
            o_rows[p].append(_dot1(rows(qq[p], c), s, NT) + rows(zz[p], c))
            s = _dot3(s, m_mat[c][p]) + n_mat[c][p]
            if chained:
                state[p] = s
            else:
                _wkv_pair_store(sout_ref, (c,), p, s)
        if chained:
            yield
    if chained:
        for p in pairs:
            s_scr[p] = state[p]

        @pl.when(pl.program_id(1) == pl.num_programs(1) - 1)
        def _():
            for p in pairs:
                _wkv_pair_store(sout_ref, (), p, state[p])
    o = jnp.concatenate([jnp.concatenate(o_rows[p], axis=0) for p in pairs], axis=1)
    yield

    inv_n = 1.0 / HEAD
    mean = headsum(o) * inv_n
    d = o - mean
    var = headsum(d * d) * inv_n
    o = d * lax.rsqrt(var + 1e-5 * HEAD) * ln_w + ln_b
    o = o + headsum(r * kh * r_k) * v
    z_ref[...] = o * g


def _gla_pair_load(ref, idx, p):
    return jnp.concatenate([ref[idx + (2 * p,)], ref[idx + (2 * p + 1,)]], axis=0).T


def _gla_pair_store(ref, idx, p, s):
    st = s.T
    dk = st.shape[0] // 2
    ref[idx + (2 * p,)] = st[:dk]
    ref[idx + (2 * p + 1,)] = st[dk:]


def _gla_body(chunk, chained, qkv_ref, lga_ref, og_ref, s0_ref, wg2_ref, bg_ref, nw_ref,
              z_ref, sout_ref, s_scr):
    nchunk = TB // chunk
    vw = z_ref.shape[-1]
    nhead = vw // LANES
    npair = nhead // 2
    kw = npair * LANES
    qkv = qkv_ref[...]
    q = qkv[:, 0:kw] * (HEAD ** -0.5)
    k = qkv[:, kw:2 * kw]
    v = qkv[:, 2 * kw:2 * kw + vw]
    gl = _dot(lga_ref[...].astype(BF16), wg2_ref[...]) + bg_ref[...]
    log_a = -_softplus(-gl) * (1.0 / GLA_GATE_TEMP)

    incl, _ = _chunk_masks(chunk)
    cum, cum_c = _chunk_cumsum(log_a, incl, chunk)
    qt = q * jnp.exp(cum)
    kt = k * jnp.exp(-cum)
    kp = k * jnp.exp(cum_c - cum)
    e_c = jnp.exp(cum_c)
    yield
    lo_half, hi_half = _half_masks()

    def msk(t, h):
        return jnp.where(lo_half if h == 0 else hi_half, t, 0.0)

    if chained:
        @pl.when(pl.program_id(1) == 0)
        def _():
            for p in range(npair):
                s_scr[p] = _gla_pair_load(s0_ref, (), p)

    pairs = range(npair)
    heads = [(p, h) for p in pairs for h in range(2)]
    rows = lambda t, c: t[c * chunk:(c + 1) * chunk]
    kt_t, kp_t, ec_t = ([_tile(t, p) for p in pairs] for t in (kt, kp, e_c))
    v_t = [_tile(v, h) for h in range(nhead)]
    qm = [msk(_tile(qt, p), h) for p, h in heads]
    sc = [_dot1(jnp.concatenate(qm[2 * p:2 * p + 2], axis=0), kt_t[p], NT) for p in pairs]
    intra = [_dot1(jnp.where(incl, sc[p][h * TB:(h + 1) * TB], 0.0), v_t[2 * p + h]) for p, h in heads]
    upd = [[jnp.where(lo_half, _dot3(rows(v_t[2 * p], c), rows(kp_t[p], c), TN),
                      _dot3(rows(v_t[2 * p + 1], c), rows(kp_t[p], c), TN)) for p in pairs]
           for c in range(nchunk)]
    yield
    state = [s_scr[p] for p in pairs] if chained else None
    inter = [[] for _ in range(nhead)]
    for c in range(nchunk):
        for p in pairs:
            s = state[p] if chained else _gla_pair_load(s0_ref, (c,), p)
            io = _dot3(jnp.concatenate([rows(qm[2 * p], c), rows(qm[2 * p + 1], c)], axis=0), s, NT)
            inter[2 * p].append(io[:chunk])
            inter[2 * p + 1].append(io[chunk:])
            s = s * ec_t[p][c * chunk:c * chunk + 1, :] + upd[c][p]
            if chained:
                state[p] = s
            else:
                _gla_pair_store(sout_ref, (c,), p, s)
    if chained:
        for p in pairs:
            s_scr[p] = state[p]

        @pl.when(pl.program_id(1) == pl.num_programs(1) - 1)
        def _():
            for p in pairs:
                _gla_pair_store(sout_ref, (), p, state[p])
    o_heads = [intra[h] + jnp.concatenate(inter[h], axis=0) for h in range(nhead)]
    yield

    og = og_ref[...]
    for h in range(nhead):
        o = o_heads[h]
        o = o * lax.rsqrt(jnp.mean(o * o, axis=-1, keepdims=True) + NORM_EPS) * nw_ref[...]
        z_ref[:, h * LANES:(h + 1) * LANES] = o * jax.nn.silu(_tile(og, h))


def _mixers_body(chunk, chained, x_ref, prev_ref, wkv0_ref, qkv_ref, lga_ref, og_ref, gla0_ref,
                 mu_ref, vec_ref, wlo_ref, g2_ref, wg2_ref, bg_ref, nw_ref,
                 za_ref, wkv_ref, zb_ref, gla_ref, carry_ref, wkv_scr, gla_scr):
    rwkv = _rwkv_body(chunk, chained, x_ref, prev_ref, wkv0_ref, mu_ref, vec_ref, wlo_ref, g2_ref, za_ref, wkv_ref,
                      carry_ref, wkv_scr)
    gla = _gla_body(chunk, chained, qkv_ref, lga_ref, og_ref, gla0_ref, wg2_ref, bg_ref, nw_ref, zb_ref, gla_ref,
                    gla_scr)
    for _ in itertools.zip_longest(rwkv, gla):
        pass


def _mixers(x_rw, prev, wkv0, qkv, lga, og, gla0, w, seq_len):
    width = wkv0.shape[1] * wkv0.shape[2]
    vw = og.shape[-1]
    cols = x_rw.shape[-1]
    chained = seq_len > TB
    if chained:
        chunk = PROMPT_CHUNK
        b, t, _ = x_rw.shape
        grid = (b, t // TB)
        rows = lambda n: pl.BlockSpec((None, TB, n), lambda i, j: (i, j, 0))
        prev_spec = pl.BlockSpec((None, 1, cols), lambda i, j: (i, 0, 0))
        st = lambda a: pl.BlockSpec((None,) + a.shape[1:], lambda i, j: (i, 0, 0, 0))
        lead = (b, t)
    else:
        chunk = seq_len
        m = x_rw.shape[1]
        assert m % TB == 0
        grid = (m // TB, 1)
        rows = lambda n: pl.BlockSpec((None, TB, n), lambda i, j: (0, i, 0))
        prev_spec = pl.BlockSpec((TB, cols), lambda i, j: (i, 0))
        st = lambda a: pl.BlockSpec((TB // chunk,) + a.shape[1:], lambda i, j: (i, 0, 0, 0))
        lead = (1, m)
    const = lambda a: pl.BlockSpec(a.shape, lambda i, j: (0,) * a.ndim)
    consts = [w["mu_shift"], w["rwkv_vec"], w["rwkv_wlo"], w["rwkv_g2"], w["gla_wg2"], w["gla_bg"], w["gla_norm_w"]]
    return pl.pallas_call(
        functools.partial(_mixers_body, chunk, chained),
        grid=grid,
        in_specs=[rows(cols), prev_spec, st(wkv0), rows(qkv.shape[-1]), rows(lga.shape[-1]), rows(vw), st(gla0)]
        + [const(a) for a in consts],
        out_specs=[rows(width), st(wkv0), rows(vw), st(gla0)],
        out_shape=[jax.ShapeDtypeStruct(lead + (width,), F32), jax.ShapeDtypeStruct(wkv0.shape, F32),
                   jax.ShapeDtypeStruct(lead + (vw,), F32), jax.ShapeDtypeStruct(gla0.shape, F32)],
        scratch_shapes=[pltpu.VMEM((1, cols), F32), pltpu.VMEM((wkv0.shape[1] // 2, LANES, LANES), F32),
                        pltpu.VMEM((gla0.shape[1] // 2, LANES, LANES), F32)],
        compiler_params=pltpu.CompilerParams(dimension_semantics=("parallel", "arbitrary"),
                                             vmem_limit_bytes=VMEM_LIMIT),
        name="mixers",
    )(x_rw, prev, wkv0, qkv, lga, og, gla0, *consts)


def _post_body(chained, seq_len, x_ref, za_ref, zb_ref, gate_ref, aux_ref, woa_ref, wob_ref, wo_ref, g_ref,
               wup_ref, cw_ref, cb_ref, wd_ref, nf_ref, y_ref, tail_ref, carry_ref, act_ref):
    tm, d = x_ref.shape
    hidden = wd_ref.shape[0]
    ya = _dot(za_ref[...].astype(BF16), woa_ref[...])
    yb = _dot(zb_ref[...].astype(BF16), wob_ref[...])
    gate = gate_ref[...]
    merged = jax.nn.sigmoid(gate[:, :d]) * ya + jax.nn.sigmoid(gate[:, d:]) * yb
    x1 = x_ref[...] + _dot(merged.astype(BF16), wo_ref[...])
    hb = _rmsnorm(x1, g_ref[...]).astype(BF16)

    if chained:
        @pl.when(pl.program_id(1) == 0)
        def _():
            carry_ref[0:6, :] = jnp.zeros((6, carry_ref.shape[1]), F32)
            carry_ref[6:8, :] = aux_ref[...]
    else:
        t = _iota2((tm, 1), 0) & (seq_len - 1)

    def conv(u, cs):
        if chained:
            ext = jnp.concatenate([carry_ref[:, cs], u], axis=0)
            carry_ref[:, cs] = u[tm - 8:, :]
            tail_ref[:, cs] = u[tm - 8:, :]
            inner = cw_ref[1:2, cs] * ext + pltpu.roll(cw_ref[0:1, cs] * ext, 1, axis=0)
            return (cb_ref[:, cs] + cw_ref[2:3, cs] * ext + pltpu.roll(inner, 1, axis=0))[8:]
        else:
            pa = aux_ref[:, cs]
            p1 = jnp.where(t == 0, pltpu.roll(pa, tm - 1, axis=0), pltpu.roll(u, 1, axis=0))
            p2 = jnp.where(t < 2, pa, pltpu.roll(u, 2, axis=0))
            tail_ref[:, cs] = u
        return cb_ref[:, cs] + cw_ref[0:1, cs] * p2 + cw_ref[1:2, cs] * p1 + cw_ref[2:3, cs] * u

    def up(j0):
        cols = (slice(j0, j0 + FFN_CHUNK), slice(hidden + j0, hidden + j0 + FFN_CHUNK))
        return [(_dot(hb, wup_ref[:, cs]), cs) for cs in cols]

    starts = list(range(0, hidden, FFN_CHUNK))
    nxt = up(starts[0])
    for i, j0 in enumerate(starts):
        cur = nxt
        if i + 1 < len(starts):
            nxt = up(starts[i + 1])
        val, gat = (conv(u, cs) for u, cs in cur)
        act_ref[:, j0:j0 + FFN_CHUNK] = (jax.nn.gelu(gat) * val).astype(BF16)
    y_ref[...] = _rmsnorm(x1 + _dot(act_ref[...], wd_ref[...]), nf_ref[...])


def _post(x, za, zb, gate, conv_state, w, seq_len):
    b, t, d = x.shape
    m = b * t
    f2 = conv_state.shape[-1]
    chained = seq_len >= POST_TM
    assert seq_len % POST_TM == 0 if chained else POST_TM_SHORT % seq_len == 0
    weights = [w["w_out_a"], w["w_out_b"], w["w_o"], w["norm_ffn"], w["ffn_w_up"], w["ffn_conv_w"],
               w["ffn_conv_b"], w["ffn_w_down"], w["norm_final"]]
    const = lambda a: pl.BlockSpec(a.shape, lambda i, j: (0,) * a.ndim, pipeline_mode=pl.Buffered(1))
    if chained:
        tm = POST_TM
        grid = (b, t // tm)
        rows = lambda a: pl.BlockSpec((None, tm, a.shape[-1]), lambda i, j: (i, j, 0))
        acts = [x, za.reshape(b, t, -1), zb.reshape(b, t, -1), gate.reshape(b, t, -1)]
        aux, aux_spec = conv_state, pl.BlockSpec((None, 2, f2), lambda i, j: (i, 0, 0))
        tail_spec = pl.BlockSpec((None, 8, f2), lambda i, j: (i, 0, 0))
        tail_shape = (b, 8, f2)
    else:
        tm = POST_TM_SHORT
        grid = (m // tm, 1)
        rows = lambda a: pl.BlockSpec((tm, a.shape[-1]), lambda i, j: (i, 0))
        acts = [x.reshape(m, d), za, zb, gate]
        aux = jnp.pad(conv_state, ((0, 0), (0, seq_len - 2), (0, 0))).reshape(m, f2)
        aux_spec = tail_spec = pl.BlockSpec((tm, f2), lambda i, j: (i, 0))
        tail_shape = (m, f2)
    y, tail = pl.pallas_call(
        functools.partial(_post_body, chained, seq_len),
        grid=grid,
        in_specs=[rows(a) for a in acts] + [aux_spec] + [const(a) for a in weights],
        out_specs=[rows(acts[0]), tail_spec],
        out_shape=[jax.ShapeDtypeStruct(acts[0].shape, F32), jax.ShapeDtypeStruct(tail_shape, F32)],
        scratch_shapes=[pltpu.VMEM((8, f2), F32), pltpu.VMEM((tm, f2 // 2), BF16)],
        compiler_params=pltpu.CompilerParams(dimension_semantics=("parallel", "arbitrary"),
                                             vmem_limit_bytes=VMEM_LIMIT),
        name="post_mixer",
    )(*acts, aux, *weights)
    new_conv = tail[:, -2:] if chained else tail.reshape(b, t, f2)[:, -2:]
    return y.reshape(b, t, d), new_conv


def _trunk(x, shift0, wkv0, gla0, conv0, w):
    b, t, d = x.shape
    m = b * t
    xf = x.reshape(m, d)
    p_rw, p_qkv, p_og, p_gate, p_lga = _norm_proj(xf, w["norm_mix"], w["w_in_parts"])
    shift_cols = p_rw.shape[1]
    new_shift = p_rw.reshape(b, t, shift_cols)[:, -1]

    if t > TB:
        shape3 = lambda a: a.reshape(b, t, a.shape[1])
        prev = shift0[:, None, :]
    else:
        shape3 = lambda a: a[None]
        prev = jnp.pad(shift0[:, None, :], ((0, 0), (0, t - 1), (0, 0))).reshape(m, shift_cols)
    z_a, wkv_new, z_b, gla_new = _mixers(shape3(p_rw), prev, wkv0, shape3(p_qkv), shape3(p_lga), shape3(p_og),
                                          gla0, w, t)
    z_a = z_a.reshape(m, z_a.shape[-1])
    z_b = z_b.reshape(m, z_b.shape[-1])

    y, new_conv = _post(x, z_a, z_b, p_gate, conv0, w, t)
    return (y, new_shift[None], wkv_new[None], gla_new[None], new_conv[None])


def kernel(x_prompt, x_sample, state_rwkv_shift, state_rwkv_wkv, state_gla, state_ffn_conv, norm_mix, w_in, mu_shift, rwkv_w0, rwkv_w2, rwkv_a0, rwkv_a2, rwkv_g2, rwkv_k_k, rwkv_k_a, rwkv_r_k, rwkv_ln_w, rwkv_ln_b, gla_wg2, gla_bg, gla_norm_w, w_out_a, w_out_b, w_o, norm_ffn, ffn_w_up, ffn_conv_w, ffn_conv_b, ffn_w_down, norm_final):
    assert norm_mix.shape[0] == 1, "single-layer step"
    d = x_prompt.shape[-1]
    width = rwkv_w0.shape[-1]
    shift_cols = mu_shift.shape[-1]
    kw = gla_wg2.shape[-1]
    vw = w_out_b.shape[1]
    lora_g = gla_wg2.shape[1]
    lw, la = rwkv_w2.shape[1], rwkv_a2.shape[1]
    assert lw == HEAD and la == HEAD and rwkv_g2.shape[1] == LANES

    win = w_in[0].astype(BF16)
    c0 = shift_cols
    c1 = c0 + 2 * kw + vw
    c2 = c1 + lora_g
    c3 = c2 + vw
    w_lga = jnp.pad(win[:, c1:c2], ((0, 0), (0, LANES - lora_g)))
    w_in_parts = [win[:, :c0], win[:, c0:c1], win[:, c2:c3], win[:, c3:], w_lga]

    zw = jnp.zeros((lw, width), BF16)
    wlo = jnp.concatenate([jnp.concatenate([rwkv_w2[0].astype(BF16), zw], axis=1),
                           jnp.concatenate([zw, rwkv_a2[0].astype(BF16)], axis=1)], axis=0)
    vec = jnp.stack([rwkv_w0[0], rwkv_a0[0], rwkv_k_k[0], rwkv_k_a[0], rwkv_r_k[0].reshape(width),
                     rwkv_ln_w[0], rwkv_ln_b[0], jnp.zeros((width,), F32)])
    w = dict(
        norm_mix=norm_mix, w_in_parts=w_in_parts, mu_shift=mu_shift, rwkv_vec=vec, rwkv_wlo=wlo,
        rwkv_g2=rwkv_g2[0].astype(BF16),
        gla_wg2=jnp.pad(gla_wg2[0].astype(BF16), ((0, LANES - lora_g), (0, 0))), gla_bg=gla_bg,
        gla_norm_w=gla_norm_w,
        w_out_a=w_out_a[0].astype(BF16), w_out_b=w_out_b[0].astype(BF16), w_o=w_o[0].astype(BF16),
        norm_ffn=norm_ffn, ffn_w_up=ffn_w_up[0].astype(BF16), ffn_conv_w=ffn_conv_w[0],
        ffn_conv_b=ffn_conv_b, ffn_w_down=ffn_w_down[0].astype(BF16), norm_final=norm_final[None],
    )

    bp = x_prompt.shape[0]
    dt = x_prompt.dtype
    zeros = lambda s: jnp.zeros((bp,) + s.shape[2:], dt)
    y_p, shift_p, wkv_p, gla_p, conv_p = _trunk(
        x_prompt, zeros(state_rwkv_shift), zeros(state_rwkv_wkv), zeros(state_gla), zeros(state_ffn_conv), w)
    y_s, shift_s, wkv_s, gla_s, conv_s = _trunk(
        x_sample, state_rwkv_shift[0], state_rwkv_wkv[0], state_gla[0], state_ffn_conv[0], w)
    return (y_p, y_s, shift_p, wkv_p, gla_p, conv_p, shift_s, wkv_s, gla_s, conv_s)
```

```python
import functools
import itertools
import math

import jax
import jax.numpy as jnp
from jax import lax
from jax.experimental import pallas as pl
from jax.experimental.pallas import tpu as pltpu

F32 = jnp.float32
BF16 = jnp.bfloat16
LANES = 128
BF16_ROWS = 16
MXU_DEPTH = 256
TB = 128
TM = 512
POST_TM = 512
POST_TM_SHORT = 128
NCHUNK = 512
FFN_CHUNK = 256
VMEM_LIMIT = 56 * 1024 * 1024
NORM_EPS = 1e-6
HEAD = 64
GLA_GATE_TEMP = 16.0
PROMPT_CHUNK = 32
INV_BLOCK = 16
NN = ((1,), (0,))
NT = ((1,), (1,))
TN = ((0,), (0,))


def _dot(a, b, dims=NN):
    return lax.dot_general(a, b, (dims, ((), ())), preferred_element_type=F32)


def _split(x):
    hi = x.astype(BF16)
    lo = (x - hi.astype(F32)).astype(BF16)
    return hi, lo


def _dot3(a, b, dims=NN):
    ka, kb = dims[0][0], dims[1][0]
    if dims == TN and not isinstance(a, tuple) and not isinstance(b, tuple) and a.shape[0] % BF16_ROWS:
        pad = BF16_ROWS - a.shape[0] % BF16_ROWS
        a = jnp.concatenate([a, jnp.zeros((pad, a.shape[1]), F32)], axis=0)
        b = jnp.concatenate([b, jnp.zeros((pad, b.shape[1]), F32)], axis=0)
    ah, al = a if isinstance(a, tuple) else _split(a)
    bh, bl = b if isinstance(b, tuple) else _split(b)
    k = ah.shape[ka]
    if dims == NN and k == MXU_DEPTH and 2 * bh.shape[1] <= MXU_DEPTH:
        n = bh.shape[1]
        wide = _dot(ah, jnp.concatenate([bh, bl], axis=1))
        return wide[:, :n] + (wide[:, n:] + _dot(al, bh))
    if k % BF16_ROWS or 2 * k > MXU_DEPTH:
        return _dot(ah, bh, dims) + (_dot(ah, bl, dims) + _dot(al, bh, dims))
    if 3 * k <= MXU_DEPTH:
        return _dot(jnp.concatenate([ah, al, ah], axis=ka), jnp.concatenate([bh, bh, bl], axis=kb), dims)
    n = bh.shape[1]
    if dims == NN and 2 * n <= MXU_DEPTH:
        wide = _dot(jnp.concatenate([ah, al], axis=1),
                    jnp.concatenate([jnp.concatenate([bh, bl], axis=1),
                                     jnp.concatenate([bh, jnp.zeros_like(bl)], axis=1)], axis=0))
        return wide[:, :n] + wide[:, n:]
    return _dot(jnp.concatenate([ah, al], axis=ka), jnp.concatenate([bh, bh], axis=kb), dims) + _dot(ah, bl, dims)


def _dot1(a, b, dims=NN):
    ah = a[0] if isinstance(a, tuple) else a.astype(BF16)
    bh = b[0] if isinstance(b, tuple) else b.astype(BF16)
    return _dot(ah, bh, dims)


def _split3(x):
    a1 = x.astype(BF16)
    r1 = x - a1.astype(F32)
    a2 = r1.astype(BF16)
    a3 = (r1 - a2.astype(F32)).astype(BF16)
    return a1, a2, a3


def _dot_exact_lhs(e, x):
    x1, x2, x3 = _split3(x)
    return _dot(jnp.concatenate([e, e], axis=1), jnp.concatenate([x1, x2], axis=0)) + _dot(e, x3)


def _dot_exact_rhs(x, e):
    return _dot(jnp.concatenate(_split(x), axis=1), jnp.concatenate([e, e], axis=0))


def _iota2(shape, dim):
    return lax.broadcasted_iota(jnp.int32, shape, dim)


def _same_block(n, size):
    sh = int(math.log2(size))
    return (_iota2((n, n), 0) >> sh) == (_iota2((n, n), 1) >> sh)


def _ones_where(mask):
    return jnp.where(mask, 1.0, 0.0).astype(BF16)


def _softplus(y):
    return jnp.maximum(y, 0.0) + jnp.log(1.0 + jnp.exp(-jnp.abs(y)))


def _rmsnorm(x, g):
    return x * lax.rsqrt(jnp.mean(x * x, axis=-1, keepdims=True) + NORM_EPS) * g


def _tile(x, p):
    return x[:, p * LANES:(p + 1) * LANES]


def _chunk_masks(chunk):
    same = _same_block(TB, chunk)
    r = _iota2((TB, TB), 0)
    c = _iota2((TB, TB), 1)
    return same & (c <= r), same & (c < r)


def _chunk_cumsum(x, incl, chunk):
    cum = _dot_exact_lhs(_ones_where(incl), x)
    total = jnp.concatenate([jnp.broadcast_to(cum[c + chunk - 1:c + chunk, :], (chunk, x.shape[1]))
                             for c in range(0, TB, chunk)], axis=0)
    return cum, total


def _half_masks():
    lane = _iota2((1, LANES), 1)
    lo = lane < HEAD
    return lo, jnp.logical_not(lo)


def _norm_proj_body(nw, x_ref, g_ref, *refs):
    w_refs, o_refs = refs[:nw], refs[nw:]
    hb = _rmsnorm(x_ref[...], g_ref[...]).astype(BF16)
    for w_ref, o_ref in zip(w_refs, o_refs):
        n = w_ref.shape[1]
        for n0 in range(0, n, NCHUNK):
            n1 = min(n0 + NCHUNK, n)
            o_ref[:, n0:n1] = _dot(hb, w_ref[:, n0:n1])


def _norm_proj(x, g, weights):
    m, d = x.shape
    nw = len(weights)
    return pl.pallas_call(
        functools.partial(_norm_proj_body, nw),
        grid=(m // TM,),
        in_specs=[pl.BlockSpec((TM, d), lambda i: (i, 0)), pl.BlockSpec((1, d), lambda i: (0, 0))]
        + [pl.BlockSpec(w.shape, lambda i: (0, 0), pipeline_mode=pl.Buffered(1)) for w in weights],
        out_specs=[pl.BlockSpec((TM, w.shape[1]), lambda i: (i, 0)) for w in weights],
        out_shape=[jax.ShapeDtypeStruct((m, w.shape[1]), F32) for w in weights],
        compiler_params=pltpu.CompilerParams(dimension_semantics=("parallel",), vmem_limit_bytes=VMEM_LIMIT),
        name="norm_proj",
    )(x, g, *weights)


def _wkv_pair_load(ref, idx, p):
    a, b = ref[idx + (2 * p,)], ref[idx + (2 * p + 1,)]
    z = jnp.zeros_like(a)
    return jnp.concatenate([jnp.concatenate([a, z], axis=1), jnp.concatenate([z, b], axis=1)], axis=0)


def _wkv_pair_store(ref, idx, p, s):
    n = s.shape[0] // 2
    ref[idx + (2 * p,)] = s[:n, :n]
    ref[idx + (2 * p + 1,)] = s[n:, n:]


def _rwkv_body(chunk, chained, x_ref, prev_ref, s0_ref, mu_ref, vec_ref, wlo_ref, g2_ref,
               z_ref, sout_ref, carry_ref, s_scr):
    nchunk = TB // chunk
    width = z_ref.shape[-1]
    npair = width // LANES
    x = x_ref[...]
    row = _iota2((TB, 1), 0)
    rolled = pltpu.roll(x, 1, axis=0)
    if chained:
        @pl.when(pl.program_id(1) == 0)
        def _():
            carry_ref[...] = prev_ref[...]
            for p in range(npair):
                s_scr[p] = _wkv_pair_load(s0_ref, (), p)
        prev = jnp.where(row == 0, carry_ref[...], rolled)
        carry_ref[...] = x[TB - 1:TB, :]
    else:
        prev = jnp.where((row & (chunk - 1)) == 0, prev_ref[...], rolled)
    xs = x + (prev - x) * mu_ref[...]
    yield

    r = xs[:, 0:width]
    k = xs[:, width:2 * width]
    v = xs[:, 2 * width:3 * width]
    lora = xs[:, 3 * width:3 * width + LANES]
    lg = xs[:, 3 * width + LANES:3 * width + 2 * LANES]
    lo_half, hi_half = _half_masks()
    lora = jnp.where(lo_half, jnp.tanh(lora), lora)
    wa = _dot(lora.astype(BF16), wlo_ref[...])
    w0, a0, k_k, k_a = vec_ref[0:1, :], vec_ref[1:2, :], vec_ref[2:3, :], vec_ref[3:4, :]
    r_k, ln_w, ln_b = vec_ref[4:5, :], vec_ref[5:6, :], vec_ref[6:7, :]
    wlog = -_softplus(-(w0 + wa[:, :width])) - 0.5
    logw = -jnp.exp(wlog)
    asig = jax.nn.sigmoid(a0 + wa[:, width:])
    g = _dot(jax.nn.sigmoid(lg).astype(BF16), g2_ref[...])
    yield

    seg = _ones_where(_same_block(LANES, HEAD))

    def headsum(t):
        return jnp.concatenate([_dot_exact_rhs(_tile(t, p), seg) for p in range(npair)], axis=1)

    kk = k * k_k
    kk = kk / jnp.maximum(jnp.sqrt(headsum(kk * kk)), 1e-12)
    kh = k * (1.0 + (asig - 1.0) * k_a)
    a_vec = -kk
    b_vec = kk * asig
    yield

    incl, strict = _chunk_masks(chunk)
    cum, cum_c = _chunk_cumsum(logw, incl, chunk)
    e_in = jnp.exp(cum)
    e_out = jnp.exp(-cum)
    e_end = jnp.exp(cum_c - cum)
    yield
    rt = r * e_in
    at = a_vec * jnp.exp(cum - logw)
    bt = b_vec * e_out
    kt = kh * e_out
    bp = b_vec * e_end
    kp = kh * e_end
    e_c = jnp.exp(cum_c)
    yield

    def msk(t, h):
        keep = lo_half if h == 0 else hi_half
        if isinstance(t, tuple):
            return tuple(jnp.where(keep, x, jnp.zeros_like(x)) for x in t)
        return jnp.where(keep, t, 0.0)

    def cat(parts, axis):
        return tuple(jnp.concatenate([part[i] for part in parts], axis=axis) for i in range(2))

    by_head = lambda t: cat([msk(t, 0), msk(t, 1)], 0)

    eye = jnp.where(_iota2((TB, TB), 0) == _iota2((TB, TB), 1), 1.0, 0.0)
    pairs = range(npair)
    heads = [(p, h) for p in pairs for h in range(2)]
    at_t, rt_t, bt_t, kt_t, v_t, bp_t, kp_t, ec_t = ([_tile(t, p) for p in pairs]
                                                     for t in (at, rt, bt, kt, v, bp, kp, e_c))
    at_s = [_split(t) for t in at_t]
    v_h = [by_head(_split(t)) for t in v_t]
    bk = [cat([by_head(_split(bt_t[p])), by_head(_split(kt_t[p]))], 0) for p in pairs]
    sc_a = [_dot3(at_s[p], bk[p], NT) for p in pairs]
    sc_r = [_dot1(rt_t[p], bk[p], NT) for p in pairs]
    a_ab = [jnp.where(strict, sc_a[p][:, h * TB:(h + 1) * TB], 0.0) for p, h in heads]
    a_ak = [jnp.where(strict, sc_a[p][:, (2 + h) * TB:(3 + h) * TB], 0.0) for p, h in heads]
    p_rb = [jnp.where(incl, sc_r[p][:, h * TB:(h + 1) * TB], 0.0) for p, h in heads]
    p_rk = [jnp.where(incl, sc_r[p][:, (2 + h) * TB:(3 + h) * TB], 0.0) for p, h in heads]
    yield
    base = min(chunk, INV_BLOCK)
    assert chunk in (base, 2 * base)
    in_base = _same_block(TB, base)
    n_in = [jnp.where(in_base, n, 0.0) for n in a_ab]
    t_inv = [eye + n for n in n_in]
    levels = int(math.log2(base))
    splits = lambda mats: [_split(mat) for mat in mats]
    powers = [_dot3(n_s, n_s) for n_s in splits(n_in)] if levels > 1 else n_in
    yield
    for level in range(1, levels):
        x_s, t_s = splits(powers), splits(t_inv)
        if level + 1 < levels:
            both = [_dot3(xs, cat([xs, ts], 1)) for xs, ts in zip(x_s, t_s)]
            powers = [b[:, :TB] for b in both]
            t_inv = [t + b[:, TB:] for t, b in zip(t_inv, both)]
        else:
            t_inv = [t + _dot3(xs, ts) for t, xs, ts in zip(t_inv, x_s, t_s)]
        yield
    if chunk > base:
        t_s = splits(t_inv)
        n_off = [_dot3(n - ni, ts) for n, ni, ts in zip(a_ab, n_in, t_s)]
        yield
        t_inv = [t + _dot3(ts, no) for t, ts, no in zip(t_inv, t_s, n_off)]
    yield
    av = [_dot3(jnp.concatenate(a_ak[2 * p:2 * p + 2], axis=1), v_h[p]) for p in pairs]
    av_s = splits(av)
    tay = [_dot3(jnp.concatenate(t_inv[2 * p:2 * p + 2], axis=1),
                 cat([cat([msk(at_s[p], h), msk(av_s[p], h)], 1) for h in range(2)], 0))
           for p in pairs]
    yield
    ta = [t[:, :LANES] for t in tay]
    yy = [t[:, LANES:] for t in tay]
    qz = [_dot1(jnp.concatenate(p_rb[2 * p:2 * p + 2], axis=1),
                jnp.concatenate([jnp.concatenate([msk(ta[p], h), msk(yy[p], h)], axis=1) for h in range(2)], axis=0))
          for p in pairs]
    zv = [_dot1(jnp.concatenate(p_rk[2 * p:2 * p + 2], axis=1), v_h[p]) for p in pairs]
    qq = [rt_t[p] + qz[p][:, :LANES] for p in pairs]
    zz = [qz[p][:, LANES:] + zv[p] for p in pairs]

    bd = _same_block(LANES, HEAD)
    diag = _iota2((LANES, LANES), 0) == _iota2((LANES, LANES), 1)
    rows = lambda t, c: t[c * chunk:(c + 1) * chunk]
    m_mat = [[jnp.where(diag, ec_t[p][c * chunk:c * chunk + 1, :], 0.0)
              + jnp.where(bd, _dot3(rows(ta[p], c), rows(bp_t[p], c), TN), 0.0) for p in pairs]
             for c in range(nchunk)]
    yield
    n_mat = [[jnp.where(bd, _dot3(jnp.concatenate([rows(yy[p], c), rows(v_t[p], c)], axis=0),
                                  jnp.concatenate([rows(bp_t[p], c), rows(kp_t[p], c)], axis=0), TN), 0.0)
              for p in pairs] for c in range(nchunk)]
    yield
    state = [s_scr[p] for p in pairs] if chained else None
    o_rows = [[] for _ in pairs]
    for c in range(nchunk):
        for p in pairs:
            s = _split(state[p] if chained else _wkv_pair_load(s0_ref, (c,), p))
            o_rows[p].append(_dot1(rows(qq[p], c), s, NT) + rows(zz[p], c))
            s = _dot3(s, m_mat[c][p]) + n_mat[c][p]
            if chained:
                state[p] = s
            else:
                _wkv_pair_store(sout_ref, (c,), p, s)
        if chained:
            yield
    if chained:
        for p in pairs:
            s_scr[p] = state[p]

        @pl.when(pl.program_id(1) == pl.num_programs(1) - 1)
        def _():
            for p in pairs:
                _wkv_pair_store(sout_ref, (), p, state[p])
    o = jnp.concatenate([jnp.concatenate(o_rows[p], axis=0) for p in pairs], axis=1)
    yield

    inv_n = 1.0 / HEAD
    mean = headsum(o) * inv_n
    d = o - mean
    var = headsum(d * d) * inv_n
    o = d * lax.rsqrt(var + 1e-5 * HEAD) * ln_w + ln_b
    o = o + headsum(r * kh * r_k) * v
    z_ref[...] = o * g


def _gla_pair_load(ref, idx, p):
    return jnp.concatenate([ref[idx + (2 * p,)], ref[idx + (2 * p + 1,)]], axis=0).T


def _gla_pair_store(ref, idx, p, s):
    st = s.T
    dk = st.shape[0] // 2
    ref[idx + (2 * p,)] = st[:dk]
    ref[idx + (2 * p + 1,)] = st[dk:]


def _gla_body(chunk, chained, qkv_ref, lga_ref, og_ref, s0_ref, wg2_ref, bg_ref, nw_ref,
              z_ref, sout_ref, s_scr):
    nchunk = TB // chunk
    vw = z_ref.shape[-1]
    nhead = vw // LANES
    npair = nhead // 2
    kw = npair * LANES
    qkv = qkv_ref[...]
    q = qkv[:, 0:kw] * (HEAD ** -0.5)
    k = qkv[:, kw:2 * kw]
    v = qkv[:, 2 * kw:2 * kw + vw]
    gl = _dot(lga_ref[...].astype(BF16), wg2_ref[...]) + bg_ref[...]
    log_a = -_softplus(-gl) * (1.0 / GLA_GATE_TEMP)

    incl, _ = _chunk_masks(chunk)
    cum, cum_c = _chunk_cumsum(log_a, incl, chunk)
    qt = q * jnp.exp(cum)
    kt = k * jnp.exp(-cum)
    kp = k * jnp.exp(cum_c - cum)
    e_c = jnp.exp(cum_c)
    yield
    lo_half, hi_half = _half_masks()

    def msk(t, h):
        return jnp.where(lo_half if h == 0 else hi_half, t, 0.0)

    if chained:
        @pl.when(pl.program_id(1) == 0)
        def _():
            for p in range(npair):
                s_scr[p] = _gla_pair_load(s0_ref, (), p)

    pairs = range(npair)
    heads = [(p, h) for p in pairs for h in range(2)]
    rows = lambda t, c: t[c * chunk:(c + 1) * chunk]
    kt_t, kp_t, ec_t = ([_tile(t, p) for p in pairs] for t in (kt, kp, e_c))
    v_t = [_tile(v, h) for h in range(nhead)]
    qm = [msk(_tile(qt, p), h) for p, h in heads]
    sc = [_dot1(jnp.concatenate(qm[2 * p:2 * p + 2], axis=0), kt_t[p], NT) for p in pairs]
    intra = [_dot1(jnp.where(incl, sc[p][h * TB:(h + 1) * TB], 0.0), v_t[2 * p + h]) for p, h in heads]
    upd = [[jnp.where(lo_half, _dot3(rows(v_t[2 * p], c), rows(kp_t[p], c), TN),
                      _dot3(rows(v_t[2 * p + 1], c), rows(kp_t[p], c), TN)) for p in pairs]
           for c in range(nchunk)]
    yield
    state = [s_scr[p] for p in pairs] if chained else None
    inter = [[] for _ in range(nhead)]
    for c in range(nchunk):
        for p in pairs:
            s = state[p] if chained else _gla_pair_load(s0_ref, (c,), p)
            io = _dot3(jnp.concatenate([rows(qm[2 * p], c), rows(qm[2 * p + 1], c)], axis=0), s, NT)
            inter[2 * p].append(io[:chunk])
            inter[2 * p + 1].append(io[chunk:])
            s = s * ec_t[p][c * chunk:c * chunk + 1, :] + upd[c][p]
            if chained:
                state[p] = s
            else:
                _gla_pair_store(sout_ref, (c,), p, s)
    if chained:
        for p in pairs:
            s_scr[p] = state[p]

        @pl.when(pl.program_id(1) == pl.num_programs(1) - 1)
        def _():
            for p in pairs:
                _gla_pair_store(sout_ref, (), p, state[p])
    o_heads = [intra[h] + jnp.concatenate(inter[h], axis=0) for h in range(nhead)]
    yield

    og = og_ref[...]
    for h in range(nhead):
        o = o_heads[h]
        o = o * lax.rsqrt(jnp.mean(o * o, axis=-1, keepdims=True) + NORM_EPS) * nw_ref[...]
        z_ref[:, h * LANES:(h + 1) * LANES] = o * jax.nn.silu(_tile(og, h))


def _mixers_body(chunk, chained, x_ref, prev_ref, wkv0_ref, qkv_ref, lga_ref, og_ref, gla0_ref,
                 mu_ref, vec_ref, wlo_ref, g2_ref, wg2_ref, bg_ref, nw_ref,
                 za_ref, wkv_ref, zb_ref, gla_ref, carry_ref, wkv_scr, gla_scr):
    rwkv = _rwkv_body(chunk, chained, x_ref, prev_ref, wkv0_ref, mu_ref, vec_ref, wlo_ref, g2_ref, za_ref, wkv_ref,
                      carry_ref, wkv_scr)
    gla = _gla_body(chunk, chained, qkv_ref, lga_ref, og_ref, gla0_ref, wg2_ref, bg_ref, nw_ref, zb_ref, gla_ref,
                    gla_scr)
    for _ in itertools.zip_longest(rwkv, gla):
        pass


def _mixers(x_rw, prev, wkv0, qkv, lga, og, gla0, w, seq_len):
    width = wkv0.shape[1] * wkv0.shape[2]
    vw = og.shape[-1]
    cols = x_rw.shape[-1]
    chained = seq_len > TB
    if chained:
        chunk = PROMPT_CHUNK
        b, t, _ = x_rw.shape
        grid = (b, t // TB)
        rows = lambda n: pl.BlockSpec((None, TB, n), lambda i, j: (i, j, 0))
        prev_spec = pl.BlockSpec((None, 1, cols), lambda i, j: (i, 0, 0))
        st = lambda a: pl.BlockSpec((None,) + a.shape[1:], lambda i, j: (i, 0, 0, 0))
        lead = (b, t)
    else:
        chunk = seq_len
        m = x_rw.shape[1]
        assert m % TB == 0
        grid = (m // TB, 1)
        rows = lambda n: pl.BlockSpec((None, TB, n), lambda i, j: (0, i, 0))
        prev_spec = pl.BlockSpec((TB, cols), lambda i, j: (i, 0))
        st = lambda a: pl.BlockSpec((TB // chunk,) + a.shape[1:], lambda i, j: (i, 0, 0, 0))
        lead = (1, m)
    const = lambda a: pl.BlockSpec(a.shape, lambda i, j: (0,) * a.ndim)
    consts = [w["mu_shift"], w["rwkv_vec"], w["rwkv_wlo"], w["rwkv_g2"], w["gla_wg2"], w["gla_bg"], w["gla_norm_w"]]
    return pl.pallas_call(
        functools.partial(_mixers_body, chunk, chained),
        grid=grid,
        in_specs=[rows(cols), prev_spec, st(wkv0), rows(qkv.shape[-1]), rows(lga.shape[-1]), rows(vw), st(gla0)]
        + [const(a) for a in consts],
        out_specs=[rows(width), st(wkv0), rows(vw), st(gla0)],
        out_shape=[jax.ShapeDtypeStruct(lead + (width,), F32), jax.ShapeDtypeStruct(wkv0.shape, F32),
                   jax.ShapeDtypeStruct(lead + (vw,), F32), jax.ShapeDtypeStruct(gla0.shape, F32)],
        scratch_shapes=[pltpu.VMEM((1, cols), F32), pltpu.VMEM((wkv0.shape[1] // 2, LANES, LANES), F32),
                        pltpu.VMEM((gla0.shape[1] // 2, LANES, LANES), F32)],
        compiler_params=pltpu.CompilerParams(dimension_semantics=("parallel", "arbitrary"),
                                             vmem_limit_bytes=VMEM_LIMIT),
        name="mixers",
    )(x_rw, prev, wkv0, qkv, lga, og, gla0, *consts)


def _post_body(chained, seq_len, x_ref, za_ref, zb_ref, gate_ref, aux_ref, woa_ref, wob_ref, wo_ref, g_ref,
               wup_ref, cw_ref, cb_ref, wd_ref, nf_ref, y_ref, tail_ref, carry_ref, act_ref):
    tm, d = x_ref.shape
    hidden = wd_ref.shape[0]
    ya = _dot(za_ref[...].astype(BF16), woa_ref[...])
    yb = _dot(zb_ref[...].astype(BF16), wob_ref[...])
    gate = gate_ref[...]
    merged = jax.nn.sigmoid(gate[:, :d]) * ya + jax.nn.sigmoid(gate[:, d:]) * yb
    x1 = x_ref[...] + _dot(merged.astype(BF16), wo_ref[...])
    hb = _rmsnorm(x1, g_ref[...]).astype(BF16)

    if chained:
        @pl.when(pl.program_id(1) == 0)
        def _():
            carry_ref[0:6, :] = jnp.zeros((6, carry_ref.shape[1]), F32)
            carry_ref[6:8, :] = aux_ref[...]
    else:
        t = _iota2((tm, 1), 0) & (seq_len - 1)

    def conv(u, cs):
        if chained:
            ext = jnp.concatenate([carry_ref[:, cs], u], axis=0)
            carry_ref[:, cs] = u[tm - 8:, :]
            tail_ref[:, cs] = u[tm - 8:, :]
            inner = cw_ref[1:2, cs] * ext + pltpu.roll(cw_ref[0:1, cs] * ext, 1, axis=0)
            return (cb_ref[:, cs] + cw_ref[2:3, cs] * ext + pltpu.roll(inner, 1, axis=0))[8:]
        else:
            pa = aux_ref[:, cs]
            p1 = jnp.where(t == 0, pltpu.roll(pa, tm - 1, axis=0), pltpu.roll(u, 1, axis=0))
            p2 = jnp.where(t < 2, pa, pltpu.roll(u, 2, axis=0))
            tail_ref[:, cs] = u
        return cb_ref[:, cs] + cw_ref[0:1, cs] * p2 + cw_ref[1:2, cs] * p1 + cw_ref[2:3, cs] * u

    def up(j0):
        cols = (slice(j0, j0 + FFN_CHUNK), slice(hidden + j0, hidden + j0 + FFN_CHUNK))
        return [(_dot(hb, wup_ref[:, cs]), cs) for cs in cols]

    starts = list(range(0, hidden, FFN_CHUNK))
    half = starts[len(starts) // 2] + FFN_CHUNK
    out = x1
    nxt = up(starts[0])
    for i, j0 in enumerate(starts):
        cur = nxt
        if i + 1 < len(starts):
            nxt = up(starts[i + 1])
        val, gat = (conv(u, cs) for u, cs in cur)
        act_ref[:, j0:j0 + FFN_CHUNK] = (jax.nn.gelu(gat) * val).astype(BF16)
        if j0 + FFN_CHUNK == half:
            out = out + _dot(act_ref[:, :half], wd_ref[:half, :])
    y_ref[...] = _rmsnorm(out + _dot(act_ref[:, half:], wd_ref[half:, :]), nf_ref[...])


def _post(x, za, zb, gate, conv_state, w, seq_len):
    b, t, d = x.shape
    m = b * t
    f2 = conv_state.shape[-1]
    chained = seq_len >= POST_TM
    assert seq_len % POST_TM == 0 if chained else POST_TM_SHORT % seq_len == 0
    weights = [w["w_out_a"], w["w_out_b"], w["w_o"], w["norm_ffn"], w["ffn_w_up"], w["ffn_conv_w"],
               w["ffn_conv_b"], w["ffn_w_down"], w["norm_final"]]
    const = lambda a: pl.BlockSpec(a.shape, lambda i, j: (0,) * a.ndim, pipeline_mode=pl.Buffered(1))
    if chained:
        tm = POST_TM
        grid = (b, t // tm)
        rows = lambda a: pl.BlockSpec((None, tm, a.shape[-1]), lambda i, j: (i, j, 0))
        acts = [x, za.reshape(b, t, -1), zb.reshape(b, t, -1), gate.reshape(b, t, -1)]
        aux, aux_spec = conv_state, pl.BlockSpec((None, 2, f2), lambda i, j: (i, 0, 0))
        tail_spec = pl.BlockSpec((None, 8, f2), lambda i, j: (i, 0, 0))
        tail_shape = (b, 8, f2)
    else:
        tm = POST_TM_SHORT
        grid = (m // tm, 1)
        rows = lambda a: pl.BlockSpec((tm, a.shape[-1]), lambda i, j: (i, 0))
        acts = [x.reshape(m, d), za, zb, gate]
        aux = jnp.pad(conv_state, ((0, 0), (0, seq_len - 2), (0, 0))).reshape(m, f2)
        aux_spec = tail_spec = pl.BlockSpec((tm, f2), lambda i, j: (i, 0))
        tail_shape = (m, f2)
    y, tail = pl.pallas_call(
        functools.partial(_post_body, chained, seq_len),
        grid=grid,
        in_specs=[rows(a) for a in acts] + [aux_spec] + [const(a) for a in weights],
        out_specs=[rows(acts[0]), tail_spec],
        out_shape=[jax.ShapeDtypeStruct(acts[0].shape, F32), jax.ShapeDtypeStruct(tail_shape, F32)],
        scratch_shapes=[pltpu.VMEM((8, f2), F32), pltpu.VMEM((tm, f2 // 2), BF16)],
        compiler_params=pltpu.CompilerParams(dimension_semantics=("parallel", "arbitrary"),
                                             vmem_limit_bytes=VMEM_LIMIT),
        name="post_mixer",
    )(*acts, aux, *weights)
    new_conv = tail[:, -2:] if chained else tail.reshape(b, t, f2)[:, -2:]
    return y.reshape(b, t, d), new_conv


def _trunk(x, shift0, wkv0, gla0, conv0, w):
    b, t, d = x.shape
    m = b * t
    xf = x.reshape(m, d)
    p_rw, p_qkv, p_og, p_gate, p_lga = _norm_proj(xf, w["norm_mix"], w["w_in_parts"])
    shift_cols = p_rw.shape[1]
    new_shift = p_rw.reshape(b, t, shift_cols)[:, -1]

    if t > TB:
        shape3 = lambda a: a.reshape(b, t, a.shape[1])
        prev = shift0[:, None, :]
    else:
        shape3 = lambda a: a[None]
        prev = jnp.pad(shift0[:, None, :], ((0, 0), (0, t - 1), (0, 0))).reshape(m, shift_cols)
    z_a, wkv_new, z_b, gla_new = _mixers(shape3(p_rw), prev, wkv0, shape3(p_qkv), shape3(p_lga), shape3(p_og),
                                          gla0, w, t)
    z_a = z_a.reshape(m, z_a.shape[-1])
    z_b = z_b.reshape(m, z_b.shape[-1])

    y, new_conv = _post(x, z_a, z_b, p_gate, conv0, w, t)
    return (y, new_shift[None], wkv_new[None], gla_new[None], new_conv[None])


def kernel(x_prompt, x_sample, state_rwkv_shift, state_rwkv_wkv, state_gla, state_ffn_conv, norm_mix, w_in, mu_shift, rwkv_w0, rwkv_w2, rwkv_a0, rwkv_a2, rwkv_g2, rwkv_k_k, rwkv_k_a, rwkv_r_k, rwkv_ln_w, rwkv_ln_b, gla_wg2, gla_bg, gla_norm_w, w_out_a, w_out_b, w_o, norm_ffn, ffn_w_up, ffn_conv_w, ffn_conv_b, ffn_w_down, norm_final):
    assert norm_mix.shape[0] == 1, "single-layer step"
    d = x_prompt.shape[-1]
    width = rwkv_w0.shape[-1]
    shift_cols = mu_shift.shape[-1]
    kw = gla_wg2.shape[-1]
    vw = w_out_b.shape[1]
    lora_g = gla_wg2.shape[1]
    lw, la = rwkv_w2.shape[1], rwkv_a2.shape[1]
    assert lw == HEAD and la == HEAD and rwkv_g2.shape[1] == LANES

    win = w_in[0].astype(BF16)
    c0 = shift_cols
    c1 = c0 + 2 * kw + vw
    c2 = c1 + lora_g
    c3 = c2 + vw
    w_lga = jnp.pad(win[:, c1:c2], ((0, 0), (0, LANES - lora_g)))
    w_in_parts = [win[:, :c0], win[:, c0:c1], win[:, c2:c3], win[:, c3:], w_lga]

    zw = jnp.zeros((lw, width), BF16)
    wlo = jnp.concatenate([jnp.concatenate([rwkv_w2[0].astype(BF16), zw], axis=1),
                           jnp.concatenate([zw, rwkv_a2[0].astype(BF16)], axis=1)], axis=0)
    vec = jnp.stack([rwkv_w0[0], rwkv_a0[0], rwkv_k_k[0], rwkv_k_a[0], rwkv_r_k[0].reshape(width),
                     rwkv_ln_w[0], rwkv_ln_b[0], jnp.zeros((width,), F32)])
    w = dict(
        norm_mix=norm_mix, w_in_parts=w_in_parts, mu_shift=mu_shift, rwkv_vec=vec, rwkv_wlo=wlo,
        rwkv_g2=rwkv_g2[0].astype(BF16),
        gla_wg2=jnp.pad(gla_wg2[0].astype(BF16), ((0, LANES - lora_g), (0, 0))), gla_bg=gla_bg,
        gla_norm_w=gla_norm_w,
        w_out_a=w_out_a[0].astype(BF16), w_out_b=w_out_b[0].astype(BF16), w_o=w_o[0].astype(BF16),
        norm_ffn=norm_ffn, ffn_w_up=ffn_w_up[0].astype(BF16), ffn_conv_w=ffn_conv_w[0],
        ffn_conv_b=ffn_conv_b, ffn_w_down=ffn_w_down[0].astype(BF16), norm_final=norm_final[None],
    )

    bp = x_prompt.shape[0]
    dt = x_prompt.dtype
    zeros = lambda s: jnp.zeros((bp,) + s.shape[2:], dt)
    y_p, shift_p, wkv_p, gla_p, conv_p = _trunk(
        x_prompt, zeros(state_rwkv_shift), zeros(state_rwkv_wkv), zeros(state_gla), zeros(state_ffn_conv), w)
    y_s, shift_s, wkv_s, gla_s, conv_s = _trunk(
        x_sample, state_rwkv_shift[0], state_rwkv_wkv[0], state_gla[0], state_ffn_conv[0], w)
    return (y_p, y_s, shift_p, wkv_p, gla_p, conv_p, shift_s, wkv_s, gla_s, conv_s)
```

```python
import functools
import itertools
import math

import jax
import jax.numpy as jnp
from jax import lax
from jax.experimental import pallas as pl
from jax.experimental.pallas import tpu as pltpu

F32 = jnp.float32
BF16 = jnp.bfloat16
LANES = 128
BF16_ROWS = 16
MXU_DEPTH = 256
TB = 128
TM = 512
POST_TM = 512
POST_TM_SHORT = 128
NCHUNK = 512
FFN_CHUNK = 256
VMEM_LIMIT = 56 * 1024 * 1024
NORM_EPS = 1e-6
HEAD = 64
GLA_GATE_TEMP = 16.0
PROMPT_CHUNK = 32
INV_BLOCK = 16
NN = ((1,), (0,))
NT = ((1,), (1,))
TN = ((0,), (0,))


def _dot(a, b, dims=NN):
    return lax.dot_general(a, b, (dims, ((), ())), preferred_element_type=F32)


def _split(x):
    hi = x.astype(BF16)
    lo = (x - hi.astype(F32)).astype(BF16)
    return hi, lo


def _dot3(a, b, dims=NN):
    ka, kb = dims[0][0], dims[1][0]
    if dims == TN and not isinstance(a, tuple) and not isinstance(b, tuple) and a.shape[0] % BF16_ROWS:
        pad = BF16_ROWS - a.shape[0] % BF16_ROWS
        a = jnp.concatenate([a, jnp.zeros((pad, a.shape[1]), F32)], axis=0)
        b = jnp.concatenate([b, jnp.zeros((pad, b.shape[1]), F32)], axis=0)
    ah, al = a if isinstance(a, tuple) else _split(a)
    bh, bl = b if isinstance(b, tuple) else _split(b)
    k = ah.shape[ka]
    if dims == NN and k == MXU_DEPTH and 2 * bh.shape[1] <= MXU_DEPTH:
        n = bh.shape[1]
        wide = _dot(ah, jnp.concatenate([bh, bl], axis=1))
        return wide[:, :n] + (wide[:, n:] + _dot(al, bh))
    if k % BF16_ROWS or 2 * k > MXU_DEPTH:
        return _dot(ah, bh, dims) + (_dot(ah, bl, dims) + _dot(al, bh, dims))
    if 3 * k <= MXU_DEPTH:
        return _dot(jnp.concatenate([ah, al, ah], axis=ka), jnp.concatenate([bh, bh, bl], axis=kb), dims)
    n = bh.shape[1]
    if dims == NN and 2 * n <= MXU_DEPTH:
        wide = _dot(jnp.concatenate([ah, al], axis=1),
                    jnp.concatenate([jnp.concatenate([bh, bl], axis=1),
                                     jnp.concatenate([bh, jnp.zeros_like(bl)], axis=1)], axis=0))
        return wide[:, :n] + wide[:, n:]
    return _dot(jnp.concatenate([ah, al], axis=ka), jnp.concatenate([bh, bh], axis=kb), dims) + _dot(ah, bl, dims)


def _dot1(a, b, dims=NN):
    ah = a[0] if isinstance(a, tuple) else a.astype(BF16)
    bh = b[0] if isinstance(b, tuple) else b.astype(BF16)
    return _dot(ah, bh, dims)


def _split3(x):
    a1 = x.astype(BF16)
    r1 = x - a1.astype(F32)
    a2 = r1.astype(BF16)
    a3 = (r1 - a2.astype(F32)).astype(BF16)
    return a1, a2, a3


def _dot_exact_lhs(e, x):
    x1, x2, x3 = _split3(x)
    return _dot(jnp.concatenate([e, e], axis=1), jnp.concatenate([x1, x2], axis=0)) + _dot(e, x3)


def _dot_exact_rhs(x, e):
    return _dot(jnp.concatenate(_split(x), axis=1), jnp.concatenate([e, e], axis=0))


def _iota2(shape, dim):
    return lax.broadcasted_iota(jnp.int32, shape, dim)


def _same_block(n, size):
    sh = int(math.log2(size))
    return (_iota2((n, n), 0) >> sh) == (_iota2((n, n), 1) >> sh)


def _ones_where(mask):
    return jnp.where(mask, 1.0, 0.0).astype(BF16)


def _softplus(y):
    return jnp.maximum(y, 0.0) + jnp.log(1.0 + jnp.exp(-jnp.abs(y)))


def _rmsnorm(x, g):
    return x * lax.rsqrt(jnp.mean(x * x, axis=-1, keepdims=True) + NORM_EPS) * g


def _tile(x, p):
    return x[:, p * LANES:(p + 1) * LANES]


def _chunk_masks(chunk):
    same = _same_block(TB, chunk)
    r = _iota2((TB, TB), 0)
    c = _iota2((TB, TB), 1)
    return same & (c <= r), same & (c < r)


def _chunk_cumsum(x, incl, chunk):
    cum = _dot_exact_lhs(_ones_where(incl), x)
    total = jnp.concatenate([jnp.broadcast_to(cum[c + chunk - 1:c + chunk, :], (chunk, x.shape[1]))
                             for c in range(0, TB, chunk)], axis=0)
    return cum, total


def _half_masks():
    lane = _iota2((1, LANES), 1)
    lo = lane < HEAD
    return lo, jnp.logical_not(lo)


def _norm_proj_body(nw, x_ref, g_ref, *refs):
    w_refs, o_refs = refs[:nw], refs[nw:]
    hb = _rmsnorm(x_ref[...], g_ref[...]).astype(BF16)
    for w_ref, o_ref in zip(w_refs, o_refs):
        n = w_ref.shape[1]
        for n0 in range(0, n, NCHUNK):
            n1 = min(n0 + NCHUNK, n)
            o_ref[:, n0:n1] = _dot(hb, w_ref[:, n0:n1])


def _norm_proj(x, g, weights):
    m, d = x.shape
    nw = len(weights)
    return pl.pallas_call(
        functools.partial(_norm_proj_body, nw),
        grid=(m // TM,),
        in_specs=[pl.BlockSpec((TM, d), lambda i: (i, 0)), pl.BlockSpec((1, d), lambda i: (0, 0))]
        + [pl.BlockSpec(w.shape, lambda i: (0, 0), pipeline_mode=pl.Buffered(1)) for w in weights],
        out_specs=[pl.BlockSpec((TM, w.shape[1]), lambda i: (i, 0)) for w in weights],
        out_shape=[jax.ShapeDtypeStruct((m, w.shape[1]), F32) for w in weights],
        compiler_params=pltpu.CompilerParams(dimension_semantics=("parallel",), vmem_limit_bytes=VMEM_LIMIT),
        name="norm_proj",
    )(x, g, *weights)


def _wkv_pair_load(ref, idx, p):
    a, b = ref[idx + (2 * p,)], ref[idx + (2 * p + 1,)]
    z = jnp.zeros_like(a)
    return jnp.concatenate([jnp.concatenate([a, z], axis=1), jnp.concatenate([z, b], axis=1)], axis=0)


def _wkv_pair_store(ref, idx, p, s):
    n = s.shape[0] // 2
    ref[idx + (2 * p,)] = s[:n, :n]
    ref[idx + (2 * p + 1,)] = s[n:, n:]


def _rwkv_body(chunk, chained, x_ref, prev_ref, s0_ref, mu_ref, vec_ref, wlo_ref, g2_ref,
               z_ref, sout_ref, carry_ref, s_scr):
    nchunk = TB // chunk
    width = z_ref.shape[-1]
    npair = width // LANES
    x = x_ref[...]
    row = _iota2((TB, 1), 0)
    rolled = pltpu.roll(x, 1, axis=0)
    if chained:
        @pl.when(pl.program_id(1) == 0)
        def _():
            carry_ref[...] = prev_ref[...]
            for p in range(npair):
                s_scr[p] = _wkv_pair_load(s0_ref, (), p)
        prev = jnp.where(row == 0, carry_ref[...], rolled)
        carry_ref[...] = x[TB - 1:TB, :]
    else:
        first = _iota2((1, chunk, 1), 1) == 0
        prev = jnp.where(first, prev_ref[...], rolled.reshape(nchunk, chunk, -1)).reshape(TB, -1)
    xs = x + (prev - x) * mu_ref[...]
    yield

    r = xs[:, 0:width]
    k = xs[:, width:2 * width]
    v = xs[:, 2 * width:3 * width]
    lora = xs[:, 3 * width:3 * width + LANES]
    lg = xs[:, 3 * width + LANES:3 * width + 2 * LANES]
    lo_half, hi_half = _half_masks()
    lora = jnp.where(lo_half, jnp.tanh(lora), lora)
    wa = _dot(lora.astype(BF16), wlo_ref[...])
    w0, a0, k_k, k_a = vec_ref[0:1, :], vec_ref[1:2, :], vec_ref[2:3, :], vec_ref[3:4, :]
    r_k, ln_w, ln_b = vec_ref[4:5, :], vec_ref[5:6, :], vec_ref[6:7, :]
    wlog = -_softplus(-(w0 + wa[:, :width])) - 0.5
    logw = -jnp.exp(wlog)
    asig = jax.nn.sigmoid(a0 + wa[:, width:])
    g = _dot(jax.nn.sigmoid(lg).astype(BF16), g2_ref[...])
    yield

    seg = _ones_where(_same_block(LANES, HEAD))

    def headsum(t):
        return jnp.concatenate([_dot_exact_rhs(_tile(t, p), seg) for p in range(npair)], axis=1)

    kk = k * k_k
    kk = kk / jnp.maximum(jnp.sqrt(headsum(kk * kk)), 1e-12)
    kh = k * (1.0 + (asig - 1.0) * k_a)
    a_vec = -kk
    b_vec = kk * asig
    yield

    incl, strict = _chunk_masks(chunk)
    cum, cum_c = _chunk_cumsum(logw, incl, chunk)
    e_in = jnp.exp(cum)
    e_out = jnp.exp(-cum)
    e_end = jnp.exp(cum_c - cum)
    yield
    rt = r * e_in
    at = a_vec * jnp.exp(cum - logw)
    bt = b_vec * e_out
    kt = kh * e_out
    bp = b_vec * e_end
    kp = kh * e_end
    e_c = jnp.exp(cum_c)
    yield

    def msk(t, h):
        keep = lo_half if h == 0 else hi_half
        if isinstance(t, tuple):
            return tuple(jnp.where(keep, x, jnp.zeros_like(x)) for x in t)
        return jnp.where(keep, t, 0.0)

    def cat(parts, axis):
        return tuple(jnp.concatenate([part[i] for part in parts], axis=axis) for i in range(2))

    by_head = lambda t: cat([msk(t, 0), msk(t, 1)], 0)

    eye = jnp.where(_iota2((TB, TB), 0) == _iota2((TB, TB), 1), 1.0, 0.0)
    pairs = range(npair)
    heads = [(p, h) for p in pairs for h in range(2)]
    at_t, rt_t, bt_t, kt_t, v_t, bp_t, kp_t, ec_t = ([_tile(t, p) for p in pairs]
                                                     for t in (at, rt, bt, kt, v, bp, kp, e_c))
    at_s = [_split(t) for t in at_t]
    v_h = [by_head(_split(t)) for t in v_t]
    bk = [cat([by_head(_split(bt_t[p])), by_head(_split(kt_t[p]))], 0) for p in pairs]
    sc_a = [_dot3(at_s[p], bk[p], NT) for p in pairs]
    sc_r = [_dot1(rt_t[p], bk[p], NT) for p in pairs]
    a_ab = [jnp.where(strict, sc_a[p][:, h * TB:(h + 1) * TB], 0.0) for p, h in heads]
    a_ak = [jnp.where(strict, sc_a[p][:, (2 + h) * TB:(3 + h) * TB], 0.0) for p, h in heads]
    p_rb = [jnp.where(incl, sc_r[p][:, h * TB:(h + 1) * TB], 0.0) for p, h in heads]
    p_rk = [jnp.where(incl, sc_r[p][:, (2 + h) * TB:(3 + h) * TB], 0.0) for p, h in heads]
    yield
    base = min(chunk, INV_BLOCK)
    assert chunk in (base, 2 * base)
    in_base = _same_block(TB, base)
    n_in = [jnp.where(in_base, n, 0.0) for n in a_ab]
    t_inv = [eye + n for n in n_in]
    levels = int(math.log2(base))
    splits = lambda mats: [_split(mat) for mat in mats]
    powers = [_dot3(n_s, n_s) for n_s in splits(n_in)] if levels > 1 else n_in
    yield
    for level in range(1, levels):
        x_s, t_s = splits(powers), splits(t_inv)
        if level + 1 < levels:
            both = [_dot3(xs, cat([xs, ts], 1)) for xs, ts in zip(x_s, t_s)]
            powers = [b[:, :TB] for b in both]
            t_inv = [t + b[:, TB:] for t, b in zip(t_inv, both)]
        else:
            t_inv = [t + _dot3(xs, ts) for t, xs, ts in zip(t_inv, x_s, t_s)]
        yield
    if chunk > base:
        t_s = splits(t_inv)
        n_off = [_dot3(n - ni, ts) for n, ni, ts in zip(a_ab, n_in, t_s)]
        yield
        t_inv = [t + _dot3(ts, no) for t, ts, no in zip(t_inv, t_s, n_off)]
    yield
    av = [_dot3(jnp.concatenate(a_ak[2 * p:2 * p + 2], axis=1), v_h[p]) for p in pairs]
    av_s = splits(av)
    tay = [_dot3(jnp.concatenate(t_inv[2 * p:2 * p + 2], axis=1),
                 cat([cat([msk(at_s[p], h), msk(av_s[p], h)], 1) for h in range(2)], 0))
           for p in pairs]
    yield
    ta = [t[:, :LANES] for t in tay]
    yy = [t[:, LANES:] for t in tay]
    qz = [_dot1(jnp.concatenate(p_rb[2 * p:2 * p + 2], axis=1),
                jnp.concatenate([jnp.concatenate([msk(ta[p], h), msk(yy[p], h)], axis=1) for h in range(2)], axis=0))
          for p in pairs]
    zv = [_dot1(jnp.concatenate(p_rk[2 * p:2 * p + 2], axis=1), v_h[p]) for p in pairs]
    qq = [rt_t[p] + qz[p][:, :LANES] for p in pairs]
    zz = [qz[p][:, LANES:] + zv[p] for p in pairs]

    bd = _same_block(LANES, HEAD)
    diag = _iota2((LANES, LANES), 0) == _iota2((LANES, LANES), 1)
    rows = lambda t, c: t[c * chunk:(c + 1) * chunk]
    m_mat = [[jnp.where(diag, ec_t[p][c * chunk:c * chunk + 1, :], 0.0)
              + jnp.where(bd, _dot3(rows(ta[p], c), rows(bp_t[p], c), TN), 0.0) for p in pairs]
             for c in range(nchunk)]
    yield
    n_mat = [[jnp.where(bd, _dot3(jnp.concatenate([rows(yy[p], c), rows(v_t[p], c)], axis=0),
                                  jnp.concatenate([rows(bp_t[p], c), rows(kp_t[p], c)], axis=0), TN), 0.0)
              for p in pairs] for c in range(nchunk)]
    yield
    state = [s_scr[p] for p in pairs] if chained else None
    o_rows = [[] for _ in pairs]
    for c in range(nchunk):
        for p in pairs:
            s = _split(state[p] if chained else _wkv_pair_load(s0_ref, (c,), p))
            o_rows[p].append(_dot1(rows(qq[p], c), s, NT) + rows(zz[p], c))
            s = _dot3(s, m_mat[c][p]) + n_mat[c][p]
            if chained:
                state[p] = s
            else:
                _wkv_pair_store(sout_ref, (c,), p, s)
        if chained:
            yield
    if chained:
        for p in pairs:
            s_scr[p] = state[p]

        @pl.when(pl.program_id(1) == pl.num_programs(1) - 1)
        def _():
            for p in pairs:
                _wkv_pair_store(sout_ref, (), p, state[p])
    o = jnp.concatenate([jnp.concatenate(o_rows[p], axis=0) for p in pairs], axis=1)
    yield

    inv_n = 1.0 / HEAD
    mean = headsum(o) * inv_n
    d = o - mean
    var = headsum(d * d) * inv_n
    o = d * lax.rsqrt(var + 1e-5 * HEAD) * ln_w + ln_b
    o = o + headsum(r * kh * r_k) * v
    z_ref[...] = o * g


def _gla_pair_load(ref, idx, p):
    return jnp.concatenate([ref[idx + (2 * p,)], ref[idx + (2 * p + 1,)]], axis=0).T


def _gla_pair_store(ref, idx, p, s):
    st = s.T
    dk = st.shape[0] // 2
    ref[idx + (2 * p,)] = st[:dk]
    ref[idx + (2 * p + 1,)] = st[dk:]


def _gla_body(chunk, chained, qkv_ref, lga_ref, og_ref, s0_ref, wg2_ref, bg_ref, nw_ref,
              z_ref, sout_ref, s_scr):
    nchunk = TB // chunk
    vw = z_ref.shape[-1]
    nhead = vw // LANES
    npair = nhead // 2
    kw = npair * LANES
    qkv = qkv_ref[...]
    q = qkv[:, 0:kw] * (HEAD ** -0.5)
    k = qkv[:, kw:2 * kw]
    v = qkv[:, 2 * kw:2 * kw + vw]
    gl = _dot(lga_ref[...].astype(BF16), wg2_ref[...]) + bg_ref[...]
    log_a = -_softplus(-gl) * (1.0 / GLA_GATE_TEMP)

    incl, _ = _chunk_masks(chunk)
    cum, cum_c = _chunk_cumsum(log_a, incl, chunk)
    qt = q * jnp.exp(cum)
    kt = k * jnp.exp(-cum)
    kp = k * jnp.exp(cum_c - cum)
    e_c = jnp.exp(cum_c)
    yield
    lo_half, hi_half = _half_masks()

    def msk(t, h):
        return jnp.where(lo_half if h == 0 else hi_half, t, 0.0)

    if chained:
        @pl.when(pl.program_id(1) == 0)
        def _():
            for p in range(npair):
                s_scr[p] = _gla_pair_load(s0_ref, (), p)

    pairs = range(npair)
    heads = [(p, h) for p in pairs for h in range(2)]
    rows = lambda t, c: t[c * chunk:(c + 1) * chunk]
    kt_t, kp_t, ec_t = ([_tile(t, p) for p in pairs] for t in (kt, kp, e_c))
    v_t = [_tile(v, h) for h in range(nhead)]
    qm = [msk(_tile(qt, p), h) for p, h in heads]
    sc = [_dot1(jnp.concatenate(qm[2 * p:2 * p + 2], axis=0), kt_t[p], NT) for p in pairs]
    intra = [_dot1(jnp.where(incl, sc[p][h * TB:(h + 1) * TB], 0.0), v_t[2 * p + h]) for p, h in heads]
    upd = [[jnp.where(lo_half, _dot3(rows(v_t[2 * p], c), rows(kp_t[p], c), TN),
                      _dot3(rows(v_t[2 * p + 1], c), rows(kp_t[p], c), TN)) for p in pairs]
           for c in range(nchunk)]
    yield
    state = [s_scr[p] for p in pairs] if chained else None
    inter = [[] for _ in range(nhead)]
    for c in range(nchunk):
        for p in pairs:
            s = state[p] if chained else _gla_pair_load(s0_ref, (c,), p)
            io = _dot3(jnp.concatenate([rows(qm[2 * p], c), rows(qm[2 * p + 1], c)], axis=0), s, NT)
            inter[2 * p].append(io[:chunk])
            inter[2 * p + 1].append(io[chunk:])
            s = s * ec_t[p][c * chunk:c * chunk + 1, :] + upd[c][p]
            if chained:
                state[p] = s
            else:
                _gla_pair_store(sout_ref, (c,), p, s)
    if chained:
        for p in pairs:
            s_scr[p] = state[p]

        @pl.when(pl.program_id(1) == pl.num_programs(1) - 1)
        def _():
            for p in pairs:
                _gla_pair_store(sout_ref, (), p, state[p])
    o_heads = [intra[h] + jnp.concatenate(inter[h], axis=0) for h in range(nhead)]
    yield

    og = og_ref[...]
    for h in range(nhead):
        o = o_heads[h]
        o = o * lax.rsqrt(jnp.mean(o * o, axis=-1, keepdims=True) + NORM_EPS) * nw_ref[...]
        z_ref[:, h * LANES:(h + 1) * LANES] = o * jax.nn.silu(_tile(og, h))


def _mixers_body(chunk, chained, x_ref, prev_ref, wkv0_ref, qkv_ref, lga_ref, og_ref, gla0_ref,
                 mu_ref, vec_ref, wlo_ref, g2_ref, wg2_ref, bg_ref, nw_ref,
                 za_ref, wkv_ref, zb_ref, gla_ref, carry_ref, wkv_scr, gla_scr):
    rwkv = _rwkv_body(chunk, chained, x_ref, prev_ref, wkv0_ref, mu_ref, vec_ref, wlo_ref, g2_ref, za_ref, wkv_ref,
                      carry_ref, wkv_scr)
    gla = _gla_body(chunk, chained, qkv_ref, lga_ref, og_ref, gla0_ref, wg2_ref, bg_ref, nw_ref, zb_ref, gla_ref,
                    gla_scr)
    for _ in itertools.zip_longest(rwkv, gla):
        pass


def _mixers(x_rw, prev, wkv0, qkv, lga, og, gla0, w, seq_len):
    width = wkv0.shape[1] * wkv0.shape[2]
    vw = og.shape[-1]
    cols = x_rw.shape[-1]
    chained = seq_len > TB
    if chained:
        chunk = PROMPT_CHUNK
        b, t, _ = x_rw.shape
        grid = (b, t // TB)
        rows = lambda n: pl.BlockSpec((None, TB, n), lambda i, j: (i, j, 0))
        prev_spec = pl.BlockSpec((None, 1, cols), lambda i, j: (i, 0, 0))
        st = lambda a: pl.BlockSpec((None,) + a.shape[1:], lambda i, j: (i, 0, 0, 0))
        lead = (b, t)
    else:
        chunk = seq_len
        m = x_rw.shape[1]
        assert m % TB == 0
        grid = (m // TB, 1)
        rows = lambda n: pl.BlockSpec((None, TB, n), lambda i, j: (0, i, 0))
        prev_spec = pl.BlockSpec((TB // chunk, 1, cols), lambda i, j: (i, 0, 0))
        st = lambda a: pl.BlockSpec((TB // chunk,) + a.shape[1:], lambda i, j: (i, 0, 0, 0))
        lead = (1, m)
    const = lambda a: pl.BlockSpec(a.shape, lambda i, j: (0,) * a.ndim)
    consts = [w["mu_shift"], w["rwkv_vec"], w["rwkv_wlo"], w["rwkv_g2"], w["gla_wg2"], w["gla_bg"], w["gla_norm_w"]]
    return pl.pallas_call(
        functools.partial(_mixers_body, chunk, chained),
        grid=grid,
        in_specs=[rows(cols), prev_spec, st(wkv0), rows(qkv.shape[-1]), rows(lga.shape[-1]), rows(vw), st(gla0)]
        + [const(a) for a in consts],
        out_specs=[rows(width), st(wkv0), rows(vw), st(gla0)],
        out_shape=[jax.ShapeDtypeStruct(lead + (width,), F32), jax.ShapeDtypeStruct(wkv0.shape, F32),
                   jax.ShapeDtypeStruct(lead + (vw,), F32), jax.ShapeDtypeStruct(gla0.shape, F32)],
        scratch_shapes=[pltpu.VMEM((1, cols), F32), pltpu.VMEM((wkv0.shape[1] // 2, LANES, LANES), F32),
                        pltpu.VMEM((gla0.shape[1] // 2, LANES, LANES), F32)],
        compiler_params=pltpu.CompilerParams(dimension_semantics=("parallel", "arbitrary"),
                                             vmem_limit_bytes=VMEM_LIMIT),
        name="mixers",
    )(x_rw, prev, wkv0, qkv, lga, og, gla0, *consts)


def _post_body(chained, seq_len, x_ref, za_ref, zb_ref, gate_ref, aux_ref, woa_ref, wob_ref, wo_ref, g_ref,
               wup_ref, cw_ref, cb_ref, wd_ref, nf_ref, y_ref, tail_ref, carry_ref, act_ref):
    tm, d = x_ref.shape
    hidden = wd_ref.shape[0]
    ya = _dot(za_ref[...].astype(BF16), woa_ref[...])
    yb = _dot(zb_ref[...].astype(BF16), wob_ref[...])
    gate = gate_ref[...]
    merged = jax.nn.sigmoid(gate[:, :d]) * ya + jax.nn.sigmoid(gate[:, d:]) * yb
    x1 = x_ref[...] + _dot(merged.astype(BF16), wo_ref[...])
    hb = _rmsnorm(x1, g_ref[...]).astype(BF16)

    if chained:
        @pl.when(pl.program_id(1) == 0)
        def _():
            carry_ref[0:6, :] = jnp.zeros((6, carry_ref.shape[1]), F32)
            carry_ref[6:8, :] = aux_ref[...]
    else:
        t = _iota2((1, seq_len, 1), 1)

    def conv(u, cs):
        if chained:
            ext = jnp.concatenate([carry_ref[:, cs], u], axis=0)
            carry_ref[:, cs] = u[tm - 8:, :]
            tail_ref[:, cs] = u[tm - 8:, :]
            inner = cw_ref[1:2, cs] * ext + pltpu.roll(cw_ref[0:1, cs] * ext, 1, axis=0)
            return (cb_ref[:, cs] + cw_ref[2:3, cs] * ext + pltpu.roll(inner, 1, axis=0))[8:]
        else:
            st = aux_ref[:, :, cs]
            by_seq = lambda a: a.reshape(tm // seq_len, seq_len, -1)
            p1 = jnp.where(t == 0, st[:, 1:2, :], by_seq(pltpu.roll(u, 1, axis=0))).reshape(tm, -1)
            p2 = jnp.where(t == 0, st[:, 0:1, :],
                           jnp.where(t == 1, st[:, 1:2, :], by_seq(pltpu.roll(u, 2, axis=0)))).reshape(tm, -1)
            tail_ref[:, :, cs] = by_seq(u)[:, seq_len - 2:, :]
        return cb_ref[:, cs] + cw_ref[0:1, cs] * p2 + cw_ref[1:2, cs] * p1 + cw_ref[2:3, cs] * u

    def up(j0):
        cols = (slice(j0, j0 + FFN_CHUNK), slice(hidden + j0, hidden + j0 + FFN_CHUNK))
        return [(_dot(hb, wup_ref[:, cs]), cs) for cs in cols]

    starts = list(range(0, hidden, FFN_CHUNK))
    half = starts[len(starts) // 2] + FFN_CHUNK
    out = x1
    nxt = up(starts[0])
    for i, j0 in enumerate(starts):
        cur = nxt
        if i + 1 < len(starts):
            nxt = up(starts[i + 1])
        val, gat = (conv(u, cs) for u, cs in cur)
        act_ref[:, j0:j0 + FFN_CHUNK] = (jax.nn.gelu(gat) * val).astype(BF16)
        if j0 + FFN_CHUNK == half:
            out = out + _dot(act_ref[:, :half], wd_ref[:half, :])
    y_ref[...] = _rmsnorm(out + _dot(act_ref[:, half:], wd_ref[half:, :]), nf_ref[...])


def _post(x, za, zb, gate, conv_state, w, seq_len):
    b, t, d = x.shape
    m = b * t
    f2 = conv_state.shape[-1]
    chained = seq_len >= POST_TM
    assert seq_len % POST_TM == 0 if chained else POST_TM_SHORT % seq_len == 0
    weights = [w["w_out_a"], w["w_out_b"], w["w_o"], w["norm_ffn"], w["ffn_w_up"], w["ffn_conv_w"],
               w["ffn_conv_b"], w["ffn_w_down"], w["norm_final"]]
    const = lambda a: pl.BlockSpec(a.shape, lambda i, j: (0,) * a.ndim, pipeline_mode=pl.Buffered(1))
    if chained:
        tm = POST_TM
        grid = (b, t // tm)
        rows = lambda a: pl.BlockSpec((None, tm, a.shape[-1]), lambda i, j: (i, j, 0))
        acts = [x, za.reshape(b, t, -1), zb.reshape(b, t, -1), gate.reshape(b, t, -1)]
        aux, aux_spec = conv_state, pl.BlockSpec((None, 2, f2), lambda i, j: (i, 0, 0))
        tail_spec = pl.BlockSpec((None, 8, f2), lambda i, j: (i, 0, 0))
        tail_shape = (b, 8, f2)
    else:
        tm = POST_TM_SHORT
        grid = (m // tm, 1)
        rows = lambda a: pl.BlockSpec((tm, a.shape[-1]), lambda i, j: (i, 0))
        acts = [x.reshape(m, d), za, zb, gate]
        aux = conv_state
        aux_spec = tail_spec = pl.BlockSpec((tm // seq_len, 2, f2), lambda i, j: (i, 0, 0))
        tail_shape = (b, 2, f2)
    y, tail = pl.pallas_call(
        functools.partial(_post_body, chained, seq_len),
        grid=grid,
        in_specs=[rows(a) for a in acts] + [aux_spec] + [const(a) for a in weights],
        out_specs=[rows(acts[0]), tail_spec],
        out_shape=[jax.ShapeDtypeStruct(acts[0].shape, F32), jax.ShapeDtypeStruct(tail_shape, F32)],
        scratch_shapes=[pltpu.VMEM((8, f2), F32), pltpu.VMEM((tm, f2 // 2), BF16)],
        compiler_params=pltpu.CompilerParams(dimension_semantics=("parallel", "arbitrary"),
                                             vmem_limit_bytes=VMEM_LIMIT),
        name="post_mixer",
    )(*acts, aux, *weights)
    new_conv = tail[:, -2:]
    return y.reshape(b, t, d), new_conv


def _trunk(x, shift0, wkv0, gla0, conv0, w):
    b, t, d = x.shape
    m = b * t
    xf = x.reshape(m, d)
    p_rw, p_qkv, p_og, p_gate, p_lga = _norm_proj(xf, w["norm_mix"], w["w_in_parts"])
    shift_cols = p_rw.shape[1]
    new_shift = p_rw.reshape(b, t, shift_cols)[:, -1]

    if t > TB:
        shape3 = lambda a: a.reshape(b, t, a.shape[1])
        prev = shift0[:, None, :]
    else:
        shape3 = lambda a: a[None]
        prev = shift0[:, None, :]
    z_a, wkv_new, z_b, gla_new = _mixers(shape3(p_rw), prev, wkv0, shape3(p_qkv), shape3(p_lga), shape3(p_og),
                                          gla0, w, t)
    z_a = z_a.reshape(m, z_a.shape[-1])
    z_b = z_b.reshape(m, z_b.shape[-1])

    y, new_conv = _post(x, z_a, z_b, p_gate, conv0, w, t)
    return (y, new_shift[None], wkv_new[None], gla_new[None], new_conv[None])


def kernel(x_prompt, x_sample, state_rwkv_shift, state_rwkv_wkv, state_gla, state_ffn_conv, norm_mix, w_in, mu_shift, rwkv_w0, rwkv_w2, rwkv_a0, rwkv_a2, rwkv_g2, rwkv_k_k, rwkv_k_a, rwkv_r_k, rwkv_ln_w, rwkv_ln_b, gla_wg2, gla_bg, gla_norm_w, w_out_a, w_out_b, w_o, norm_ffn, ffn_w_up, ffn_conv_w, ffn_conv_b, ffn_w_down, norm_final):
    assert norm_mix.shape[0] == 1, "single-layer step"
    d = x_prompt.shape[-1]
    width = rwkv_w0.shape[-1]
    shift_cols = mu_shift.shape[-1]
    kw = gla_wg2.shape[-1]
    vw = w_out_b.shape[1]
    lora_g = gla_wg2.shape[1]
    lw, la = rwkv_w2.shape[1], rwkv_a2.shape[1]
    assert lw == HEAD and la == HEAD and rwkv_g2.shape[1] == LANES

    win = w_in[0].astype(BF16)
    c0 = shift_cols
    c1 = c0 + 2 * kw + vw
    c2 = c1 + lora_g
    c3 = c2 + vw
    w_lga = jnp.pad(win[:, c1:c2], ((0, 0), (0, LANES - lora_g)))
    w_in_parts = [win[:, :c0], win[:, c0:c1], win[:, c2:c3], win[:, c3:], w_lga]

    zw = jnp.zeros((lw, width), BF16)
    wlo = jnp.concatenate([jnp.concatenate([rwkv_w2[0].astype(BF16), zw], axis=1),
                           jnp.concatenate([zw, rwkv_a2[0].astype(BF16)], axis=1)], axis=0)
    vec = jnp.stack([rwkv_w0[0], rwkv_a0[0], rwkv_k_k[0], rwkv_k_a[0], rwkv_r_k[0].reshape(width),
                     rwkv_ln_w[0], rwkv_ln_b[0], jnp.zeros((width,), F32)])
    w = dict(
        norm_mix=norm_mix, w_in_parts=w_in_parts, mu_shift=mu_shift, rwkv_vec=vec, rwkv_wlo=wlo,
        rwkv_g2=rwkv_g2[0].astype(BF16),
        gla_wg2=jnp.pad(gla_wg2[0].astype(BF16), ((0, LANES - lora_g), (0, 0))), gla_bg=gla_bg,
        gla_norm_w=gla_norm_w,
        w_out_a=w_out_a[0].astype(BF16), w_out_b=w_out_b[0].astype(BF16), w_o=w_o[0].astype(BF16),
        norm_ffn=norm_ffn, ffn_w_up=ffn_w_up[0].astype(BF16), ffn_conv_w=ffn_conv_w[0],
        ffn_conv_b=ffn_conv_b, ffn_w_down=ffn_w_down[0].astype(BF16), norm_final=norm_final[None],
    )

    bp = x_prompt.shape[0]
    dt = x_prompt.dtype
    zeros = lambda s: jnp.zeros((bp,) + s.shape[2:], dt)
    y_p, shift_p, wkv_p, gla_p, conv_p = _trunk(
        x_prompt, zeros(state_rwkv_shift), zeros(state_rwkv_wkv), zeros(state_gla), zeros(state_ffn_conv), w)
    y_s, shift_s, wkv_s, gla_s, conv_s = _trunk(
        x_sample, state_rwkv_shift[0], state_rwkv_wkv[0], state_gla[0], state_ffn_conv[0], w)
    return (y_p, y_s, shift_p, wkv_p, gla_p, conv_p, shift_s, wkv_s, gla_s, conv_s)
```

```python
import functools
import itertools
import math

import jax
import jax.numpy as jnp
from jax import lax
from jax.experimental import pallas as pl
from jax.experimental.pallas import tpu as pltpu

F32 = jnp.float32
BF16 = jnp.bfloat16
LANES = 128
BF16_ROWS = 16
MXU_DEPTH = 256
TB = 128
MIX_ROWS = 256
TM = 512
POST_TM = 512
POST_TM_SHORT = 128
NCHUNK = 512
FFN_CHUNK = 256
VMEM_LIMIT = 56 * 1024 * 1024
NORM_EPS = 1e-6
HEAD = 64
GLA_GATE_TEMP = 16.0
PROMPT_CHUNK = 32
INV_BLOCK = 16
NN = ((1,), (0,))
NT = ((1,), (1,))
TN = ((0,), (0,))


def _dot(a, b, dims=NN):
    return lax.dot_general(a, b, (dims, ((), ())), preferred_element_type=F32)


def _split(x):
    hi = x.astype(BF16)
    lo = (x - hi.astype(F32)).astype(BF16)
    return hi, lo


def _dot3(a, b, dims=NN):
    ka, kb = dims[0][0], dims[1][0]
    if dims == TN and not isinstance(a, tuple) and not isinstance(b, tuple) and a.shape[0] % BF16_ROWS:
        pad = BF16_ROWS - a.shape[0] % BF16_ROWS
        a = jnp.concatenate([a, jnp.zeros((pad, a.shape[1]), F32)], axis=0)
        b = jnp.concatenate([b, jnp.zeros((pad, b.shape[1]), F32)], axis=0)
    ah, al = a if isinstance(a, tuple) else _split(a)
    bh, bl = b if isinstance(b, tuple) else _split(b)
    k = ah.shape[ka]
    if dims == NN and k == MXU_DEPTH and 2 * bh.shape[1] <= MXU_DEPTH:
        n = bh.shape[1]
        wide = _dot(ah, jnp.concatenate([bh, bl], axis=1))
        return wide[:, :n] + (wide[:, n:] + _dot(al, bh))
    if k % BF16_ROWS or 2 * k > MXU_DEPTH:
        return _dot(ah, bh, dims) + (_dot(ah, bl, dims) + _dot(al, bh, dims))
    if 3 * k <= MXU_DEPTH:
        return _dot(jnp.concatenate([ah, al, ah], axis=ka), jnp.concatenate([bh, bh, bl], axis=kb), dims)
    n = bh.shape[1]
    if dims == NN and 2 * n <= MXU_DEPTH:
        wide = _dot(jnp.concatenate([ah, al], axis=1),
                    jnp.concatenate([jnp.concatenate([bh, bl], axis=1),
                                     jnp.concatenate([bh, jnp.zeros_like(bl)], axis=1)], axis=0))
        return wide[:, :n] + wide[:, n:]
    return _dot(jnp.concatenate([ah, al], axis=ka), jnp.concatenate([bh, bh], axis=kb), dims) + _dot(ah, bl, dims)


def _dot1(a, b, dims=NN):
    ah = a[0] if isinstance(a, tuple) else a.astype(BF16)
    bh = b[0] if isinstance(b, tuple) else b.astype(BF16)
    return _dot(ah, bh, dims)


def _split3(x):
    a1 = x.astype(BF16)
    r1 = x - a1.astype(F32)
    a2 = r1.astype(BF16)
    a3 = (r1 - a2.astype(F32)).astype(BF16)
    return a1, a2, a3


def _dot_exact_lhs(e, x):
    x1, x2, x3 = _split3(x)
    return _dot(jnp.concatenate([e, e], axis=1), jnp.concatenate([x1, x2], axis=0)) + _dot(e, x3)


def _dot_exact_rhs(x, e):
    return _dot(jnp.concatenate(_split(x), axis=1), jnp.concatenate([e, e], axis=0))


def _iota2(shape, dim):
    return lax.broadcasted_iota(jnp.int32, shape, dim)


def _same_block(n, size):
    sh = int(math.log2(size))
    return (_iota2((n, n), 0) >> sh) == (_iota2((n, n), 1) >> sh)


def _ones_where(mask):
    return jnp.where(mask, 1.0, 0.0).astype(BF16)


def _softplus(y):
    return jnp.maximum(y, 0.0) + jnp.log(1.0 + jnp.exp(-jnp.abs(y)))


def _rmsnorm(x, g):
    return x * lax.rsqrt(jnp.mean(x * x, axis=-1, keepdims=True) + NORM_EPS) * g


def _tile(x, p):
    return x[:, p * LANES:(p + 1) * LANES]


def _chunk_masks(chunk):
    same = _same_block(TB, chunk)
    r = _iota2((TB, TB), 0)
    c = _iota2((TB, TB), 1)
    return same & (c <= r), same & (c < r)


def _chunk_cumsum(x, incl, chunk):
    cum = _dot_exact_lhs(_ones_where(incl), x)
    total = jnp.concatenate([jnp.broadcast_to(cum[c + chunk - 1:c + chunk, :], (chunk, x.shape[1]))
                             for c in range(0, TB, chunk)], axis=0)
    return cum, total


def _half_masks():
    lane = _iota2((1, LANES), 1)
    lo = lane < HEAD
    return lo, jnp.logical_not(lo)


def _norm_proj_body(nw, x_ref, g_ref, *refs):
    w_refs, o_refs = refs[:nw], refs[nw:]
    hb = _rmsnorm(x_ref[...], g_ref[...]).astype(BF16)
    for w_ref, o_ref in zip(w_refs, o_refs):
        n = w_ref.shape[1]
        for n0 in range(0, n, NCHUNK):
            n1 = min(n0 + NCHUNK, n)
            o_ref[:, n0:n1] = _dot(hb, w_ref[:, n0:n1])


def _norm_proj(x, g, weights):
    m, d = x.shape
    nw = len(weights)
    return pl.pallas_call(
        functools.partial(_norm_proj_body, nw),
        grid=(m // TM,),
        in_specs=[pl.BlockSpec((TM, d), lambda i: (i, 0)), pl.BlockSpec((1, d), lambda i: (0, 0))]
        + [pl.BlockSpec(w.shape, lambda i: (0, 0), pipeline_mode=pl.Buffered(1)) for w in weights],
        out_specs=[pl.BlockSpec((TM, w.shape[1]), lambda i: (i, 0)) for w in weights],
        out_shape=[jax.ShapeDtypeStruct((m, w.shape[1]), F32) for w in weights],
        compiler_params=pltpu.CompilerParams(dimension_semantics=("parallel",), vmem_limit_bytes=VMEM_LIMIT),
        name="norm_proj",
    )(x, g, *weights)


def _wkv_pair_load(ref, idx, p):
    a, b = ref[idx + (2 * p,)], ref[idx + (2 * p + 1,)]
    z = jnp.zeros_like(a)
    return jnp.concatenate([jnp.concatenate([a, z], axis=1), jnp.concatenate([z, b], axis=1)], axis=0)


def _wkv_pair_store(ref, idx, p, s):
    n = s.shape[0] // 2
    ref[idx + (2 * p,)] = s[:n, :n]
    ref[idx + (2 * p + 1,)] = s[n:, n:]


def _rwkv_body(chunk, chained, start, x_ref, prev_ref, s0_ref, mu_ref, vec_ref, wlo_ref, g2_ref,
               z_ref, sout_ref, carry_ref, s_scr):
    nchunk = TB // chunk
    width = z_ref.shape[-1]
    npair = width // LANES
    x = x_ref[...]
    row = _iota2((TB, 1), 0)
    rolled = pltpu.roll(x, 1, axis=0)
    if chained:
        if start is not None:
            @pl.when(start)
            def _():
                carry_ref[...] = prev_ref[...]
                for p in range(npair):
                    s_scr[p] = _wkv_pair_load(s0_ref, (), p)
        prev = jnp.where(row == 0, carry_ref[...], rolled)
        carry_ref[...] = x[TB - 1:TB, :]
    else:
        first = _iota2((1, chunk, 1), 1) == 0
        prev = jnp.where(first, prev_ref[...], rolled.reshape(nchunk, chunk, -1)).reshape(TB, -1)
    xs = x + (prev - x) * mu_ref[...]
    yield

    r = xs[:, 0:width]
    k = xs[:, width:2 * width]
    v = xs[:, 2 * width:3 * width]
    lora = xs[:, 3 * width:3 * width + LANES]
    lg = xs[:, 3 * width + LANES:3 * width + 2 * LANES]
    lo_half, hi_half = _half_masks()
    lora = jnp.where(lo_half, jnp.tanh(lora), lora)
    wa = _dot(lora.astype(BF16), wlo_ref[...])
    w0, a0, k_k, k_a = vec_ref[0:1, :], vec_ref[1:2, :], vec_ref[2:3, :], vec_ref[3:4, :]
    r_k, ln_w, ln_b = vec_ref[4:5, :], vec_ref[5:6, :], vec_ref[6:7, :]
    wlog = -_softplus(-(w0 + wa[:, :width])) - 0.5
    logw = -jnp.exp(wlog)
    asig = jax.nn.sigmoid(a0 + wa[:, width:])
    g = _dot(jax.nn.sigmoid(lg).astype(BF16), g2_ref[...])
    yield

    seg = _ones_where(_same_block(LANES, HEAD))

    def headsum(t):
        return jnp.concatenate([_dot_exact_rhs(_tile(t, p), seg) for p in range(npair)], axis=1)

    kk = k * k_k
    kk = kk / jnp.maximum(jnp.sqrt(headsum(kk * kk)), 1e-12)
    kh = k * (1.0 + (asig - 1.0) * k_a)
    a_vec = -kk
    b_vec = kk * asig
    yield

    incl, strict = _chunk_masks(chunk)
    cum, cum_c = _chunk_cumsum(logw, incl, chunk)
    e_in = jnp.exp(cum)
    e_out = jnp.exp(-cum)
    e_end = jnp.exp(cum_c - cum)
    yield
    rt = r * e_in
    at = a_vec * jnp.exp(cum - logw)
    bt = b_vec * e_out
    kt = kh * e_out
    bp = b_vec * e_end
    kp = kh * e_end
    e_c = jnp.exp(cum_c)
    yield

    def msk(t, h):
        keep = lo_half if h == 0 else hi_half
        if isinstance(t, tuple):
            return tuple(jnp.where(keep, x, jnp.zeros_like(x)) for x in t)
        return jnp.where(keep, t, 0.0)

    def cat(parts, axis):
        return tuple(jnp.concatenate([part[i] for part in parts], axis=axis) for i in range(2))

    by_head = lambda t: cat([msk(t, 0), msk(t, 1)], 0)

    eye = jnp.where(_iota2((TB, TB), 0) == _iota2((TB, TB), 1), 1.0, 0.0)
    pairs = range(npair)
    heads = [(p, h) for p in pairs for h in range(2)]
    at_t, rt_t, bt_t, kt_t, v_t, bp_t, kp_t, ec_t = ([_tile(t, p) for p in pairs]
                                                     for t in (at, rt, bt, kt, v, bp, kp, e_c))
    at_s = [_split(t) for t in at_t]
    v_h = [by_head(_split(t)) for t in v_t]
    bk = [cat([by_head(_split(bt_t[p])), by_head(_split(kt_t[p]))], 0) for p in pairs]
    sc_a = [_dot3(at_s[p], bk[p], NT) for p in pairs]
    sc_r = [_dot1(rt_t[p], bk[p], NT) for p in pairs]
    a_ab = [jnp.where(strict, sc_a[p][:, h * TB:(h + 1) * TB], 0.0) for p, h in heads]
    a_ak = [jnp.where(strict, sc_a[p][:, (2 + h) * TB:(3 + h) * TB], 0.0) for p, h in heads]
    p_rb = [jnp.where(incl, sc_r[p][:, h * TB:(h + 1) * TB], 0.0) for p, h in heads]
    p_rk = [jnp.where(incl, sc_r[p][:, (2 + h) * TB:(3 + h) * TB], 0.0) for p, h in heads]
    yield
    base = min(chunk, INV_BLOCK)
    assert chunk in (base, 2 * base)
    in_base = _same_block(TB, base)
    n_in = [jnp.where(in_base, n, 0.0) for n in a_ab]
    t_inv = [eye + n for n in n_in]
    levels = int(math.log2(base))
    splits = lambda mats: [_split(mat) for mat in mats]
    powers = [_dot3(n_s, n_s) for n_s in splits(n_in)] if levels > 1 else n_in
    yield
    for level in range(1, levels):
        x_s, t_s = splits(powers), splits(t_inv)
        if level + 1 < levels:
            both = [_dot3(xs, cat([xs, ts], 1)) for xs, ts in zip(x_s, t_s)]
            powers = [b[:, :TB] for b in both]
            t_inv = [t + b[:, TB:] for t, b in zip(t_inv, both)]
        else:
            t_inv = [t + _dot3(xs, ts) for t, xs, ts in zip(t_inv, x_s, t_s)]
        yield
    if chunk > base:
        t_s = splits(t_inv)
        n_off = [_dot3(n - ni, ts) for n, ni, ts in zip(a_ab, n_in, t_s)]
        yield
        t_inv = [t + _dot3(ts, no) for t, ts, no in zip(t_inv, t_s, n_off)]
    yield
    av = [_dot3(jnp.concatenate(a_ak[2 * p:2 * p + 2], axis=1), v_h[p]) for p in pairs]
    av_s = splits(av)
    tay = [_dot3(jnp.concatenate(t_inv[2 * p:2 * p + 2], axis=1),
                 cat([cat([msk(at_s[p], h), msk(av_s[p], h)], 1) for h in range(2)], 0))
           for p in pairs]
    yield
    ta = [t[:, :LANES] for t in tay]
    yy = [t[:, LANES:] for t in tay]
    qz = [_dot1(jnp.concatenate(p_rb[2 * p:2 * p + 2], axis=1),
                jnp.concatenate([jnp.concatenate([msk(ta[p], h), msk(yy[p], h)], axis=1) for h in range(2)], axis=0))
          for p in pairs]
    zv = [_dot1(jnp.concatenate(p_rk[2 * p:2 * p + 2], axis=1), v_h[p]) for p in pairs]
    qq = [rt_t[p] + qz[p][:, :LANES] for p in pairs]
    zz = [qz[p][:, LANES:] + zv[p] for p in pairs]

    bd = _same_block(LANES, HEAD)
    diag = _iota2((LANES, LANES), 0) == _iota2((LANES, LANES), 1)
    rows = lambda t, c: t[c * chunk:(c + 1) * chunk]
    m_mat = [[jnp.where(diag, ec_t[p][c * chunk:c * chunk + 1, :], 0.0)
              + jnp.where(bd, _dot3(rows(ta[p], c), rows(bp_t[p], c), TN), 0.0) for p in pairs]
             for c in range(nchunk)]
    yield
    n_mat = [[jnp.where(bd, _dot3(jnp.concatenate([rows(yy[p], c), rows(v_t[p], c)], axis=0),
                                  jnp.concatenate([rows(bp_t[p], c), rows(kp_t[p], c)], axis=0), TN), 0.0)
              for p in pairs] for c in range(nchunk)]
    yield
    state = [s_scr[p] for p in pairs] if chained else None
    o_rows = [[] for _ in pairs]
    for c in range(nchunk):
        for p in pairs:
            s = _split(state[p] if chained else _wkv_pair_load(s0_ref, (c,), p))
            o_rows[p].append(_dot1(rows(qq[p], c), s, NT) + rows(zz[p], c))
            s = _dot3(s, m_mat[c][p]) + n_mat[c][p]
            if chained:
                state[p] = s
            else:
                _wkv_pair_store(sout_ref, (c,), p, s)
        if chained:
            yield
    if chained:
        for p in pairs:
            s_scr[p] = state[p]

        @pl.when(pl.program_id(1) == pl.num_programs(1) - 1)
        def _():
            for p in pairs:
                _wkv_pair_store(sout_ref, (), p, state[p])
    o = jnp.concatenate([jnp.concatenate(o_rows[p], axis=0) for p in pairs], axis=1)
    yield

    inv_n = 1.0 / HEAD
    mean = headsum(o) * inv_n
    d = o - mean
    var = headsum(d * d) * inv_n
    o = d * lax.rsqrt(var + 1e-5 * HEAD) * ln_w + ln_b
    o = o + headsum(r * kh * r_k) * v
    z_ref[...] = o * g


def _gla_pair_load(ref, idx, p):
    return jnp.concatenate([ref[idx + (2 * p,)], ref[idx + (2 * p + 1,)]], axis=0).T


def _gla_pair_store(ref, idx, p, s):
    st = s.T
    dk = st.shape[0] // 2
    ref[idx + (2 * p,)] = st[:dk]
    ref[idx + (2 * p + 1,)] = st[dk:]


def _gla_body(chunk, chained, start, qkv_ref, lga_ref, og_ref, s0_ref, wg2_ref, bg_ref, nw_ref,
              z_ref, sout_ref, s_scr):
    nchunk = TB // chunk
    vw = z_ref.shape[-1]
    nhead = vw // LANES
    npair = nhead // 2
    kw = npair * LANES
    qkv = qkv_ref[...]
    q = qkv[:, 0:kw] * (HEAD ** -0.5)
    k = qkv[:, kw:2 * kw]
    v = qkv[:, 2 * kw:2 * kw + vw]
    gl = _dot(lga_ref[...].astype(BF16), wg2_ref[...]) + bg_ref[...]
    log_a = -_softplus(-gl) * (1.0 / GLA_GATE_TEMP)

    incl, _ = _chunk_masks(chunk)
    cum, cum_c = _chunk_cumsum(log_a, incl, chunk)
    qt = q * jnp.exp(cum)
    kt = k * jnp.exp(-cum)
    kp = k * jnp.exp(cum_c - cum)
    e_c = jnp.exp(cum_c)
    yield
    lo_half, hi_half = _half_masks()

    def msk(t, h):
        return jnp.where(lo_half if h == 0 else hi_half, t, 0.0)

    if chained and start is not None:
        @pl.when(start)
        def _():
            for p in range(npair):
                s_scr[p] = _gla_pair_load(s0_ref, (), p)

    pairs = range(npair)
    heads = [(p, h) for p in pairs for h in range(2)]
    rows = lambda t, c: t[c * chunk:(c + 1) * chunk]
    kt_t, kp_t, ec_t = ([_tile(t, p) for p in pairs] for t in (kt, kp, e_c))
    v_t = [_tile(v, h) for h in range(nhead)]
    qm = [msk(_tile(qt, p), h) for p, h in heads]
    sc = [_dot1(jnp.concatenate(qm[2 * p:2 * p + 2], axis=0), kt_t[p], NT) for p in pairs]
    intra = [_dot1(jnp.where(incl, sc[p][h * TB:(h + 1) * TB], 0.0), v_t[2 * p + h]) for p, h in heads]
    upd = [[jnp.where(lo_half, _dot3(rows(v_t[2 * p], c), rows(kp_t[p], c), TN),
                      _dot3(rows(v_t[2 * p + 1], c), rows(kp_t[p], c), TN)) for p in pairs]
           for c in range(nchunk)]
    yield
    state = [s_scr[p] for p in pairs] if chained else None
    inter = [[] for _ in range(nhead)]
    for c in range(nchunk):
        for p in pairs:
            s = state[p] if chained else _gla_pair_load(s0_ref, (c,), p)
            io = _dot3(jnp.concatenate([rows(qm[2 * p], c), rows(qm[2 * p + 1], c)], axis=0), s, NT)
            inter[2 * p].append(io[:chunk])
            inter[2 * p + 1].append(io[chunk:])
            s = s * ec_t[p][c * chunk:c * chunk + 1, :] + upd[c][p]
            if chained:
                state[p] = s
            else:
                _gla_pair_store(sout_ref, (c,), p, s)
    if chained:
        for p in pairs:
            s_scr[p] = state[p]

        @pl.when(pl.program_id(1) == pl.num_programs(1) - 1)
        def _():
            for p in pairs:
                _gla_pair_store(sout_ref, (), p, state[p])
    o_heads = [intra[h] + jnp.concatenate(inter[h], axis=0) for h in range(nhead)]
    yield

    og = og_ref[...]
    for h in range(nhead):
        o = o_heads[h]
        o = o * lax.rsqrt(jnp.mean(o * o, axis=-1, keepdims=True) + NORM_EPS) * nw_ref[...]
        z_ref[:, h * LANES:(h + 1) * LANES] = o * jax.nn.silu(_tile(og, h))


def _mixers_body(chunk, chained, x_ref, prev_ref, wkv0_ref, qkv_ref, lga_ref, og_ref, gla0_ref,
                 mu_ref, vec_ref, wlo_ref, g2_ref, wg2_ref, bg_ref, nw_ref,
                 za_ref, wkv_ref, zb_ref, gla_ref, carry_ref, wkv_scr, gla_scr):
    for blk in range(x_ref.shape[0] // TB):
        rows = lambda ref: ref.at[pl.ds(blk * TB, TB)]
        start = (pl.program_id(1) == 0) if blk == 0 else None
        rwkv = _rwkv_body(chunk, chained, start, rows(x_ref), prev_ref, wkv0_ref, mu_ref, vec_ref, wlo_ref, g2_ref,
                          rows(za_ref), wkv_ref, carry_ref, wkv_scr)
        gla = _gla_body(chunk, chained, start, rows(qkv_ref), rows(lga_ref), rows(og_ref), gla0_ref, wg2_ref, bg_ref,
                        nw_ref, rows(zb_ref), gla_ref, gla_scr)
        for _ in itertools.zip_longest(rwkv, gla):
            pass


def _mixers(x_rw, prev, wkv0, qkv, lga, og, gla0, w, seq_len):
    width = wkv0.shape[1] * wkv0.shape[2]
    vw = og.shape[-1]
    cols = x_rw.shape[-1]
    chained = seq_len > TB
    if chained:
        chunk = PROMPT_CHUNK
        b, t, _ = x_rw.shape
        assert t % MIX_ROWS == 0
        grid = (b, t // MIX_ROWS)
        rows = lambda n: pl.BlockSpec((None, MIX_ROWS, n), lambda i, j: (i, j, 0))
        prev_spec = pl.BlockSpec((None, 1, cols), lambda i, j: (i, 0, 0))
        st = lambda a: pl.BlockSpec((None,) + a.shape[1:], lambda i, j: (i, 0, 0, 0))
        lead = (b, t)
    else:
        chunk = seq_len
        m = x_rw.shape[1]
        assert m % TB == 0
        grid = (m // TB, 1)
        rows = lambda n: pl.BlockSpec((None, TB, n), lambda i, j: (0, i, 0))
        prev_spec = pl.BlockSpec((TB // chunk, 1, cols), lambda i, j: (i, 0, 0))
        st = lambda a: pl.BlockSpec((TB // chunk,) + a.shape[1:], lambda i, j: (i, 0, 0, 0))
        lead = (1, m)
    const = lambda a: pl.BlockSpec(a.shape, lambda i, j: (0,) * a.ndim)
    consts = [w["mu_shift"], w["rwkv_vec"], w["rwkv_wlo"], w["rwkv_g2"], w["gla_wg2"], w["gla_bg"], w["gla_norm_w"]]
    return pl.pallas_call(
        functools.partial(_mixers_body, chunk, chained),
        grid=grid,
        in_specs=[rows(cols), prev_spec, st(wkv0), rows(qkv.shape[-1]), rows(lga.shape[-1]), rows(vw), st(gla0)]
        + [const(a) for a in consts],
        out_specs=[rows(width), st(wkv0), rows(vw), st(gla0)],
        out_shape=[jax.ShapeDtypeStruct(lead + (width,), F32), jax.ShapeDtypeStruct(wkv0.shape, F32),
                   jax.ShapeDtypeStruct(lead + (vw,), F32), jax.ShapeDtypeStruct(gla0.shape, F32)],
        scratch_shapes=[pltpu.VMEM((1, cols), F32), pltpu.VMEM((wkv0.shape[1] // 2, LANES, LANES), F32),
                        pltpu.VMEM((gla0.shape[1] // 2, LANES, LANES), F32)],
        compiler_params=pltpu.CompilerParams(dimension_semantics=("parallel", "arbitrary"),
                                             vmem_limit_bytes=VMEM_LIMIT),
        name="mixers",
    )(x_rw, prev, wkv0, qkv, lga, og, gla0, *consts)


def _post_body(chained, seq_len, x_ref, za_ref, zb_ref, gate_ref, aux_ref, woa_ref, wob_ref, wo_ref, g_ref,
               wup_ref, cw_ref, cb_ref, wd_ref, nf_ref, y_ref, tail_ref, carry_ref, act_ref):
    tm, d = x_ref.shape
    hidden = wd_ref.shape[0]
    ya = _dot(za_ref[...].astype(BF16), woa_ref[...])
    yb = _dot(zb_ref[...].astype(BF16), wob_ref[...])
    gate = gate_ref[...]
    merged = jax.nn.sigmoid(gate[:, :d]) * ya + jax.nn.sigmoid(gate[:, d:]) * yb
    x1 = x_ref[...] + _dot(merged.astype(BF16), wo_ref[...])
    hb = _rmsnorm(x1, g_ref[...]).astype(BF16)

    if chained:
        @pl.when(pl.program_id(1) == 0)
        def _():
            carry_ref[0:6, :] = jnp.zeros((6, carry_ref.shape[1]), F32)
            carry_ref[6:8, :] = aux_ref[...]
    else:
        t = _iota2((1, seq_len, 1), 1)

    def conv(u, cs):
        if chained:
            ext = jnp.concatenate([carry_ref[:, cs], u], axis=0)
            carry_ref[:, cs] = u[tm - 8:, :]
            tail_ref[:, cs] = u[tm - 8:, :]
            inner = cw_ref[1:2, cs] * ext + pltpu.roll(cw_ref[0:1, cs] * ext, 1, axis=0)
            return (cb_ref[:, cs] + cw_ref[2:3, cs] * ext + pltpu.roll(inner, 1, axis=0))[8:]
        else:
            st = aux_ref[:, :, cs]
            by_seq = lambda a: a.reshape(tm // seq_len, seq_len, -1)
            p1 = jnp.where(t == 0, st[:, 1:2, :], by_seq(pltpu.roll(u, 1, axis=0))).reshape(tm, -1)
            p2 = jnp.where(t == 0, st[:, 0:1, :],
                           jnp.where(t == 1, st[:, 1:2, :], by_seq(pltpu.roll(u, 2, axis=0)))).reshape(tm, -1)
            tail_ref[:, :, cs] = by_seq(u)[:, seq_len - 2:, :]
        return cb_ref[:, cs] + cw_ref[0:1, cs] * p2 + cw_ref[1:2, cs] * p1 + cw_ref[2:3, cs] * u

    def up(j0):
        cols = (slice(j0, j0 + FFN_CHUNK), slice(hidden + j0, hidden + j0 + FFN_CHUNK))
        return [(_dot(hb, wup_ref[:, cs]), cs) for cs in cols]

    starts = list(range(0, hidden, FFN_CHUNK))
    half = starts[len(starts) // 2] + FFN_CHUNK
    out = x1
    nxt = up(starts[0])
    for i, j0 in enumerate(starts):
        cur = nxt
        if i + 1 < len(starts):
            nxt = up(starts[i + 1])
        val, gat = (conv(u, cs) for u, cs in cur)
        act_ref[:, j0:j0 + FFN_CHUNK] = (jax.nn.gelu(gat) * val).astype(BF16)
        if j0 + FFN_CHUNK == half:
            out = out + _dot(act_ref[:, :half], wd_ref[:half, :])
    y_ref[...] = _rmsnorm(out + _dot(act_ref[:, half:], wd_ref[half:, :]), nf_ref[...])


def _post(x, za, zb, gate, conv_state, w, seq_len):
    b, t, d = x.shape
    m = b * t
    f2 = conv_state.shape[-1]
    chained = seq_len >= POST_TM
    assert seq_len % POST_TM == 0 if chained else POST_TM_SHORT % seq_len == 0
    weights = [w["w_out_a"], w["w_out_b"], w["w_o"], w["norm_ffn"], w["ffn_w_up"], w["ffn_conv_w"],
               w["ffn_conv_b"], w["ffn_w_down"], w["norm_final"]]
    const = lambda a: pl.BlockSpec(a.shape, lambda i, j: (0,) * a.ndim, pipeline_mode=pl.Buffered(1))
    if chained:
        tm = POST_TM
        grid = (b, t // tm)
        rows = lambda a: pl.BlockSpec((None, tm, a.shape[-1]), lambda i, j: (i, j, 0))
        acts = [x, za.reshape(b, t, -1), zb.reshape(b, t, -1), gate.reshape(b, t, -1)]
        aux, aux_spec = conv_state, pl.BlockSpec((None, 2, f2), lambda i, j: (i, 0, 0))
        tail_spec = pl.BlockSpec((None, 8, f2), lambda i, j: (i, 0, 0))
        tail_shape = (b, 8, f2)
    else:
        tm = POST_TM_SHORT
        grid = (m // tm, 1)
        rows = lambda a: pl.BlockSpec((tm, a.shape[-1]), lambda i, j: (i, 0))
        acts = [x.reshape(m, d), za, zb, gate]
        aux = conv_state
        aux_spec = tail_spec = pl.BlockSpec((tm // seq_len, 2, f2), lambda i, j: (i, 0, 0))
        tail_shape = (b, 2, f2)
    y, tail = pl.pallas_call(
        functools.partial(_post_body, chained, seq_len),
        grid=grid,
        in_specs=[rows(a) for a in acts] + [aux_spec] + [const(a) for a in weights],
        out_specs=[rows(acts[0]), tail_spec],
        out_shape=[jax.ShapeDtypeStruct(acts[0].shape, F32), jax.ShapeDtypeStruct(tail_shape, F32)],
        scratch_shapes=[pltpu.VMEM((8, f2), F32), pltpu.VMEM((tm, f2 // 2), BF16)],
        compiler_params=pltpu.CompilerParams(dimension_semantics=("parallel", "arbitrary"),
                                             vmem_limit_bytes=VMEM_LIMIT),
        name="post_mixer",
    )(*acts, aux, *weights)
    new_conv = tail[:, -2:]
    return y.reshape(b, t, d), new_conv


def _trunk(x, shift0, wkv0, gla0, conv0, w):
    b, t, d = x.shape
    m = b * t
    xf = x.reshape(m, d)
    p_rw, p_qkv, p_og, p_gate, p_lga = _norm_proj(xf, w["norm_mix"], w["w_in_parts"])
    shift_cols = p_rw.shape[1]
    new_shift = p_rw.reshape(b, t, shift_cols)[:, -1]

    if t > TB:
        shape3 = lambda a: a.reshape(b, t, a.shape[1])
        prev = shift0[:, None, :]
    else:
        shape3 = lambda a: a[None]
        prev = shift0[:, None, :]
    z_a, wkv_new, z_b, gla_new = _mixers(shape3(p_rw), prev, wkv0, shape3(p_qkv), shape3(p_lga), shape3(p_og),
                                          gla0, w, t)
    z_a = z_a.reshape(m, z_a.shape[-1])
    z_b = z_b.reshape(m, z_b.shape[-1])

    y, new_conv = _post(x, z_a, z_b, p_gate, conv0, w, t)
    return (y, new_shift[None], wkv_new[None], gla_new[None], new_conv[None])


def kernel(x_prompt, x_sample, state_rwkv_shift, state_rwkv_wkv, state_gla, state_ffn_conv, norm_mix, w_in, mu_shift, rwkv_w0, rwkv_w2, rwkv_a0, rwkv_a2, rwkv_g2, rwkv_k_k, rwkv_k_a, rwkv_r_k, rwkv_ln_w, rwkv_ln_b, gla_wg2, gla_bg, gla_norm_w, w_out_a, w_out_b, w_o, norm_ffn, ffn_w_up, ffn_conv_w, ffn_conv_b, ffn_w_down, norm_final):
    assert norm_mix.shape[0] == 1, "single-layer step"
    d = x_prompt.shape[-1]
    width = rwkv_w0.shape[-1]
    shift_cols = mu_shift.shape[-1]
    kw = gla_wg2.shape[-1]
    vw = w_out_b.shape[1]
    lora_g = gla_wg2.shape[1]
    lw, la = rwkv_w2.shape[1], rwkv_a2.shape[1]
    assert lw == HEAD and la == HEAD and rwkv_g2.shape[1] == LANES

    win = w_in[0].astype(BF16)
    c0 = shift_cols
    c1 = c0 + 2 * kw + vw
    c2 = c1 + lora_g
    c3 = c2 + vw
    w_lga = jnp.pad(win[:, c1:c2], ((0, 0), (0, LANES - lora_g)))
    w_in_parts = [win[:, :c0], win[:, c0:c1], win[:, c2:c3], win[:, c3:], w_lga]

    zw = jnp.zeros((lw, width), BF16)
    wlo = jnp.concatenate([jnp.concatenate([rwkv_w2[0].astype(BF16), zw], axis=1),
                           jnp.concatenate([zw, rwkv_a2[0].astype(BF16)], axis=1)], axis=0)
    vec = jnp.stack([rwkv_w0[0], rwkv_a0[0], rwkv_k_k[0], rwkv_k_a[0], rwkv_r_k[0].reshape(width),
                     rwkv_ln_w[0], rwkv_ln_b[0], jnp.zeros((width,), F32)])
    w = dict(
        norm_mix=norm_mix, w_in_parts=w_in_parts, mu_shift=mu_shift, rwkv_vec=vec, rwkv_wlo=wlo,
        rwkv_g2=rwkv_g2[0].astype(BF16),
        gla_wg2=jnp.pad(gla_wg2[0].astype(BF16), ((0, LANES - lora_g), (0, 0))), gla_bg=gla_bg,
        gla_norm_w=gla_norm_w,
        w_out_a=w_out_a[0].astype(BF16), w_out_b=w_out_b[0].astype(BF16), w_o=w_o[0].astype(BF16),
        norm_ffn=norm_ffn, ffn_w_up=ffn_w_up[0].astype(BF16), ffn_conv_w=ffn_conv_w[0],
        ffn_conv_b=ffn_conv_b, ffn_w_down=ffn_w_down[0].astype(BF16), norm_final=norm_final[None],
    )

    bp = x_prompt.shape[0]
    dt = x_prompt.dtype
    zeros = lambda s: jnp.zeros((bp,) + s.shape[2:], dt)
    y_p, shift_p, wkv_p, gla_p, conv_p = _trunk(
        x_prompt, zeros(state_rwkv_shift), zeros(state_rwkv_wkv), zeros(state_gla), zeros(state_ffn_conv), w)
    y_s, shift_s, wkv_s, gla_s, conv_s = _trunk(
        x_sample, state_rwkv_shift[0], state_rwkv_wkv[0], state_gla[0], state_ffn_conv[0], w)
    return (y_p, y_s, shift_p, wkv_p, gla_p, conv_p, shift_s, wkv_s, gla_s, conv_s)
```

```python
import functools
import itertools
import math

import jax
import jax.numpy as jnp
from jax import lax
from jax.experimental import pallas as pl
from jax.experimental.pallas import tpu as pltpu

F32 = jnp.float32
BF16 = jnp.bfloat16
LANES = 128
BF16_ROWS = 16
MXU_DEPTH = 256
TB = 128
MIX_ROWS = 512
TM = 512
POST_TM = 512
POST_TM_SHORT = 128
NCHUNK = 512
FFN_CHUNK = 256
VMEM_LIMIT = 56 * 1024 * 1024
NORM_EPS = 1e-6
HEAD = 64
GLA_GATE_TEMP = 16.0
PROMPT_CHUNK = 32
INV_BLOCK = 16
NN = ((1,), (0,))
NT = ((1,), (1,))
TN = ((0,), (0,))


def _dot(a, b, dims=NN):
    return lax.dot_general(a, b, (dims, ((), ())), preferred_element_type=F32)


def _split(x):
    hi = x.astype(BF16)
    lo = (x - hi.astype(F32)).astype(BF16)
    return hi, lo


def _dot3(a, b, dims=NN):
    ka, kb = dims[0][0], dims[1][0]
    if dims == TN and not isinstance(a, tuple) and not isinstance(b, tuple) and a.shape[0] % BF16_ROWS:
        pad = BF16_ROWS - a.shape[0] % BF16_ROWS
        a = jnp.concatenate([a, jnp.zeros((pad, a.shape[1]), F32)], axis=0)
        b = jnp.concatenate([b, jnp.zeros((pad, b.shape[1]), F32)], axis=0)
    ah, al = a if isinstance(a, tuple) else _split(a)
    bh, bl = b if isinstance(b, tuple) else _split(b)
    k = ah.shape[ka]
    if dims == NN and k == MXU_DEPTH and 2 * bh.shape[1] <= MXU_DEPTH:
        n = bh.shape[1]
        wide = _dot(ah, jnp.concatenate([bh, bl], axis=1))
        return wide[:, :n] + (wide[:, n:] + _dot(al, bh))
    if k % BF16_ROWS or 2 * k > MXU_DEPTH:
        return _dot(ah, bh, dims) + (_dot(ah, bl, dims) + _dot(al, bh, dims))
    if 3 * k <= MXU_DEPTH:
        return _dot(jnp.concatenate([ah, al, ah], axis=ka), jnp.concatenate([bh, bh, bl], axis=kb), dims)
    n = bh.shape[1]
    if dims == NN and 2 * n <= MXU_DEPTH:
        wide = _dot(jnp.concatenate([ah, al], axis=1),
                    jnp.concatenate([jnp.concatenate([bh, bl], axis=1),
                                     jnp.concatenate([bh, jnp.zeros_like(bl)], axis=1)], axis=0))
        return wide[:, :n] + wide[:, n:]
    return _dot(jnp.concatenate([ah, al], axis=ka), jnp.concatenate([bh, bh], axis=kb), dims) + _dot(ah, bl, dims)


def _dot1(a, b, dims=NN):
    ah = a[0] if isinstance(a, tuple) else a.astype(BF16)
    bh = b[0] if isinstance(b, tuple) else b.astype(BF16)
    return _dot(ah, bh, dims)


def _split3(x):
    a1 = x.astype(BF16)
    r1 = x - a1.astype(F32)
    a2 = r1.astype(BF16)
    a3 = (r1 - a2.astype(F32)).astype(BF16)
    return a1, a2, a3


def _dot_exact_lhs(e, x):
    x1, x2, x3 = _split3(x)
    return _dot(jnp.concatenate([e, e], axis=1), jnp.concatenate([x1, x2], axis=0)) + _dot(e, x3)


def _dot_exact_rhs(x, e):
    return _dot(jnp.concatenate(_split(x), axis=1), jnp.concatenate([e, e], axis=0))


def _iota2(shape, dim):
    return lax.broadcasted_iota(jnp.int32, shape, dim)


def _same_block(n, size):
    sh = int(math.log2(size))
    return (_iota2((n, n), 0) >> sh) == (_iota2((n, n), 1) >> sh)


def _ones_where(mask):
    return jnp.where(mask, 1.0, 0.0).astype(BF16)


def _softplus(y):
    return jnp.maximum(y, 0.0) + jnp.log(1.0 + jnp.exp(-jnp.abs(y)))


def _rmsnorm(x, g):
    return x * lax.rsqrt(jnp.mean(x * x, axis=-1, keepdims=True) + NORM_EPS) * g


def _tile(x, p):
    return x[:, p * LANES:(p + 1) * LANES]


def _chunk_masks(chunk):
    same = _same_block(TB, chunk)
    r = _iota2((TB, TB), 0)
    c = _iota2((TB, TB), 1)
    return same & (c <= r), same & (c < r)


def _chunk_cumsum(x, incl, chunk):
    cum = _dot_exact_lhs(_ones_where(incl), x)
    total = jnp.concatenate([jnp.broadcast_to(cum[c + chunk - 1:c + chunk, :], (chunk, x.shape[1]))
                             for c in range(0, TB, chunk)], axis=0)
    return cum, total


def _half_masks():
    lane = _iota2((1, LANES), 1)
    lo = lane < HEAD
    return lo, jnp.logical_not(lo)


def _norm_proj_body(nw, x_ref, g_ref, *refs):
    w_refs, o_refs = refs[:nw], refs[nw:]
    hb = _rmsnorm(x_ref[...], g_ref[...]).astype(BF16)
    for w_ref, o_ref in zip(w_refs, o_refs):
        n = w_ref.shape[1]
        for n0 in range(0, n, NCHUNK):
            n1 = min(n0 + NCHUNK, n)
            o_ref[:, n0:n1] = _dot(hb, w_ref[:, n0:n1])


def _norm_proj(x, g, weights):
    m, d = x.shape
    nw = len(weights)
    return pl.pallas_call(
        functools.partial(_norm_proj_body, nw),
        grid=(m // TM,),
        in_specs=[pl.BlockSpec((TM, d), lambda i: (i, 0)), pl.BlockSpec((1, d), lambda i: (0, 0))]
        + [pl.BlockSpec(w.shape, lambda i: (0, 0), pipeline_mode=pl.Buffered(1)) for w in weights],
        out_specs=[pl.BlockSpec((TM, w.shape[1]), lambda i: (i, 0)) for w in weights],
        out_shape=[jax.ShapeDtypeStruct((m, w.shape[1]), F32) for w in weights],
        compiler_params=pltpu.CompilerParams(dimension_semantics=("parallel",), vmem_limit_bytes=VMEM_LIMIT),
        name="norm_proj",
    )(x, g, *weights)


def _wkv_pair_load(ref, idx, p):
    a, b = ref[idx + (2 * p,)], ref[idx + (2 * p + 1,)]
    z = jnp.zeros_like(a)
    return jnp.concatenate([jnp.concatenate([a, z], axis=1), jnp.concatenate([z, b], axis=1)], axis=0)


def _wkv_pair_store(ref, idx, p, s):
    n = s.shape[0] // 2
    ref[idx + (2 * p,)] = s[:n, :n]
    ref[idx + (2 * p + 1,)] = s[n:, n:]


def _rwkv_body(chunk, chained, start, x_ref, prev_ref, s0_ref, mu_ref, vec_ref, wlo_ref, g2_ref,
               z_ref, sout_ref, carry_ref, s_scr):
    nchunk = TB // chunk
    width = z_ref.shape[-1]
    npair = width // LANES
    x = x_ref[...]
    row = _iota2((TB, 1), 0)
    rolled = pltpu.roll(x, 1, axis=0)
    if chained:
        if start is not None:
            @pl.when(start)
            def _():
                carry_ref[...] = prev_ref[...]
                for p in range(npair):
                    s_scr[p] = _wkv_pair_load(s0_ref, (), p)
        prev = jnp.where(row == 0, carry_ref[...], rolled)
        carry_ref[...] = x[TB - 1:TB, :]
    else:
        first = _iota2((1, chunk, 1), 1) == 0
        prev = jnp.where(first, prev_ref[...], rolled.reshape(nchunk, chunk, -1)).reshape(TB, -1)
    xs = x + (prev - x) * mu_ref[...]
    yield

    r = xs[:, 0:width]
    k = xs[:, width:2 * width]
    v = xs[:, 2 * width:3 * width]
    lora = xs[:, 3 * width:3 * width + LANES]
    lg = xs[:, 3 * width + LANES:3 * width + 2 * LANES]
    lo_half, hi_half = _half_masks()
    lora = jnp.where(lo_half, jnp.tanh(lora), lora)
    wa = _dot(lora.astype(BF16), wlo_ref[...])
    w0, a0, k_k, k_a = vec_ref[0:1, :], vec_ref[1:2, :], vec_ref[2:3, :], vec_ref[3:4, :]
    r_k, ln_w, ln_b = vec_ref[4:5, :], vec_ref[5:6, :], vec_ref[6:7, :]
    wlog = -_softplus(-(w0 + wa[:, :width])) - 0.5
    logw = -jnp.exp(wlog)
    asig = jax.nn.sigmoid(a0 + wa[:, width:])
    g = _dot(jax.nn.sigmoid(lg).astype(BF16), g2_ref[...])
    yield

    seg = _ones_where(_same_block(LANES, HEAD))

    def headsum(t):
        return jnp.concatenate([_dot_exact_rhs(_tile(t, p), seg) for p in range(npair)], axis=1)

    kk = k * k_k
    kk = kk / jnp.maximum(jnp.sqrt(headsum(kk * kk)), 1e-12)
    kh = k * (1.0 + (asig - 1.0) * k_a)
    a_vec = -kk
    b_vec = kk * asig
    yield

    incl, strict = _chunk_masks(chunk)
    cum, cum_c = _chunk_cumsum(logw, incl, chunk)
    e_in = jnp.exp(cum)
    e_out = jnp.exp(-cum)
    e_end = jnp.exp(cum_c - cum)
    yield
    rt = r * e_in
    at = a_vec * jnp.exp(cum - logw)
    bt = b_vec * e_out
    kt = kh * e_out
    bp = b_vec * e_end
    kp = kh * e_end
    e_c = jnp.exp(cum_c)
    yield

    def msk(t, h):
        keep = lo_half if h == 0 else hi_half
        if isinstance(t, tuple):
            return tuple(jnp.where(keep, x, jnp.zeros_like(x)) for x in t)
        return jnp.where(keep, t, 0.0)

    def cat(parts, axis):
        return tuple(jnp.concatenate([part[i] for part in parts], axis=axis) for i in range(2))

    by_head = lambda t: cat([msk(t, 0), msk(t, 1)], 0)

    eye = jnp.where(_iota2((TB, TB), 0) == _iota2((TB, TB), 1), 1.0, 0.0)
    pairs = range(npair)
    heads = [(p, h) for p in pairs for h in range(2)]
    at_t, rt_t, bt_t, kt_t, v_t, bp_t, kp_t, ec_t = ([_tile(t, p) for p in pairs]
                                                     for t in (at, rt, bt, kt, v, bp, kp, e_c))
    at_s = [_split(t) for t in at_t]
    v_h = [by_head(_split(t)) for t in v_t]
    bk = [cat([by_head(_split(bt_t[p])), by_head(_split(kt_t[p]))], 0) for p in pairs]
    sc_a = [_dot3(at_s[p], bk[p], NT) for p in pairs]
    sc_r = [_dot1(rt_t[p], bk[p], NT) for p in pairs]
    a_ab = [jnp.where(strict, sc_a[p][:, h * TB:(h + 1) * TB], 0.0) for p, h in heads]
    a_ak = [jnp.where(strict, sc_a[p][:, (2 + h) * TB:(3 + h) * TB], 0.0) for p, h in heads]
    p_rb = [jnp.where(incl, sc_r[p][:, h * TB:(h + 1) * TB], 0.0) for p, h in heads]
    p_rk = [jnp.where(incl, sc_r[p][:, (2 + h) * TB:(3 + h) * TB], 0.0) for p, h in heads]
    yield
    base = min(chunk, INV_BLOCK)
    assert chunk in (base, 2 * base)
    in_base = _same_block(TB, base)
    n_in = [jnp.where(in_base, n, 0.0) for n in a_ab]
    t_inv = [eye + n for n in n_in]
    levels = int(math.log2(base))
    splits = lambda mats: [_split(mat) for mat in mats]
    powers = [_dot3(n_s, n_s) for n_s in splits(n_in)] if levels > 1 else n_in
    yield
    for level in range(1, levels):
        x_s, t_s = splits(powers), splits(t_inv)
        if level + 1 < levels:
            both = [_dot3(xs, cat([xs, ts], 1)) for xs, ts in zip(x_s, t_s)]
            powers = [b[:, :TB] for b in both]
            t_inv = [t + b[:, TB:] for t, b in zip(t_inv, both)]
        else:
            t_inv = [t + _dot3(xs, ts) for t, xs, ts in zip(t_inv, x_s, t_s)]
        yield
    if chunk > base:
        t_s = splits(t_inv)
        n_off = [_dot3(n - ni, ts) for n, ni, ts in zip(a_ab, n_in, t_s)]
        yield
        t_inv = [t + _dot3(ts, no) for t, ts, no in zip(t_inv, t_s, n_off)]
    yield
    av = [_dot3(jnp.concatenate(a_ak[2 * p:2 * p + 2], axis=1), v_h[p]) for p in pairs]
    av_s = splits(av)
    tay = [_dot3(jnp.concatenate(t_inv[2 * p:2 * p + 2], axis=1),
                 cat([cat([msk(at_s[p], h), msk(av_s[p], h)], 1) for h in range(2)], 0))
           for p in pairs]
    yield
    ta = [t[:, :LANES] for t in tay]
    yy = [t[:, LANES:] for t in tay]
    qz = [_dot1(jnp.concatenate(p_rb[2 * p:2 * p + 2], axis=1),
                jnp.concatenate([jnp.concatenate([msk(ta[p], h), msk(yy[p], h)], axis=1) for h in range(2)], axis=0))
          for p in pairs]
    zv = [_dot1(jnp.concatenate(p_rk[2 * p:2 * p + 2], axis=1), v_h[p]) for p in pairs]
    qq = [rt_t[p] + qz[p][:, :LANES] for p in pairs]
    zz = [qz[p][:, LANES:] + zv[p] for p in pairs]

    bd = _same_block(LANES, HEAD)
    diag = _iota2((LANES, LANES), 0) == _iota2((LANES, LANES), 1)
    rows = lambda t, c: t[c * chunk:(c + 1) * chunk]
    m_mat = [[jnp.where(diag, ec_t[p][c * chunk:c * chunk + 1, :], 0.0)
              + jnp.where(bd, _dot3(rows(ta[p], c), rows(bp_t[p], c), TN), 0.0) for p in pairs]
             for c in range(nchunk)]
    yield
    n_mat = [[jnp.where(bd, _dot3(jnp.concatenate([rows(yy[p], c), rows(v_t[p], c)], axis=0),
                                  jnp.concatenate([rows(bp_t[p], c), rows(kp_t[p], c)], axis=0), TN), 0.0)
              for p in pairs] for c in range(nchunk)]
    yield
    state = [s_scr[p] for p in pairs] if chained else None
    o_rows = [[] for _ in pairs]
    for c in range(nchunk):
        for p in pairs:
            s = _split(state[p] if chained else _wkv_pair_load(s0_ref, (c,), p))
            o_rows[p].append(_dot1(rows(qq[p], c), s, NT) + rows(zz[p], c))
            s = _dot3(s, m_mat[c][p]) + n_mat[c][p]
            if chained:
                state[p] = s
            else:
                _wkv_pair_store(sout_ref, (c,), p, s)
        if chained:
            yield
    if chained:
        for p in pairs:
            s_scr[p] = state[p]

        @pl.when(pl.program_id(1) == pl.num_programs(1) - 1)
        def _():
            for p in pairs:
                _wkv_pair_store(sout_ref, (), p, state[p])
    o = jnp.concatenate([jnp.concatenate(o_rows[p], axis=0) for p in pairs], axis=1)
    yield

    inv_n = 1.0 / HEAD
    mean = headsum(o) * inv_n
    d = o - mean
    var = headsum(d * d) * inv_n
    o = d * lax.rsqrt(var + 1e-5 * HEAD) * ln_w + ln_b
    o = o + headsum(r * kh * r_k) * v
    z_ref[...] = o * g


def _gla_pair_load(ref, idx, p):
    return jnp.concatenate([ref[idx + (2 * p,)], ref[idx + (2 * p + 1,)]], axis=0).T


def _gla_pair_store(ref, idx, p, s):
    st = s.T
    dk = st.shape[0] // 2
    ref[idx + (2 * p,)] = st[:dk]
    ref[idx + (2 * p + 1,)] = st[dk:]


def _gla_body(chunk, chained, start, qkv_ref, lga_ref, og_ref, s0_ref, wg2_ref, bg_ref, nw_ref,
              z_ref, sout_ref, s_scr):
    nchunk = TB // chunk
    vw = z_ref.shape[-1]
    nhead = vw // LANES
    npair = nhead // 2
    kw = npair * LANES
    qkv = qkv_ref[...]
    q = qkv[:, 0:kw] * (HEAD ** -0.5)
    k = qkv[:, kw:2 * kw]
    v = qkv[:, 2 * kw:2 * kw + vw]
    gl = _dot(lga_ref[...].astype(BF16), wg2_ref[...]) + bg_ref[...]
    log_a = -_softplus(-gl) * (1.0 / GLA_GATE_TEMP)

    incl, _ = _chunk_masks(chunk)
    cum, cum_c = _chunk_cumsum(log_a, incl, chunk)
    qt = q * jnp.exp(cum)
    kt = k * jnp.exp(-cum)
    kp = k * jnp.exp(cum_c - cum)
    e_c = jnp.exp(cum_c)
    yield
    lo_half, hi_half = _half_masks()

    def msk(t, h):
        return jnp.where(lo_half if h == 0 else hi_half, t, 0.0)

    if chained and start is not None:
        @pl.when(start)
        def _():
            for p in range(npair):
                s_scr[p] = _gla_pair_load(s0_ref, (), p)

    pairs = range(npair)
    heads = [(p, h) for p in pairs for h in range(2)]
    rows = lambda t, c: t[c * chunk:(c + 1) * chunk]
    kt_t, kp_t, ec_t = ([_tile(t, p) for p in pairs] for t in (kt, kp, e_c))
    v_t = [_tile(v, h) for h in range(nhead)]
    qm = [msk(_tile(qt, p), h) for p, h in heads]
    sc = [_dot1(jnp.concatenate(qm[2 * p:2 * p + 2], axis=0), kt_t[p], NT) for p in pairs]
    intra = [_dot1(jnp.where(incl, sc[p][h * TB:(h + 1) * TB], 0.0), v_t[2 * p + h]) for p, h in heads]
    upd = [[jnp.where(lo_half, _dot3(rows(v_t[2 * p], c), rows(kp_t[p], c), TN),
                      _dot3(rows(v_t[2 * p + 1], c), rows(kp_t[p], c), TN)) for p in pairs]
           for c in range(nchunk)]
    yield
    state = [s_scr[p] for p in pairs] if chained else None
    inter = [[] for _ in range(nhead)]
    for c in range(nchunk):
        for p in pairs:
            s = state[p] if chained else _gla_pair_load(s0_ref, (c,), p)
            io = _dot3(jnp.concatenate([rows(qm[2 * p], c), rows(qm[2 * p + 1], c)], axis=0), s, NT)
            inter[2 * p].append(io[:chunk])
            inter[2 * p + 1].append(io[chunk:])
            s = s * ec_t[p][c * chunk:c * chunk + 1, :] + upd[c][p]
            if chained:
                state[p] = s
            else:
                _gla_pair_store(sout_ref, (c,), p, s)
    if chained:
        for p in pairs:
            s_scr[p] = state[p]

        @pl.when(pl.program_id(1) == pl.num_programs(1) - 1)
        def _():
            for p in pairs:
                _gla_pair_store(sout_ref, (), p, state[p])
    o_heads = [intra[h] + jnp.concatenate(inter[h], axis=0) for h in range(nhead)]
    yield

    og = og_ref[...]
    for h in range(nhead):
        o = o_heads[h]
        o = o * lax.rsqrt(jnp.mean(o * o, axis=-1, keepdims=True) + NORM_EPS) * nw_ref[...]
        z_ref[:, h * LANES:(h + 1) * LANES] = o * jax.nn.silu(_tile(og, h))


def _mixers_body(chunk, chained, x_ref, prev_ref, wkv0_ref, qkv_ref, lga_ref, og_ref, gla0_ref,
                 mu_ref, vec_ref, wlo_ref, g2_ref, wg2_ref, bg_ref, nw_ref,
                 za_ref, wkv_ref, zb_ref, gla_ref, carry_ref, wkv_scr, gla_scr):
    for blk in range(x_ref.shape[0] // TB):
        rows = lambda ref: ref.at[pl.ds(blk * TB, TB)]
        start = (pl.program_id(1) == 0) if blk == 0 else None
        rwkv = _rwkv_body(chunk, chained, start, rows(x_ref), prev_ref, wkv0_ref, mu_ref, vec_ref, wlo_ref, g2_ref,
                          rows(za_ref), wkv_ref, carry_ref, wkv_scr)
        gla = _gla_body(chunk, chained, start, rows(qkv_ref), rows(lga_ref), rows(og_ref), gla0_ref, wg2_ref, bg_ref,
                        nw_ref, rows(zb_ref), gla_ref, gla_scr)
        for _ in itertools.zip_longest(rwkv, gla):
            pass


def _mixers(x_rw, prev, wkv0, qkv, lga, og, gla0, w, seq_len):
    width = wkv0.shape[1] * wkv0.shape[2]
    vw = og.shape[-1]
    cols = x_rw.shape[-1]
    chained = seq_len > TB
    if chained:
        chunk = PROMPT_CHUNK
        b, t, _ = x_rw.shape
        assert t % MIX_ROWS == 0
        grid = (b, t // MIX_ROWS)
        rows = lambda n: pl.BlockSpec((None, MIX_ROWS, n), lambda i, j: (i, j, 0))
        prev_spec = pl.BlockSpec((None, 1, cols), lambda i, j: (i, 0, 0))
        st = lambda a: pl.BlockSpec((None,) + a.shape[1:], lambda i, j: (i, 0, 0, 0))
        lead = (b, t)
    else:
        chunk = seq_len
        m = x_rw.shape[1]
        assert m % TB == 0
        grid = (m // TB, 1)
        rows = lambda n: pl.BlockSpec((None, TB, n), lambda i, j: (0, i, 0))
        prev_spec = pl.BlockSpec((TB // chunk, 1, cols), lambda i, j: (i, 0, 0))
        st = lambda a: pl.BlockSpec((TB // chunk,) + a.shape[1:], lambda i, j: (i, 0, 0, 0))
        lead = (1, m)
    const = lambda a: pl.BlockSpec(a.shape, lambda i, j: (0,) * a.ndim)
    consts = [w["mu_shift"], w["rwkv_vec"], w["rwkv_wlo"], w["rwkv_g2"], w["gla_wg2"], w["gla_bg"], w["gla_norm_w"]]
    return pl.pallas_call(
        functools.partial(_mixers_body, chunk, chained),
        grid=grid,
        in_specs=[rows(cols), prev_spec, st(wkv0), rows(qkv.shape[-1]), rows(lga.shape[-1]), rows(vw), st(gla0)]
        + [const(a) for a in consts],
        out_specs=[rows(width), st(wkv0), rows(vw), st(gla0)],
        out_shape=[jax.ShapeDtypeStruct(lead + (width,), F32), jax.ShapeDtypeStruct(wkv0.shape, F32),
                   jax.ShapeDtypeStruct(lead + (vw,), F32), jax.ShapeDtypeStruct(gla0.shape, F32)],
        scratch_shapes=[pltpu.VMEM((1, cols), F32), pltpu.VMEM((wkv0.shape[1] // 2, LANES, LANES), F32),
                        pltpu.VMEM((gla0.shape[1] // 2, LANES, LANES), F32)],
        compiler_params=pltpu.CompilerParams(dimension_semantics=("parallel", "arbitrary"),
                                             vmem_limit_bytes=VMEM_LIMIT),
        name="mixers",
    )(x_rw, prev, wkv0, qkv, lga, og, gla0, *consts)


def _post_body(chained, seq_len, x_ref, za_ref, zb_ref, gate_ref, aux_ref, woa_ref, wob_ref, wo_ref, g_ref,
               wup_ref, cw_ref, cb_ref, wd_ref, nf_ref, y_ref, tail_ref, carry_ref, act_ref):
    tm, d = x_ref.shape
    hidden = wd_ref.shape[0]
    ya = _dot(za_ref[...].astype(BF16), woa_ref[...])
    yb = _dot(zb_ref[...].astype(BF16), wob_ref[...])
    gate = gate_ref[...]
    merged = jax.nn.sigmoid(gate[:, :d]) * ya + jax.nn.sigmoid(gate[:, d:]) * yb
    x1 = x_ref[...] + _dot(merged.astype(BF16), wo_ref[...])
    hb = _rmsnorm(x1, g_ref[...]).astype(BF16)

    if chained:
        @pl.when(pl.program_id(1) == 0)
        def _():
            carry_ref[0:6, :] = jnp.zeros((6, carry_ref.shape[1]), F32)
            carry_ref[6:8, :] = aux_ref[...]
    else:
        t = _iota2((1, seq_len, 1), 1)

    def conv(u, cs):
        if chained:
            ext = jnp.concatenate([carry_ref[:, cs], u], axis=0)
            carry_ref[:, cs] = u[tm - 8:, :]
            tail_ref[:, cs] = u[tm - 8:, :]
            inner = cw_ref[1:2, cs] * ext + pltpu.roll(cw_ref[0:1, cs] * ext, 1, axis=0)
            return (cb_ref[:, cs] + cw_ref[2:3, cs] * ext + pltpu.roll(inner, 1, axis=0))[8:]
        else:
            st = aux_ref[:, :, cs]
            by_seq = lambda a: a.reshape(tm // seq_len, seq_len, -1)
            p1 = jnp.where(t == 0, st[:, 1:2, :], by_seq(pltpu.roll(u, 1, axis=0))).reshape(tm, -1)
            p2 = jnp.where(t == 0, st[:, 0:1, :],
                           jnp.where(t == 1, st[:, 1:2, :], by_seq(pltpu.roll(u, 2, axis=0)))).reshape(tm, -1)
            tail_ref[:, :, cs] = by_seq(u)[:, seq_len - 2:, :]
        return cb_ref[:, cs] + cw_ref[0:1, cs] * p2 + cw_ref[1:2, cs] * p1 + cw_ref[2:3, cs] * u

    def up(j0):
        cols = (slice(j0, j0 + FFN_CHUNK), slice(hidden + j0, hidden + j0 + FFN_CHUNK))
        return [(_dot(hb, wup_ref[:, cs]), cs) for cs in cols]

    starts = list(range(0, hidden, FFN_CHUNK))
    half = starts[len(starts) // 2] + FFN_CHUNK
    out = x1
    nxt = up(starts[0])
    for i, j0 in enumerate(starts):
        cur = nxt
        if i + 1 < len(starts):
            nxt = up(starts[i + 1])
        val, gat = (conv(u, cs) for u, cs in cur)
        act_ref[:, j0:j0 + FFN_CHUNK] = (jax.nn.gelu(gat) * val).astype(BF16)
        if j0 + FFN_CHUNK == half:
            out = out + _dot(act_ref[:, :half], wd_ref[:half, :])
    y_ref[...] = _rmsnorm(out + _dot(act_ref[:, half:], wd_ref[half:, :]), nf_ref[...])


def _post(x, za, zb, gate, conv_state, w, seq_len):
    b, t, d = x.shape
    m = b * t
    f2 = conv_state.shape[-1]
    chained = seq_len >= POST_TM
    assert seq_len % POST_TM == 0 if chained else POST_TM_SHORT % seq_len == 0
    weights = [w["w_out_a"], w["w_out_b"], w["w_o"], w["norm_ffn"], w["ffn_w_up"], w["ffn_conv_w"],
               w["ffn_conv_b"], w["ffn_w_down"], w["norm_final"]]
    const = lambda a: pl.BlockSpec(a.shape, lambda i, j: (0,) * a.ndim, pipeline_mode=pl.Buffered(1))
    if chained:
        tm = POST_TM
        grid = (b, t // tm)
        rows = lambda a: pl.BlockSpec((None, tm, a.shape[-1]), lambda i, j: (i, j, 0))
        acts = [x, za.reshape(b, t, -1), zb.reshape(b, t, -1), gate.reshape(b, t, -1)]
        aux, aux_spec = conv_state, pl.BlockSpec((None, 2, f2), lambda i, j: (i, 0, 0))
        tail_spec = pl.BlockSpec((None, 8, f2), lambda i, j: (i, 0, 0))
        tail_shape = (b, 8, f2)
    else:
        tm = POST_TM_SHORT
        grid = (m // tm, 1)
        rows = lambda a: pl.BlockSpec((tm, a.shape[-1]), lambda i, j: (i, 0))
        acts = [x.reshape(m, d), za, zb, gate]
        aux = conv_state
        aux_spec = tail_spec = pl.BlockSpec((tm // seq_len, 2, f2), lambda i, j: (i, 0, 0))
        tail_shape = (b, 2, f2)
    y, tail = pl.pallas_call(
        functools.partial(_post_body, chained, seq_len),
        grid=grid,
        in_specs=[rows(a) for a in acts] + [aux_spec] + [const(a) for a in weights],
        out_specs=[rows(acts[0]), tail_spec],
        out_shape=[jax.ShapeDtypeStruct(acts[0].shape, F32), jax.ShapeDtypeStruct(tail_shape, F32)],
        scratch_shapes=[pltpu.VMEM((8, f2), F32), pltpu.VMEM((tm, f2 // 2), BF16)],
        compiler_params=pltpu.CompilerParams(dimension_semantics=("parallel", "arbitrary"),
                                             vmem_limit_bytes=VMEM_LIMIT),
        name="post_mixer",
    )(*acts, aux, *weights)
    new_conv = tail[:, -2:]
    return y.reshape(b, t, d), new_conv


def _trunk(x, shift0, wkv0, gla0, conv0, w):
    b, t, d = x.shape
    m = b * t
    xf = x.reshape(m, d)
    p_rw, p_qkv, p_og, p_gate, p_lga = _norm_proj(xf, w["norm_mix"], w["w_in_parts"])
    shift_cols = p_rw.shape[1]
    new_shift = p_rw.reshape(b, t, shift_cols)[:, -1]

    if t > TB:
        shape3 = lambda a: a.reshape(b, t, a.shape[1])
        prev = shift0[:, None, :]
    else:
        shape3 = lambda a: a[None]
        prev = shift0[:, None, :]
    z_a, wkv_new, z_b, gla_new = _mixers(shape3(p_rw), prev, wkv0, shape3(p_qkv), shape3(p_lga), shape3(p_og),
                                          gla0, w, t)
    z_a = z_a.reshape(m, z_a.shape[-1])
    z_b = z_b.reshape(m, z_b.shape[-1])

    y, new_conv = _post(x, z_a, z_b, p_gate, conv0, w, t)
    return (y, new_shift[None], wkv_new[None], gla_new[None], new_conv[None])


def kernel(x_prompt, x_sample, state_rwkv_shift, state_rwkv_wkv, state_gla, state_ffn_conv, norm_mix, w_in, mu_shift, rwkv_w0, rwkv_w2, rwkv_a0, rwkv_a2, rwkv_g2, rwkv_k_k, rwkv_k_a, rwkv_r_k, rwkv_ln_w, rwkv_ln_b, gla_wg2, gla_bg, gla_norm_w, w_out_a, w_out_b, w_o, norm_ffn, ffn_w_up, ffn_conv_w, ffn_conv_b, ffn_w_down, norm_final):
    assert norm_mix.shape[0] == 1, "single-layer step"
    d = x_prompt.shape[-1]
    width = rwkv_w0.shape[-1]
    shift_cols = mu_shift.shape[-1]
    kw = gla_wg2.shape[-1]
    vw = w_out_b.shape[1]
    lora_g = gla_wg2.shape[1]
    lw, la = rwkv_w2.shape[1], rwkv_a2.shape[1]
    assert lw == HEAD and la == HEAD and rwkv_g2.shape[1] == LANES

    win = w_in[0].astype(BF16)
    c0 = shift_cols
    c1 = c0 + 2 * kw + vw
    c2 = c1 + lora_g
    c3 = c2 + vw
    w_lga = jnp.pad(win[:, c1:c2], ((0, 0), (0, LANES - lora_g)))
    w_in_parts = [win[:, :c0], win[:, c0:c1], win[:, c2:c3], win[:, c3:], w_lga]

    zw = jnp.zeros((lw, width), BF16)
    wlo = jnp.concatenate([jnp.concatenate([rwkv_w2[0].astype(BF16), zw], axis=1),
                           jnp.concatenate([zw, rwkv_a2[0].astype(BF16)], axis=1)], axis=0)
    vec = jnp.stack([rwkv_w0[0], rwkv_a0[0], rwkv_k_k[0], rwkv_k_a[0], rwkv_r_k[0].reshape(width),
                     rwkv_ln_w[0], rwkv_ln_b[0], jnp.zeros((width,), F32)])
    w = dict(
        norm_mix=norm_mix, w_in_parts=w_in_parts, mu_shift=mu_shift, rwkv_vec=vec, rwkv_wlo=wlo,
        rwkv_g2=rwkv_g2[0].astype(BF16),
        gla_wg2=jnp.pad(gla_wg2[0].astype(BF16), ((0, LANES - lora_g), (0, 0))), gla_bg=gla_bg,
        gla_norm_w=gla_norm_w,
        w_out_a=w_out_a[0].astype(BF16), w_out_b=w_out_b[0].astype(BF16), w_o=w_o[0].astype(BF16),
        norm_ffn=norm_ffn, ffn_w_up=ffn_w_up[0].astype(BF16), ffn_conv_w=ffn_conv_w[0],
        ffn_conv_b=ffn_conv_b, ffn_w_down=ffn_w_down[0].astype(BF16), norm_final=norm_final[None],
    )

    bp = x_prompt.shape[0]
    dt = x_prompt.dtype
    zeros = lambda s: jnp.zeros((bp,) + s.shape[2:], dt)
    y_p, shift_p, wkv_p, gla_p, conv_p = _trunk(
        x_prompt, zeros(state_rwkv_shift), zeros(state_rwkv_wkv), zeros(state_gla), zeros(state_ffn_conv), w)
    y_s, shift_s, wkv_s, gla_s, conv_s = _trunk(
        x_sample, state_rwkv_shift[0], state_rwkv_wkv[0], state_gla[0], state_ffn_conv[0], w)
    return (y_p, y_s, shift_p, wkv_p, gla_p, conv_p, shift_s, wkv_s, gla_s, conv_s)
```
